```python
import math
import jax, jax.numpy as jnp
from jax import lax
import numpy as np

D_MODEL = 1024
BATCH = 8
SEQ = 2048
DEPTH = 4

CHUNK = 64
N_EVEN = (DEPTH + 1) // 2
N_ODD = DEPTH // 2
EPS = 1e-6

RET_HEADS = 4
RET_DK = D_MODEL // 8
RET_DV = 2 * RET_DK
ROPE_BASE = 10000.0
A_QK = RET_HEADS * RET_DK
A_V = RET_HEADS * RET_DV

SG_GROUPS = 4
SG_CH = D_MODEL // 8
SG_WIDTH = SG_GROUPS * SG_CH
SG_LEN = 128

EVEN_IN = 2 * A_QK + 2 * A_V + 2 * SG_WIDTH
EVEN_MIX = A_V + SG_WIDTH

ATT_HEADS = 16
ATT_DH = D_MODEL // ATT_HEADS
BAND_CHUNKS = 9
BAND_LEN = BAND_CHUNKS * CHUNK
MAX_REL = 256
REL_SIZE = (CHUNK - 1) + MAX_REL + 1

N_EXPERTS = 32
TOP_K = 4
D_FF = D_MODEL
SWIGLU_ALPHA = 1.702
SWIGLU_LIMIT = 7.0
MOE_BLOCK = 128

kernel_name = "hybrid_retention_gmlp_bandattn_moe"


def rms_norm(x, g):
    x32 = x.astype(jnp.float32)
    y = x32 * lax.rsqrt(jnp.mean(jnp.square(x32), axis=-1, keepdims=True) + EPS)
    return (y * g.astype(jnp.float32)).astype(x.dtype)


def layer_norm(x, g, b):
    x32 = x.astype(jnp.float32)
    mu = jnp.mean(x32, axis=-1, keepdims=True)
    var = jnp.mean(jnp.square(x32 - mu), axis=-1, keepdims=True)
    y = (x32 - mu) * lax.rsqrt(var + EPS)
    return (y * g.astype(jnp.float32) + b.astype(jnp.float32)).astype(x.dtype)


def head_group_norm(x):
    x32 = x.astype(jnp.float32)
    mu = jnp.mean(x32, axis=-1, keepdims=True)
    var = jnp.mean(jnp.square(x32 - mu), axis=-1, keepdims=True)
    return ((x32 - mu) * lax.rsqrt(var + EPS)).astype(x.dtype)


def rotary(x, pos):
    half = x.shape[-1] // 2
    inv = ROPE_BASE ** (-jnp.arange(half, dtype=jnp.float32) / half)
    ang = pos[:, None] * inv[None, :]
    cos = jnp.cos(ang)[None, :, None, :].astype(x.dtype)
    sin = jnp.sin(ang)[None, :, None, :].astype(x.dtype)
    x1, x2 = x[..., :half], x[..., half:]
    return jnp.concatenate([x1 * cos - x2 * sin, x1 * sin + x2 * cos], axis=-1)


def retention(q, k, v):
    B_, S_, H, dk = q.shape
    dv = v.shape[-1]
    dt = q.dtype
    nC = S_ // CHUNK
    log_g = jnp.log1p(-jnp.exp2(-5.0 - jnp.arange(H, dtype=jnp.float32)))
    idx = jnp.arange(CHUNK, dtype=jnp.float32)
    intra_decay = jnp.exp(log_g[:, None, None] * jnp.abs(idx[:, None] - idx[None, :]))
    q_decay = jnp.exp(log_g[:, None] * (idx[None, :] + 1.0))
    k_decay = jnp.exp(log_g[:, None] * (CHUNK - 1.0 - idx[None, :]))
    chunk_decay = jnp.exp(log_g * CHUNK).astype(dt)

    qc = q.reshape(B_, nC, CHUNK, H, dk)
    kc = k.reshape(B_, nC, CHUNK, H, dk) * (dk ** -0.5)
    vc = v.reshape(B_, nC, CHUNK, H, dv)

    s = jnp.einsum('bnqhd,bnkhd->bhnqk', qc, kc) * intra_decay[:, None].astype(dt)
    intra = jnp.einsum('bhnqk,bnkhe->bnqhe', s, vc)

    kv = jnp.einsum('bnkhd,bnkhe->nbhde',
                    kc * k_decay.T[None, None, :, :, None].astype(dt), vc)

    def step(state, kv_n):
        return state * chunk_decay[None, :, None, None] + kv_n, state

    _, states = lax.scan(step, jnp.zeros((B_, H, dk, dv), dt), kv)
    inter = jnp.einsum('bnqhd,nbhde->bnqhe',
                       qc * q_decay.T[None, None, :, :, None].astype(dt), states)
    return (intra + inter).reshape(B_, S_, H, dv)


def spatial_gate(u, z, ln_g, ln_b, w_s, b_s):
    B_, S_, _ = z.shape
    z = layer_norm(z, ln_g, ln_b)
    nP = S_ // SG_LEN
    cpos = jnp.arange(SG_LEN) // CHUNK
    mask = cpos[:, None] >= cpos[None, :]
    w = jnp.where(mask[None], w_s, jnp.zeros((), w_s.dtype))
    zb = z.reshape(B_, nP, SG_LEN, SG_GROUPS, SG_CH)
    mixed = jnp.einsum('gij,bnjgc->bnigc', w, zb) + b_s.T[None, None, :, :, None]
    return u * mixed.reshape(B_, S_, SG_WIDTH)


def even_mixer(h, w_in, w_out, sg_ln_g, sg_ln_b, sg_w, sg_b):
    B_, S_, _ = h.shape
    proj = h @ w_in
    o1 = A_QK
    o2 = o1 + A_QK
    o3 = o2 + A_V
    o4 = o3 + A_V
    o5 = o4 + SG_WIDTH
    q, k, v, gate, u, z = jnp.split(proj, [o1, o2, o3, o4, o5], axis=-1)
    pos = jnp.arange(S_, dtype=jnp.float32)
    q = rotary(q.reshape(B_, S_, RET_HEADS, RET_DK), pos)
    k = rotary(k.reshape(B_, S_, RET_HEADS, RET_DK), pos)
    v = v.reshape(B_, S_, RET_HEADS, RET_DV)
    ret = head_group_norm(retention(q, k, v)).reshape(B_, S_, A_V)
    a_out = jax.nn.silu(gate) * ret
    b_out = spatial_gate(jax.nn.gelu(u), jax.nn.gelu(z), sg_ln_g, sg_ln_b, sg_w, sg_b)
    return jnp.concatenate([a_out, b_out], axis=-1) @ w_out


def band_attention(q, k, v, q_g, k_g, rel_bias):
    B_, S_, H, Dh = q.shape
    nC = S_ // CHUNK
    pad = (BAND_CHUNKS - 1) * CHUNK
    q = rms_norm(q, q_g)
    k = rms_norm(k, k_g)
    kpad = jnp.pad(k, ((0, 0), (pad, 0), (0, 0), (0, 0)))
    vpad = jnp.pad(v, ((0, 0), (pad, 0), (0, 0), (0, 0)))
    qc = jnp.transpose(q.reshape(B_, nC, CHUNK, H, Dh), (1, 0, 2, 3, 4))

    kpos = jnp.arange(BAND_LEN) - pad
    rel = jnp.arange(CHUNK)[:, None] - kpos[None, :]
    rel_idx = jnp.clip(rel, -(CHUNK - 1), MAX_REL) + (CHUNK - 1)
    bias = rel_bias[:, rel_idx].astype(jnp.float32)
    scale = Dh ** -0.5

    def one_chunk(args):
        n, qn = args
        start = n * CHUNK
        kb = lax.dynamic_slice_in_dim(kpad, start, BAND_LEN, axis=1)
        vb = lax.dynamic_slice_in_dim(vpad, start, BAND_LEN, axis=1)
        s = jnp.einsum('bqhd,bkhd->bhqk', qn, kb).astype(jnp.float32) * scale + bias[None]
        valid = (start + jnp.arange(BAND_LEN)) >= pad
        s = jnp.where(valid[None, None, None, :], s, -1e30)
        p = jax.nn.softmax(s, axis=-1).astype(vb.dtype)
        return jnp.einsum('bhqk,bkhd->bqhd', p, vb)

    o = lax.map(one_chunk, (jnp.arange(nC), qc))
    return jnp.transpose(o, (1, 0, 2, 3, 4)).reshape(B_, S_, H * Dh)


def odd_mixer(h, w_in, w_out, q_g, k_g, rel_bias):
    B_, S_, _ = h.shape
    q, k, v = jnp.split(h @ w_in, 3, axis=-1)
    shp = (B_, S_, ATT_HEADS, ATT_DH)
    o = band_attention(q.reshape(shp), k.reshape(shp), v.reshape(shp), q_g, k_g, rel_bias)
    return o @ w_out


def moe(h, router_w, router_b, w1, b1, w2, b2):
    B_, S_, D = h.shape
    T = B_ * S_
    xt = h.reshape(T, D)
    logits = xt.astype(jnp.float32) @ router_w.astype(jnp.float32) + router_b.astype(jnp.float32)
    top_vals, top_idx = lax.top_k(logits, TOP_K)
    gates = jax.nn.softmax(top_vals, axis=-1)

    n_assign = T * TOP_K
    flat_e = top_idx.reshape(-1)
    flat_tok = jnp.arange(n_assign, dtype=jnp.int32) // TOP_K
    flat_gate = gates.reshape(-1)
    order = jnp.argsort(flat_e)
    sorted_e = flat_e[order]
    counts = jnp.zeros((N_EXPERTS,), jnp.int32).at[flat_e].add(1)
    padded = ((counts + MOE_BLOCK - 1) // MOE_BLOCK) * MOE_BLOCK
    starts = jnp.cumsum(counts) - counts
    pends = jnp.cumsum(padded)
    pstarts = pends - padded
    dest = pstarts[sorted_e] + (jnp.arange(n_assign, dtype=jnp.int32) - starts[sorted_e])

    cap = ((n_assign + MOE_BLOCK - 1) // MOE_BLOCK) * MOE_BLOCK + N_EXPERTS * MOE_BLOCK
    n_blocks = cap // MOE_BLOCK
    row_tok = jnp.zeros((cap,), jnp.int32).at[dest].set(flat_tok[order])
    row_gate = jnp.zeros((cap,), jnp.float32).at[dest].set(flat_gate[order])
    block_start = jnp.arange(n_blocks, dtype=jnp.int32) * MOE_BLOCK
    block_e = jnp.minimum(jnp.searchsorted(pends, block_start, side='right'), N_EXPERTS - 1)
    xs = xt[row_tok].reshape(n_blocks, MOE_BLOCK, D)

    def expert_block(args):
        e, xb = args
        hid = xb @ w1[e] + b1[e]
        x_glu = jnp.minimum(hid[:, :D_FF], SWIGLU_LIMIT)
        x_lin = jnp.clip(hid[:, D_FF:], -SWIGLU_LIMIT, SWIGLU_LIMIT)
        act = x_glu * jax.nn.sigmoid(SWIGLU_ALPHA * x_glu) * (x_lin + 1.0)
        return act @ w2[e] + b2[e]

    ys = lax.map(expert_block, (block_e, xs)).reshape(cap, D)
    y = jnp.zeros((T, D), ys.dtype).at[row_tok].add(ys * row_gate[:, None].astype(ys.dtype))
    return y.reshape(B_, S_, D).astype(h.dtype)


def setup_inputs(seed: int = 0) -> dict:
    key = jax.random.key(seed)
    ks = jax.random.split(key, 24)

    def nrm(k, shape, scale):
        return jax.random.normal(k, shape, jnp.float32) * scale

    return {
        "x": nrm(ks[0], (BATCH, SEQ, D_MODEL), 1.0),
        "c": nrm(ks[1], (BATCH, D_MODEL), 1.0),
        "ada_w": nrm(ks[2], (DEPTH, D_MODEL, 6 * D_MODEL), 0.5 * D_MODEL ** -0.5),
        "ada_b": nrm(ks[3], (DEPTH, 6 * D_MODEL), 0.02),
        "norm_mix_g": 1.0 + nrm(ks[4], (DEPTH, D_MODEL), 0.02),
        "norm_ffn_g": 1.0 + nrm(ks[5], (DEPTH, D_MODEL), 0.02),
        "ev_w_in": nrm(ks[6], (N_EVEN, D_MODEL, EVEN_IN), D_MODEL ** -0.5),
        "ev_w_out": nrm(ks[7], (N_EVEN, EVEN_MIX, D_MODEL), EVEN_MIX ** -0.5),
        "sg_ln_g": 1.0 + nrm(ks[8], (N_EVEN, SG_WIDTH), 0.02),
        "sg_ln_b": nrm(ks[9], (N_EVEN, SG_WIDTH), 0.02),
        "sg_w": nrm(ks[10], (N_EVEN, SG_GROUPS, SG_LEN, SG_LEN), SG_LEN ** -0.5),
        "sg_b": 1.0 + nrm(ks[11], (N_EVEN, SG_GROUPS, SG_LEN), 0.02),
        "od_w_in": nrm(ks[12], (N_ODD, D_MODEL, 3 * D_MODEL), D_MODEL ** -0.5),
        "od_w_out": nrm(ks[13], (N_ODD, D_MODEL, D_MODEL), D_MODEL ** -0.5),
        "od_q_g": 1.0 + nrm(ks[14], (N_ODD, ATT_DH), 0.02),
        "od_k_g": 1.0 + nrm(ks[15], (N_ODD, ATT_DH), 0.02),
        "od_rel_bias": nrm(ks[16], (N_ODD, ATT_HEADS, REL_SIZE), 0.2),
        "moe_router_w": nrm(ks[17], (DEPTH, D_MODEL, N_EXPERTS), D_MODEL ** -0.5),
        "moe_router_b": nrm(ks[18], (DEPTH, N_EXPERTS), 0.01),
        "moe_w1": nrm(ks[19], (DEPTH, N_EXPERTS, D_MODEL, 2 * D_FF), D_MODEL ** -0.5),
        "moe_b1": nrm(ks[20], (DEPTH, N_EXPERTS, 2 * D_FF), 0.01),
        "moe_w2": nrm(ks[21], (DEPTH, N_EXPERTS, D_FF, D_MODEL), D_FF ** -0.5),
        "moe_b2": nrm(ks[22], (DEPTH, N_EXPERTS, D_MODEL), 0.01),
    }


def reference(x, c, ada_w, ada_b, norm_mix_g, norm_ffn_g, ev_w_in, ev_w_out,
              sg_ln_g, sg_ln_b, sg_w, sg_b, od_w_in, od_w_out, od_q_g, od_k_g,
              od_rel_bias, moe_router_w, moe_router_b, moe_w1, moe_b1, moe_w2, moe_b2):
    c_act = jax.nn.silu(c)
    for l in range(DEPTH):
        mod = c_act @ ada_w[l] + ada_b[l]
        sh1, sc1, g1, sh2, sc2, g2 = jnp.split(mod, 6, axis=-1)
        h = rms_norm(x, norm_mix_g[l]) * (1.0 + sc1[:, None]) + sh1[:, None]
        i = l // 2
        if l % 2 == 0:
            mix = even_mixer(h, ev_w_in[i], ev_w_out[i], sg_ln_g[i], sg_ln_b[i], sg_w[i], sg_b[i])
        else:
            mix = odd_mixer(h, od_w_in[i], od_w_out[i], od_q_g[i], od_k_g[i], od_rel_bias[i])
        x = x + g1[:, None] * mix
        h = rms_norm(x, norm_ffn_g[l]) * (1.0 + sc2[:, None]) + sh2[:, None]
        x = x + g2[:, None] * moe(h, moe_router_w[l], moe_router_b[l], moe_w1[l], moe_b1[l],
                                  moe_w2[l], moe_b2[l])
    return x
```

```python
import functools
import math

import numpy as np
import jax
import jax.numpy as jnp
from jax import lax
from jax.experimental import pallas as pl
from jax.experimental.pallas import tpu as pltpu

D_MODEL = 1024
BATCH = 8
SEQ = 2048
DEPTH = 4
TOKENS = BATCH * SEQ
CHUNK = 64
EPS = 1e-6

RET_HEADS = 4
RET_DK = 128
RET_DV = 256
ROPE_BASE = 10000.0
A_QK = RET_HEADS * RET_DK
A_V = RET_HEADS * RET_DV
SG_GROUPS = 4
SG_CH = 128
SG_WIDTH = SG_GROUPS * SG_CH
SG_LEN = 128
EVEN_IN = 2 * A_QK + 2 * A_V + 2 * SG_WIDTH
EVEN_MIX = A_V + SG_WIDTH

ATT_HEADS = 16
ATT_DH = 64
BAND_PAD = 8 * CHUNK
MAX_REL = 256
REL_SIZE = (CHUNK - 1) + MAX_REL + 1

N_EXPERTS = 32
TOP_K = 4
D_FF = D_MODEL
SWIGLU_ALPHA = 1.702
SWIGLU_LIMIT = 7.0

ROW_TILE = 512
COL_CHUNK = 512
RET_BLOCK = 256
ATT_QBLOCK = 128
ATT_BAND = ATT_QBLOCK + BAND_PAD
ROUTE_TILE = 512
MOE_BLOCK = 256
MOE_CAP = TOKENS * TOP_K + N_EXPERTS * MOE_BLOCK
MOE_NBLOCKS = MOE_CAP // MOE_BLOCK
DISPATCH_TILE = 256
VMEM_LIMIT_V7X = 56 * 1024 * 1024

NEG_BIG = -1e30


def _silu(x):
    return x * (1.0 / (1.0 + jnp.exp(-x)))


def _gelu_tanh(x):
    return 0.5 * x * (1.0 + jnp.tanh(math.sqrt(2.0 / math.pi) * (x + 0.044715 * (x * x * x))))


def _bf16(x):
    return x.astype(jnp.bfloat16)


def _dot(a, b):
    return jnp.dot(a, b, preferred_element_type=jnp.float32)


def _dot_nt(a, b):
    return lax.dot_general(a, b, (((1,), (1,)), ((), ())), preferred_element_type=jnp.float32)


def _dot_tn(a, b):
    return lax.dot_general(a, b, (((0,), (0,)), ((), ())), preferred_element_type=jnp.float32)


def _norm_mod(x, g, sc, sh):
    y = x * lax.rsqrt(jnp.mean(x * x, axis=-1, keepdims=True) + EPS)
    return (y * g) * (1.0 + sc) + sh


def _ada_kernel(c_ref, w_ref, b_ref, o_ref):
    c_act = _silu(c_ref[...])
    o_ref[0] = _dot(c_act, w_ref[0]) + b_ref[0]


def _ada_mod(c, ada_w, ada_b):
    n_col = 6
    return pl.pallas_call(
        _ada_kernel,
        grid=(DEPTH, n_col),
        in_specs=[
            pl.BlockSpec((BATCH, D_MODEL), lambda l, j: (0, 0)),
            pl.BlockSpec((1, D_MODEL, D_MODEL), lambda l, j: (l, 0, j)),
            pl.BlockSpec((1, 1, D_MODEL), lambda l, j: (l, 0, j)),
        ],
        out_specs=pl.BlockSpec((1, BATCH, D_MODEL), lambda l, j: (l, 0, j)),
        out_shape=jax.ShapeDtypeStruct((DEPTH, BATCH, 6 * D_MODEL), jnp.float32),
        name="ada_mod",
    )(c, ada_w, ada_b.reshape(DEPTH, 1, 6 * D_MODEL))


def _norm_proj_kernel(x_ref, g_ref, sc_ref, sh_ref, w_ref, o_ref):
    h = _bf16(_norm_mod(x_ref[...], g_ref[...], sc_ref[0], sh_ref[0]))
    n_out = o_ref.shape[1]
    for j in range(n_out // COL_CHUNK):
        cols = slice(j * COL_CHUNK, (j + 1) * COL_CHUNK)
        o_ref[:, cols] = _bf16(_dot(h, w_ref[:, cols]))


def _norm_proj(x, g, sc, sh, w_bf16):
    n_out = w_bf16.shape[1]
    tiles_per_batch = SEQ // ROW_TILE
    return pl.pallas_call(
        _norm_proj_kernel,
        grid=(TOKENS // ROW_TILE,),
        in_specs=[
            pl.BlockSpec((ROW_TILE, D_MODEL), lambda i: (i, 0)),
            pl.BlockSpec((1, D_MODEL), lambda i: (0, 0)),
            pl.BlockSpec((1, 1, D_MODEL), lambda i: (i // tiles_per_batch, 0, 0)),
            pl.BlockSpec((1, 1, D_MODEL), lambda i: (i // tiles_per_batch, 0, 0)),
            pl.BlockSpec((D_MODEL, n_out), lambda i: (0, 0)),
        ],
        out_specs=pl.BlockSpec((ROW_TILE, n_out), lambda i: (i, 0)),
        out_shape=jax.ShapeDtypeStruct((TOKENS, n_out), jnp.bfloat16),
        compiler_params=pltpu.CompilerParams(vmem_limit_bytes=VMEM_LIMIT_V7X),
        name="norm_proj",
    )(x, g.reshape(1, D_MODEL), sc, sh, w_bf16)


def _out_proj_kernel(*refs, widths):
    part_refs = refs[:len(widths)]
    w_ref, x_ref, g_ref, o_ref = refs[len(widths):]
    for j in range(D_MODEL // COL_CHUNK):
        cols = slice(j * COL_CHUNK, (j + 1) * COL_CHUNK)
        acc = None
        row0 = 0
        for p_ref, width in zip(part_refs, widths):
            term = _dot(p_ref[...], w_ref[row0:row0 + width, cols])
            acc = term if acc is None else acc + term
            row0 += width
        o_ref[:, cols] = x_ref[:, cols] + g_ref[0][:, cols] * acc


def _out_proj(parts, w_bf16, x, gate):
    widths = tuple(p.shape[1] for p in parts)
    tiles_per_batch = SEQ // ROW_TILE
    in_specs = [pl.BlockSpec((ROW_TILE, width), lambda i: (i, 0)) for width in widths]
    in_specs += [
        pl.BlockSpec((sum(widths), D_MODEL), lambda i: (0, 0)),
        pl.BlockSpec((ROW_TILE, D_MODEL), lambda i: (i, 0)),
        pl.BlockSpec((1, 1, D_MODEL), lambda i: (i // tiles_per_batch, 0, 0)),
    ]
    return pl.pallas_call(
        functools.partial(_out_proj_kernel, widths=widths),
        grid=(TOKENS // ROW_TILE,),
        in_specs=in_specs,
        out_specs=pl.BlockSpec((ROW_TILE, D_MODEL), lambda i: (i, 0)),
        out_shape=jax.ShapeDtypeStruct((TOKENS, D_MODEL), jnp.float32),
        compiler_params=pltpu.CompilerParams(vmem_limit_bytes=VMEM_LIMIT_V7X),
        name="out_proj",
    )(*parts, w_bf16, x, gate)


def _retention_tables():
    heads = np.arange(RET_HEADS, dtype=np.float64)
    log_g = np.log1p(-np.exp2(-5.0 - heads))
    idx = np.arange(RET_BLOCK, dtype=np.float64)
    diff = idx[:, None] - idx[None, :]
    ci, cj = (idx // CHUNK)[:, None], (idx // CHUNK)[None, :]
    expo = np.where(ci == cj, np.abs(diff), diff)
    decay = np.where(cj <= ci, np.exp(log_g[:, None, None] * expo[None]), 0.0)
    q_dec = np.exp(log_g[:, None] * (idx[None, :] + 1.0))
    k_dec = np.exp(log_g[:, None] * (RET_BLOCK - 1.0 - idx[None, :]))
    blk_dec = np.exp(log_g * RET_BLOCK)
    q_dec = np.broadcast_to(q_dec[:, :, None], (RET_HEADS, RET_BLOCK, RET_DK))
    k_dec = np.broadcast_to(k_dec[:, :, None], (RET_HEADS, RET_BLOCK, RET_DK))
    blk_dec = np.broadcast_to(blk_dec[:, None, None], (RET_HEADS, 1, RET_DV))
    half = RET_DK // 2
    inv = ROPE_BASE ** (-np.arange(half, dtype=np.float64) / half)
    ang = np.arange(SEQ, dtype=np.float64)[:, None] * inv[None, :]
    cos = np.concatenate([np.cos(ang), np.cos(ang)], axis=1)
    sin = np.concatenate([-np.sin(ang), np.sin(ang)], axis=1)
    f32 = lambda a: jnp.asarray(np.ascontiguousarray(a), jnp.float32)
    return f32(decay), f32(q_dec), f32(k_dec), f32(blk_dec), f32(cos), f32(sin)


def _retention_kernel(q_ref, k_ref, v_ref, gate_ref, cos_ref, sin_ref, dec_ref, qd_ref, kd_ref,
                      bd_ref, o_ref, state_ref):
    @pl.when(pl.program_id(2) == 0)
    def _():
        state_ref[...] = jnp.zeros_like(state_ref)

    cos, sin = cos_ref[...], sin_ref[...]
    q = q_ref[...].astype(jnp.float32)
    k = k_ref[...].astype(jnp.float32)
    qr = q * cos + pltpu.roll(q, RET_DK // 2, axis=1) * sin
    kr = (k * cos + pltpu.roll(k, RET_DK // 2, axis=1) * sin) * (RET_DK ** -0.5)
    v = v_ref[...]

    scores = _dot_nt(_bf16(qr), _bf16(kr)) * dec_ref[0]
    intra = _dot(_bf16(scores), v)
    state = state_ref[...]
    inter = _dot(_bf16(qr * qd_ref[0]), _bf16(state))
    state_ref[...] = state * bd_ref[0] + _dot_tn(_bf16(kr * kd_ref[0]), v)

    o = intra + inter
    mu = jnp.mean(o, axis=-1, keepdims=True)
    cen = o - mu
    var = jnp.mean(cen * cen, axis=-1, keepdims=True)
    normed = cen * lax.rsqrt(var + EPS)
    o_ref[...] = _bf16(_silu(gate_ref[...].astype(jnp.float32)) * normed)


def _retention(proj):
    decay, q_dec, k_dec, blk_dec, cos, sin = _retention_tables()
    nblk = SEQ // RET_BLOCK
    row = lambda b, h, n: b * nblk + n
    return pl.pallas_call(
        _retention_kernel,
        grid=(BATCH, RET_HEADS, nblk),
        in_specs=[
            pl.BlockSpec((RET_BLOCK, RET_DK), lambda b, h, n: (row(b, h, n), h)),
            pl.BlockSpec((RET_BLOCK, RET_DK), lambda b, h, n: (row(b, h, n), A_QK // RET_DK + h)),
            pl.BlockSpec((RET_BLOCK, RET_DV), lambda b, h, n: (row(b, h, n), 2 * A_QK // RET_DV + h)),
            pl.BlockSpec((RET_BLOCK, RET_DV),
                         lambda b, h, n: (row(b, h, n), (2 * A_QK + A_V) // RET_DV + h)),
            pl.BlockSpec((RET_BLOCK, RET_DK), lambda b, h, n: (n, 0)),
            pl.BlockSpec((RET_BLOCK, RET_DK), lambda b, h, n: (n, 0)),
            pl.BlockSpec((1, RET_BLOCK, RET_BLOCK), lambda b, h, n: (h, 0, 0)),
            pl.BlockSpec((1, RET_BLOCK, RET_DK), lambda b, h, n: (h, 0, 0)),
            pl.BlockSpec((1, RET_BLOCK, RET_DK), lambda b, h, n: (h, 0, 0)),
            pl.BlockSpec((1, 1, RET_DV), lambda b, h, n: (h, 0, 0)),
        ],
        out_specs=pl.BlockSpec((RET_BLOCK, RET_DV), lambda b, h, n: (row(b, h, n), h)),
        out_shape=jax.ShapeDtypeStruct((TOKENS, A_V), jnp.bfloat16),
        scratch_shapes=[pltpu.VMEM((RET_DK, RET_DV), jnp.float32)],
        compiler_params=pltpu.CompilerParams(
            dimension_semantics=("arbitrary", "arbitrary", "arbitrary")),
        name="retention",
    )(proj, proj, proj, proj, cos, sin, decay, q_dec, k_dec, blk_dec)


def _spatial_gate_kernel(u_ref, z_ref, lng_ref, lnb_ref, w_ref, b_ref, o_ref):
    u = _gelu_tanh(u_ref[...].astype(jnp.float32))
    z = _gelu_tanh(z_ref[...].astype(jnp.float32))
    mu = jnp.mean(z, axis=-1, keepdims=True)
    cen = z - mu
    var = jnp.mean(cen * cen, axis=-1, keepdims=True)
    zn = _bf16(cen * lax.rsqrt(var + EPS) * lng_ref[...] + lnb_ref[...])
    rows = lax.broadcasted_iota(jnp.int32, (SG_LEN, SG_LEN), 0)
    cols = lax.broadcasted_iota(jnp.int32, (SG_LEN, SG_LEN), 1)
    keep = (rows // CHUNK) >= (cols // CHUNK)
    for g in range(SG_GROUPS):
        cs = slice(g * SG_CH, (g + 1) * SG_CH)
        w = _bf16(jnp.where(keep, w_ref[g], 0.0))
        mixed = _dot(w, zn[:, cs]) + b_ref[g]
        o_ref[:, cs] = _bf16(u[:, cs] * mixed)


def _spatial_gate(proj, ln_g, ln_b, w_s, b_s):
    u_col = (2 * A_QK + 2 * A_V) // SG_WIDTH
    b_full = jnp.broadcast_to(b_s[:, :, None], (SG_GROUPS, SG_LEN, SG_CH))
    return pl.pallas_call(
        _spatial_gate_kernel,
        grid=(TOKENS // SG_LEN,),
        in_specs=[
            pl.BlockSpec((SG_LEN, SG_WIDTH), lambda i: (i, u_col)),
            pl.BlockSpec((SG_LEN, SG_WIDTH), lambda i: (i, u_col + 1)),
            pl.BlockSpec((1, SG_WIDTH), lambda i: (0, 0)),
            pl.BlockSpec((1, SG_WIDTH), lambda i: (0, 0)),
            pl.BlockSpec((SG_GROUPS, SG_LEN, SG_LEN), lambda i: (0, 0, 0)),
            pl.BlockSpec((SG_GROUPS, SG_LEN, SG_CH), lambda i: (0, 0, 0)),
        ],
        out_specs=pl.BlockSpec((SG_LEN, SG_WIDTH), lambda i: (i, 0)),
        out_shape=jax.ShapeDtypeStruct((TOKENS, SG_WIDTH), jnp.bfloat16),
        name="spatial_gate",
    )(proj, proj, ln_g.reshape(1, SG_WIDTH), ln_b.reshape(1, SG_WIDTH), w_s, b_full)


def _band_bias(rel_bias):
    i = np.arange(ATT_QBLOCK)[:, None]
    jj = np.arange(ATT_BAND)[None, :]
    rel = i - (jj - BAND_PAD)
    rel_idx = np.clip(rel, -(CHUNK - 1), MAX_REL) + (CHUNK - 1)
    in_band = (jj // CHUNK >= i // CHUNK) & (jj // CHUNK <= i // CHUNK + BAND_PAD // CHUNK)
    bias = rel_bias[:, rel_idx].astype(jnp.float32)
    return jnp.where(jnp.asarray(in_band)[None], bias, NEG_BIG)


def _head_rms(x, gain):
    lane = lax.broadcasted_iota(jnp.int32, x.shape, 1)
    first = lane < ATT_DH
    sq = x * x
    tot = jnp.sum(sq, axis=-1, keepdims=True)
    s0 = jnp.sum(jnp.where(first, sq, 0.0), axis=-1, keepdims=True)
    ms = jnp.where(first, s0, tot - s0) * (1.0 / ATT_DH)
    return x * lax.rsqrt(ms + EPS) * gain


def _band_attn_kernel(q_ref, k_ref, v_ref, bias_ref, qg_ref, kg_ref, o_ref, kpad_ref, vpad_ref):
    n = pl.program_id(2)

    @pl.when(n == 0)
    def _():
        kpad_ref[0:BAND_PAD, :] = jnp.zeros((BAND_PAD, 2 * ATT_DH), jnp.bfloat16)
        vpad_ref[0:BAND_PAD, :] = jnp.zeros((BAND_PAD, 2 * ATT_DH), jnp.bfloat16)
        kpad_ref[BAND_PAD:, :] = _bf16(_head_rms(k_ref[...].astype(jnp.float32), kg_ref[...]))
        vpad_ref[BAND_PAD:, :] = v_ref[...]

    q = _bf16(_head_rms(q_ref[...].astype(jnp.float32), qg_ref[...]) * (ATT_DH ** -0.5))
    start = pl.multiple_of(n * ATT_QBLOCK, ATT_QBLOCK)
    kb = kpad_ref[pl.ds(start, ATT_BAND), :]
    vb = vpad_ref[pl.ds(start, ATT_BAND), :]
    key_pos = lax.broadcasted_iota(jnp.int32, (ATT_QBLOCK, ATT_BAND), 1) + start
    valid = key_pos >= BAND_PAD
    outs = []
    for h in range(2):
        hs = slice(h * ATT_DH, (h + 1) * ATT_DH)
        s = _dot_nt(q[:, hs], kb[:, hs]) + bias_ref[h]
        s = jnp.where(valid, s, NEG_BIG)
        m = jnp.max(s, axis=-1, keepdims=True)
        p = jnp.exp(s - m)
        denom = jnp.sum(p, axis=-1, keepdims=True)
        outs.append(_dot(_bf16(p), vb[:, hs]) * (1.0 / denom))
    o_ref[...] = _bf16(jnp.concatenate(outs, axis=-1))


def _band_attention(qkv, q_g, k_g, rel_bias):
    bias = _band_bias(rel_bias)
    nq = SEQ // ATT_QBLOCK
    pair = 2 * ATT_DH
    n_pairs = ATT_HEADS // 2
    qg2 = jnp.concatenate([q_g, q_g]).reshape(1, pair)
    kg2 = jnp.concatenate([k_g, k_g]).reshape(1, pair)
    return pl.pallas_call(
        _band_attn_kernel,
        grid=(BATCH, n_pairs, nq),
        in_specs=[
            pl.BlockSpec((ATT_QBLOCK, pair), lambda b, hp, n: (b * nq + n, hp)),
            pl.BlockSpec((SEQ, pair), lambda b, hp, n: (b, n_pairs + hp)),
            pl.BlockSpec((SEQ, pair), lambda b, hp, n: (b, 2 * n_pairs + hp)),
            pl.BlockSpec((2, ATT_QBLOCK, ATT_BAND), lambda b, hp, n: (hp, 0, 0)),
            pl.BlockSpec((1, pair), lambda b, hp, n: (0, 0)),
            pl.BlockSpec((1, pair), lambda b, hp, n: (0, 0)),
        ],
        out_specs=pl.BlockSpec((ATT_QBLOCK, pair), lambda b, hp, n: (b * nq + n, hp)),
        out_shape=jax.ShapeDtypeStruct((TOKENS, D_MODEL), jnp.bfloat16),
        scratch_shapes=[pltpu.VMEM((SEQ + BAND_PAD, pair), jnp.bfloat16),
                        pltpu.VMEM((SEQ + BAND_PAD, pair), jnp.bfloat16)],
        compiler_params=pltpu.CompilerParams(
            dimension_semantics=("arbitrary", "arbitrary", "arbitrary")),
        name="band_attention",
    )(qkv, qkv, qkv, bias, qg2, kg2)


def _split_bf16(x):
    hi = _bf16(x)
    lo = _bf16(x - hi.astype(jnp.float32))
    return hi, lo


def _router_kernel(x_ref, g_ref, sc_ref, sh_ref, rw_ref, rb_ref,
                   h_ref, idx_ref, rank_ref, gate_ref, cnt_ref, carry_ref):
    @pl.when(pl.program_id(0) == 0)
    def _():
        carry_ref[...] = jnp.zeros_like(carry_ref)

    h = _norm_mod(x_ref[...], g_ref[...], sc_ref[0], sh_ref[0])
    h_ref[...] = h
    h_hi, h_lo = _split_bf16(h)
    w_hi, w_lo = _split_bf16(rw_ref[...])
    logits = _dot_nt(w_hi, h_hi) + _dot_nt(w_hi, h_lo) + _dot_nt(w_lo, h_hi) + rb_ref[...]

    expert = lax.broadcasted_iota(jnp.int32, logits.shape, 0).astype(jnp.float32)
    work = logits
    vals, idxs, sels = [], [], []
    for _ in range(TOP_K):
        m = jnp.max(work, axis=0, keepdims=True)
        pick = jnp.min(jnp.where(work == m, expert, float(N_EXPERTS)), axis=0, keepdims=True)
        sel = expert == pick
        work = jnp.where(sel, -jnp.inf, work)
        vals.append(m)
        idxs.append(pick)
        sels.append(sel)
    exps = [jnp.exp(v - vals[0]) for v in vals]
    denom = exps[0] + exps[1] + exps[2] + exps[3]
    gate_ref[...] = jnp.concatenate([e / denom for e in exps], axis=0)
    idx_ref[...] = jnp.concatenate(idxs, axis=0).astype(jnp.int32)

    chosen = jnp.zeros(logits.shape, jnp.float32)
    for sel in sels:
        chosen = jnp.where(sel, 1.0, chosen)
    tile = logits.shape[1]
    earlier = (lax.broadcasted_iota(jnp.int32, (tile, tile), 0)
               < lax.broadcasted_iota(jnp.int32, (tile, tile), 1))
    before = _dot(_bf16(chosen), jnp.where(earlier, 1.0, 0.0).astype(jnp.bfloat16))
    rank_full = before + carry_ref[...]
    ranks = [jnp.sum(jnp.where(sel, rank_full, 0.0), axis=0, keepdims=True) for sel in sels]
    rank_ref[...] = jnp.concatenate(ranks, axis=0).astype(jnp.int32)
    carry = carry_ref[...] + jnp.sum(chosen, axis=1, keepdims=True)
    carry_ref[...] = carry
    cnt_ref[...] = jnp.broadcast_to(carry, cnt_ref.shape)


def _router(x, g, sc, sh, router_w, router_b):
    tiles_per_batch = SEQ // ROUTE_TILE
    lane_out = lambda dt: jax.ShapeDtypeStruct((TOP_K, TOKENS), dt)
    lane_spec = pl.BlockSpec((TOP_K, ROUTE_TILE), lambda i: (0, i))
    return pl.pallas_call(
        _router_kernel,
        grid=(TOKENS // ROUTE_TILE,),
        in_specs=[
            pl.BlockSpec((ROUTE_TILE, D_MODEL), lambda i: (i, 0)),
            pl.BlockSpec((1, D_MODEL), lambda i: (0, 0)),
            pl.BlockSpec((1, 1, D_MODEL), lambda i: (i // tiles_per_batch, 0, 0)),
            pl.BlockSpec((1, 1, D_MODEL), lambda i: (i // tiles_per_batch, 0, 0)),
            pl.BlockSpec((N_EXPERTS, D_MODEL), lambda i: (0, 0)),
            pl.BlockSpec((N_EXPERTS, 1), lambda i: (0, 0)),
        ],
        out_specs=[
            pl.BlockSpec((ROUTE_TILE, D_MODEL), lambda i: (i, 0)),
            lane_spec, lane_spec, lane_spec,
            pl.BlockSpec((N_EXPERTS, 128), lambda i: (0, 0)),
        ],
        out_shape=[
            jax.ShapeDtypeStruct((TOKENS, D_MODEL), jnp.float32),
            lane_out(jnp.int32), lane_out(jnp.int32), lane_out(jnp.float32),
            jax.ShapeDtypeStruct((N_EXPERTS, 128), jnp.float32),
        ],
        scratch_shapes=[pltpu.VMEM((N_EXPERTS, 1), jnp.float32)],
        compiler_params=pltpu.CompilerParams(dimension_semantics=("arbitrary",)),
        name="moe_router",
    )(x, g.reshape(1, D_MODEL), sc, sh, router_w.T, router_b.reshape(N_EXPERTS, 1))


def _row_copy(src_ref, src_row, dst_ref, dst_row, sem):
    return pltpu.make_async_copy(src_ref.at[pl.ds(src_row, 1), :],
                                 dst_ref.at[pl.ds(dst_row, 1), :], sem)


def _dispatch_kernel(pos_ref, h_ref, xs_in_ref, xs_ref, sem):
    del xs_in_ref

    def issue(t, carry):
        for k in range(TOP_K):
            _row_copy(h_ref, t, xs_ref, pos_ref[t * TOP_K + k], sem).start()
        return carry

    lax.fori_loop(0, DISPATCH_TILE, issue, 0)

    def drain(t, carry):
        for k in range(TOP_K):
            _row_copy(h_ref, t, xs_ref, pos_ref[t * TOP_K + k], sem).wait()
        return carry

    lax.fori_loop(0, DISPATCH_TILE, drain, 0)


def _dispatch(h, pos_flat):
    xs0 = jnp.zeros((MOE_CAP, D_MODEL), jnp.float32)
    return pl.pallas_call(
        _dispatch_kernel,
        grid=(TOKENS // DISPATCH_TILE,),
        in_specs=[
            pl.BlockSpec((DISPATCH_TILE * TOP_K,), lambda i: (i,), memory_space=pltpu.SMEM),
            pl.BlockSpec((DISPATCH_TILE, D_MODEL), lambda i: (i, 0)),
            pl.BlockSpec(memory_space=pl.ANY),
        ],
        out_specs=pl.BlockSpec(memory_space=pl.ANY),
        out_shape=jax.ShapeDtypeStruct((MOE_CAP, D_MODEL), jnp.float32),
        scratch_shapes=[pltpu.SemaphoreType.DMA],
        input_output_aliases={2: 0},
        compiler_params=pltpu.CompilerParams(dimension_semantics=("arbitrary",)),
        name="moe_dispatch",
    )(pos_flat, h, xs0)


def _expert_kernel(be_ref, nused_ref, xs_ref, w1_ref, b1_ref, w2_ref, b2_ref, ys_ref,
                   w1b_ref, w2b_ref):
    i = pl.program_id(0)
    new_expert = jnp.logical_or(i == 0, be_ref[i] != be_ref[jnp.maximum(i - 1, 0)])

    @pl.when(new_expert)
    def _():
        for r in range(D_MODEL // 256):
            rows = slice(r * 256, (r + 1) * 256)
            w1b_ref[rows, :] = _bf16(w1_ref[0, rows, :])
            w2b_ref[rows, :] = _bf16(w2_ref[0, rows, :])

    @pl.when(i < nused_ref[0])
    def _():
        x = _bf16(xs_ref[...])
        glu = _dot(x, w1b_ref[:, :D_FF]) + b1_ref[0][:, :D_FF]
        lin = _dot(x, w1b_ref[:, D_FF:]) + b1_ref[0][:, D_FF:]
        glu = jnp.minimum(glu, SWIGLU_LIMIT)
        lin = jnp.clip(lin, -SWIGLU_LIMIT, SWIGLU_LIMIT)
        act = glu * (1.0 / (1.0 + jnp.exp(-SWIGLU_ALPHA * glu))) * (lin + 1.0)
        ys_ref[...] = _dot(_bf16(act), w2b_ref[...]) + b2_ref[0]

    @pl.when(i >= nused_ref[0])
    def _():
        ys_ref[...] = jnp.zeros_like(ys_ref)


def _experts(xs, block_e, n_used, w1, b1, w2, b2):
    grid_spec = pltpu.PrefetchScalarGridSpec(
        num_scalar_prefetch=2,
        grid=(MOE_NBLOCKS,),
        in_specs=[
            pl.BlockSpec((MOE_BLOCK, D_MODEL), lambda i, be, nu: (jnp.minimum(i, nu[0] - 1), 0)),
            pl.BlockSpec((1, D_MODEL, 2 * D_FF), lambda i, be, nu: (be[i], 0, 0)),
            pl.BlockSpec((1, 1, 2 * D_FF), lambda i, be, nu: (be[i], 0, 0)),
            pl.BlockSpec((1, D_FF, D_MODEL), lambda i, be, nu: (be[i], 0, 0)),
            pl.BlockSpec((1, 1, D_MODEL), lambda i, be, nu: (be[i], 0, 0)),
        ],
        out_specs=pl.BlockSpec((MOE_BLOCK, D_MODEL), lambda i, be, nu: (i, 0)),
        scratch_shapes=[pltpu.VMEM((D_MODEL, 2 * D_FF), jnp.bfloat16),
                        pltpu.VMEM((D_FF, D_MODEL), jnp.bfloat16)],
    )
    return pl.pallas_call(
        _expert_kernel,
        grid_spec=grid_spec,
        out_shape=jax.ShapeDtypeStruct((MOE_CAP, D_MODEL), jnp.float32),
        compiler_params=pltpu.CompilerParams(dimension_semantics=("arbitrary",),
                                             vmem_limit_bytes=VMEM_LIMIT_V7X),
        name="moe_experts",
    )(block_e, n_used, xs, w1, b1.reshape(N_EXPERTS, 1, 2 * D_FF), w2,
      b2.reshape(N_EXPERTS, 1, D_MODEL))


def _combine_kernel(pos_ref, x_ref, gates_ref, g2_ref, ys_ref, o_ref, buf_ref, sem):
    def issue(t, carry):
        for k in range(TOP_K):
            _row_copy(ys_ref, pos_ref[t * TOP_K + k], buf_ref.at[k], t, sem).start()
        return carry

    lax.fori_loop(0, DISPATCH_TILE, issue, 0)

    def drain(t, carry):
        for k in range(TOP_K):
            _row_copy(ys_ref, pos_ref[t * TOP_K + k], buf_ref.at[k], t, sem).wait()
        return carry

    lax.fori_loop(0, DISPATCH_TILE, drain, 0)

    gates = gates_ref[...]
    y = gates[:, 0:1] * buf_ref[0]
    for k in range(1, TOP_K):
        y = y + gates[:, k:k + 1] * buf_ref[k]
    o_ref[...] = x_ref[...] + g2_ref[0] * y


def _combine(x, pos_flat, gates_tok, g2, ys):
    tiles_per_batch = SEQ // DISPATCH_TILE
    return pl.pallas_call(
        _combine_kernel,
        grid=(TOKENS // DISPATCH_TILE,),
        in_specs=[
            pl.BlockSpec((DISPATCH_TILE * TOP_K,), lambda i: (i,), memory_space=pltpu.SMEM),
            pl.BlockSpec((DISPATCH_TILE, D_MODEL), lambda i: (i, 0)),
            pl.BlockSpec((DISPATCH_TILE, TOP_K), lambda i: (i, 0)),
            pl.BlockSpec((1, 1, D_MODEL), lambda i: (i // tiles_per_batch, 0, 0)),
            pl.BlockSpec(memory_space=pl.ANY),
        ],
        out_specs=pl.BlockSpec((DISPATCH_TILE, D_MODEL), lambda i: (i, 0)),
        out_shape=jax.ShapeDtypeStruct((TOKENS, D_MODEL), jnp.float32),
        scratch_shapes=[pltpu.VMEM((TOP_K, DISPATCH_TILE, D_MODEL), jnp.float32),
                        pltpu.SemaphoreType.DMA],
        compiler_params=pltpu.CompilerParams(dimension_semantics=("arbitrary",)),
        name="moe_combine",
    )(pos_flat, x, gates_tok, g2, ys)


def _moe_layer(x, g, sc, sh, g2, router_w, router_b, w1, b1, w2, b2):
    h, idx, rank, gates, cnt = _router(x, g, sc, sh, router_w, router_b)
    counts = cnt[:, 0].astype(jnp.int32)
    padded = ((counts + MOE_BLOCK - 1) // MOE_BLOCK) * MOE_BLOCK
    pends = jnp.cumsum(padded)
    pstarts = pends - padded
    pos = pstarts[idx] + rank
    pos_flat = pos.T.reshape(-1)
    block_start = jnp.arange(MOE_NBLOCKS, dtype=jnp.int32) * MOE_BLOCK
    block_e = jnp.minimum(jnp.searchsorted(pends, block_start, side="right"),
                          N_EXPERTS - 1).astype(jnp.int32)
    n_used = (pends[-1:] // MOE_BLOCK).astype(jnp.int32)
    xs = _dispatch(h, pos_flat)
    ys = _experts(xs, block_e, n_used, w1, b1, w2, b2)
    return _combine(x, pos_flat, gates.T, g2, ys)


def kernel(x, c, ada_w, ada_b, norm_mix_g, norm_ffn_g, ev_w_in, ev_w_out, sg_ln_g, sg_ln_b, sg_w, sg_b, od_w_in, od_w_out, od_q_g, od_k_g, od_rel_bias, moe_router_w, moe_router_b, moe_w1, moe_b1, moe_w2, moe_b2):
    mod = _ada_mod(c, ada_w, ada_b)
    xt = x.reshape(TOKENS, D_MODEL)
    for l in range(DEPTH):
        sh1, sc1, g1, sh2, sc2, g2 = [m.reshape(BATCH, 1, D_MODEL)
                                      for m in jnp.split(mod[l], 6, axis=-1)]
        i = l // 2
        if l % 2 == 0:
            proj = _norm_proj(xt, norm_mix_g[l], sc1, sh1, _bf16(ev_w_in[i]))
            a_out = _retention(proj)
            b_out = _spatial_gate(proj, sg_ln_g[i], sg_ln_b[i], sg_w[i], sg_b[i])
            xt = _out_proj([a_out, b_out], _bf16(ev_w_out[i]), xt, g1)
        else:
            qkv = _norm_proj(xt, norm_mix_g[l], sc1, sh1, _bf16(od_w_in[i]))
            att = _band_attention(qkv, od_q_g[i], od_k_g[i], od_rel_bias[i])
            xt = _out_proj([att], _bf16(od_w_out[i]), xt, g1)
        xt = _moe_layer(xt, norm_ffn_g[l], sc2, sh2, g2, moe_router_w[l], moe_router_b[l],
                        moe_w1[l], moe_b1[l], moe_w2[l], moe_b2[l])
    return xt.reshape(BATCH, SEQ, D_MODEL)
```

```python
import functools
import math

import numpy as np
import jax
import jax.numpy as jnp
from jax import lax
from jax.experimental import pallas as pl
from jax.experimental.pallas import tpu as pltpu

D_MODEL = 1024
BATCH = 8
SEQ = 2048
DEPTH = 4
TOKENS = BATCH * SEQ
CHUNK = 64
EPS = 1e-6

RET_HEADS = 4
RET_DK = 128
RET_DV = 256
ROPE_BASE = 10000.0
A_QK = RET_HEADS * RET_DK
A_V = RET_HEADS * RET_DV
SG_GROUPS = 4
SG_CH = 128
SG_WIDTH = SG_GROUPS * SG_CH
SG_LEN = 128
EVEN_IN = 2 * A_QK + 2 * A_V + 2 * SG_WIDTH
EVEN_MIX = A_V + SG_WIDTH

ATT_HEADS = 16
ATT_DH = 64
BAND_PAD = 8 * CHUNK
MAX_REL = 256
REL_SIZE = (CHUNK - 1) + MAX_REL + 1

N_EXPERTS = 32
TOP_K = 4
D_FF = D_MODEL
SWIGLU_ALPHA = 1.702
SWIGLU_LIMIT = 7.0

ROW_TILE = 512
COL_CHUNK = 512
RET_BLOCK = 256
ATT_QBLOCK = 128
ATT_BAND = ATT_QBLOCK + BAND_PAD
ROUTE_TILE = 512
MOE_BLOCK = 256
MOE_CAP = TOKENS * TOP_K + N_EXPERTS * MOE_BLOCK
MOE_NBLOCKS = MOE_CAP // MOE_BLOCK
DISPATCH_TILE = 256
VMEM_LIMIT_V7X = 56 * 1024 * 1024

NEG_BIG = -1e30


def _silu(x):
    return x * (1.0 / (1.0 + jnp.exp(-x)))


def _gelu_tanh(x):
    return 0.5 * x * (1.0 + jnp.tanh(math.sqrt(2.0 / math.pi) * (x + 0.044715 * (x * x * x))))


def _bf16(x):
    return x.astype(jnp.bfloat16)


def _dot(a, b):
    return jnp.dot(a, b, preferred_element_type=jnp.float32)


def _dot_nt(a, b):
    return lax.dot_general(a, b, (((1,), (1,)), ((), ())), preferred_element_type=jnp.float32)


def _dot_tn(a, b):
    return lax.dot_general(a, b, (((0,), (0,)), ((), ())), preferred_element_type=jnp.float32)


def _norm_mod(x, g, sc, sh):
    y = x * lax.rsqrt(jnp.mean(x * x, axis=-1, keepdims=True) + EPS)
    return (y * g) * (1.0 + sc) + sh


def _ada_kernel(c_ref, w_ref, b_ref, o_ref):
    c_act = _silu(c_ref[...])
    o_ref[0] = _dot(c_act, w_ref[0]) + b_ref[0]


def _ada_mod(c, ada_w, ada_b):
    n_col = 6
    return pl.pallas_call(
        _ada_kernel,
        grid=(DEPTH, n_col),
        in_specs=[
            pl.BlockSpec((BATCH, D_MODEL), lambda l, j: (0, 0)),
            pl.BlockSpec((1, D_MODEL, D_MODEL), lambda l, j: (l, 0, j)),
            pl.BlockSpec((1, 1, D_MODEL), lambda l, j: (l, 0, j)),
        ],
        out_specs=pl.BlockSpec((1, BATCH, D_MODEL), lambda l, j: (l, 0, j)),
        out_shape=jax.ShapeDtypeStruct((DEPTH, BATCH, 6 * D_MODEL), jnp.float32),
        name="ada_mod",
    )(c, ada_w, ada_b.reshape(DEPTH, 1, 6 * D_MODEL))


def _norm_proj_kernel(x_ref, g_ref, sc_ref, sh_ref, w_ref, o_ref):
    h = _bf16(_norm_mod(x_ref[...], g_ref[...], sc_ref[0], sh_ref[0]))
    n_out = o_ref.shape[1]
    for j in range(n_out // COL_CHUNK):
        cols = slice(j * COL_CHUNK, (j + 1) * COL_CHUNK)
        o_ref[:, cols] = _bf16(_dot(h, w_ref[:, cols]))


def _norm_proj(x, g, sc, sh, w_bf16):
    n_out = w_bf16.shape[1]
    tiles_per_batch = SEQ // ROW_TILE
    return pl.pallas_call(
        _norm_proj_kernel,
        grid=(TOKENS // ROW_TILE,),
        in_specs=[
            pl.BlockSpec((ROW_TILE, D_MODEL), lambda i: (i, 0)),
            pl.BlockSpec((1, D_MODEL), lambda i: (0, 0)),
            pl.BlockSpec((1, 1, D_MODEL), lambda i: (i // tiles_per_batch, 0, 0)),
            pl.BlockSpec((1, 1, D_MODEL), lambda i: (i // tiles_per_batch, 0, 0)),
            pl.BlockSpec((D_MODEL, n_out), lambda i: (0, 0)),
        ],
        out_specs=pl.BlockSpec((ROW_TILE, n_out), lambda i: (i, 0)),
        out_shape=jax.ShapeDtypeStruct((TOKENS, n_out), jnp.bfloat16),
        compiler_params=pltpu.CompilerParams(vmem_limit_bytes=VMEM_LIMIT_V7X),
        name="norm_proj",
    )(x, g.reshape(1, D_MODEL), sc, sh, w_bf16)


def _out_proj_kernel(*refs, widths):
    part_refs = refs[:len(widths)]
    w_ref, x_ref, g_ref, o_ref = refs[len(widths):]
    for j in range(D_MODEL // COL_CHUNK):
        cols = slice(j * COL_CHUNK, (j + 1) * COL_CHUNK)
        acc = None
        row0 = 0
        for p_ref, width in zip(part_refs, widths):
            term = _dot(p_ref[...], w_ref[row0:row0 + width, cols])
            acc = term if acc is None else acc + term
            row0 += width
        o_ref[:, cols] = x_ref[:, cols] + g_ref[0][:, cols] * acc


def _out_proj(parts, w_bf16, x, gate):
    widths = tuple(p.shape[1] for p in parts)
    tiles_per_batch = SEQ // ROW_TILE
    in_specs = [pl.BlockSpec((ROW_TILE, width), lambda i: (i, 0)) for width in widths]
    in_specs += [
        pl.BlockSpec((sum(widths), D_MODEL), lambda i: (0, 0)),
        pl.BlockSpec((ROW_TILE, D_MODEL), lambda i: (i, 0)),
        pl.BlockSpec((1, 1, D_MODEL), lambda i: (i // tiles_per_batch, 0, 0)),
    ]
    return pl.pallas_call(
        functools.partial(_out_proj_kernel, widths=widths),
        grid=(TOKENS // ROW_TILE,),
        in_specs=in_specs,
        out_specs=pl.BlockSpec((ROW_TILE, D_MODEL), lambda i: (i, 0)),
        out_shape=jax.ShapeDtypeStruct((TOKENS, D_MODEL), jnp.float32),
        compiler_params=pltpu.CompilerParams(vmem_limit_bytes=VMEM_LIMIT_V7X),
        name="out_proj",
    )(*parts, w_bf16, x, gate)


def _retention_tables():
    heads = np.arange(RET_HEADS, dtype=np.float64)
    log_g = np.log1p(-np.exp2(-5.0 - heads))
    idx = np.arange(RET_BLOCK, dtype=np.float64)
    diff = idx[:, None] - idx[None, :]
    ci, cj = (idx // CHUNK)[:, None], (idx // CHUNK)[None, :]
    expo = np.where(ci == cj, np.abs(diff), diff)
    decay = np.where(cj <= ci, np.exp(log_g[:, None, None] * expo[None]), 0.0)
    q_dec = np.exp(log_g[:, None] * (idx[None, :] + 1.0))
    k_dec = np.exp(log_g[:, None] * (RET_BLOCK - 1.0 - idx[None, :]))
    blk_dec = np.exp(log_g * RET_BLOCK)
    q_dec = np.broadcast_to(q_dec[:, :, None], (RET_HEADS, RET_BLOCK, RET_DK))
    k_dec = np.broadcast_to(k_dec[:, :, None], (RET_HEADS, RET_BLOCK, RET_DK))
    blk_dec = np.broadcast_to(blk_dec[:, None, None], (RET_HEADS, 1, RET_DV))
    half = RET_DK // 2
    inv = ROPE_BASE ** (-np.arange(half, dtype=np.float64) / half)
    ang = np.arange(SEQ, dtype=np.float64)[:, None] * inv[None, :]
    cos = np.concatenate([np.cos(ang), np.cos(ang)], axis=1)
    sin = np.concatenate([-np.sin(ang), np.sin(ang)], axis=1)
    f32 = lambda a: jnp.asarray(np.ascontiguousarray(a), jnp.float32)
    return f32(decay), f32(q_dec), f32(k_dec), f32(blk_dec), f32(cos), f32(sin)


def _retention_kernel(q_ref, k_ref, v_ref, gate_ref, cos_ref, sin_ref, dec_ref, qd_ref, kd_ref,
                      bd_ref, o_ref, state_ref):
    @pl.when(pl.program_id(2) == 0)
    def _():
        state_ref[...] = jnp.zeros_like(state_ref)

    cos, sin = cos_ref[...], sin_ref[...]
    q = q_ref[...].astype(jnp.float32)
    k = k_ref[...].astype(jnp.float32)
    qr = q * cos + pltpu.roll(q, RET_DK // 2, axis=1) * sin
    kr = (k * cos + pltpu.roll(k, RET_DK // 2, axis=1) * sin) * (RET_DK ** -0.5)
    v = v_ref[...]

    scores = _dot_nt(_bf16(qr), _bf16(kr)) * dec_ref[0]
    intra = _dot(_bf16(scores), v)
    state = state_ref[...]
    inter = _dot(_bf16(qr * qd_ref[0]), _bf16(state))
    state_ref[...] = state * bd_ref[0] + _dot_tn(_bf16(kr * kd_ref[0]), v)

    o = intra + inter
    mu = jnp.mean(o, axis=-1, keepdims=True)
    cen = o - mu
    var = jnp.mean(cen * cen, axis=-1, keepdims=True)
    normed = cen * lax.rsqrt(var + EPS)
    o_ref[...] = _bf16(_silu(gate_ref[...].astype(jnp.float32)) * normed)


def _retention(proj):
    decay, q_dec, k_dec, blk_dec, cos, sin = _retention_tables()
    nblk = SEQ // RET_BLOCK
    row = lambda b, h, n: b * nblk + n
    return pl.pallas_call(
        _retention_kernel,
        grid=(BATCH, RET_HEADS, nblk),
        in_specs=[
            pl.BlockSpec((RET_BLOCK, RET_DK), lambda b, h, n: (row(b, h, n), h)),
            pl.BlockSpec((RET_BLOCK, RET_DK), lambda b, h, n: (row(b, h, n), A_QK // RET_DK + h)),
            pl.BlockSpec((RET_BLOCK, RET_DV), lambda b, h, n: (row(b, h, n), 2 * A_QK // RET_DV + h)),
            pl.BlockSpec((RET_BLOCK, RET_DV),
                         lambda b, h, n: (row(b, h, n), (2 * A_QK + A_V) // RET_DV + h)),
            pl.BlockSpec((RET_BLOCK, RET_DK), lambda b, h, n: (n, 0)),
            pl.BlockSpec((RET_BLOCK, RET_DK), lambda b, h, n: (n, 0)),
            pl.BlockSpec((1, RET_BLOCK, RET_BLOCK), lambda b, h, n: (h, 0, 0)),
            pl.BlockSpec((1, RET_BLOCK, RET_DK), lambda b, h, n: (h, 0, 0)),
            pl.BlockSpec((1, RET_BLOCK, RET_DK), lambda b, h, n: (h, 0, 0)),
            pl.BlockSpec((1, 1, RET_DV), lambda b, h, n: (h, 0, 0)),
        ],
        out_specs=pl.BlockSpec((RET_BLOCK, RET_DV), lambda b, h, n: (row(b, h, n), h)),
        out_shape=jax.ShapeDtypeStruct((TOKENS, A_V), jnp.bfloat16),
        scratch_shapes=[pltpu.VMEM((RET_DK, RET_DV), jnp.float32)],
        compiler_params=pltpu.CompilerParams(
            dimension_semantics=("arbitrary", "arbitrary", "arbitrary")),
        name="retention",
    )(proj, proj, proj, proj, cos, sin, decay, q_dec, k_dec, blk_dec)


def _spatial_gate_kernel(u_ref, z_ref, lng_ref, lnb_ref, w_ref, b_ref, o_ref):
    u = _gelu_tanh(u_ref[...].astype(jnp.float32))
    z = _gelu_tanh(z_ref[...].astype(jnp.float32))
    mu = jnp.mean(z, axis=-1, keepdims=True)
    cen = z - mu
    var = jnp.mean(cen * cen, axis=-1, keepdims=True)
    zn = _bf16(cen * lax.rsqrt(var + EPS) * lng_ref[...] + lnb_ref[...])
    rows = lax.broadcasted_iota(jnp.int32, (SG_LEN, SG_LEN), 0)
    cols = lax.broadcasted_iota(jnp.int32, (SG_LEN, SG_LEN), 1)
    keep = (rows // CHUNK) >= (cols // CHUNK)
    for g in range(SG_GROUPS):
        cs = slice(g * SG_CH, (g + 1) * SG_CH)
        w = _bf16(jnp.where(keep, w_ref[g], 0.0))
        mixed = _dot(w, zn[:, cs]) + b_ref[g]
        o_ref[:, cs] = _bf16(u[:, cs] * mixed)


def _spatial_gate(proj, ln_g, ln_b, w_s, b_s):
    u_col = (2 * A_QK + 2 * A_V) // SG_WIDTH
    b_full = jnp.broadcast_to(b_s[:, :, None], (SG_GROUPS, SG_LEN, SG_CH))
    return pl.pallas_call(
        _spatial_gate_kernel,
        grid=(TOKENS // SG_LEN,),
        in_specs=[
            pl.BlockSpec((SG_LEN, SG_WIDTH), lambda i: (i, u_col)),
            pl.BlockSpec((SG_LEN, SG_WIDTH), lambda i: (i, u_col + 1)),
            pl.BlockSpec((1, SG_WIDTH), lambda i: (0, 0)),
            pl.BlockSpec((1, SG_WIDTH), lambda i: (0, 0)),
            pl.BlockSpec((SG_GROUPS, SG_LEN, SG_LEN), lambda i: (0, 0, 0)),
            pl.BlockSpec((SG_GROUPS, SG_LEN, SG_CH), lambda i: (0, 0, 0)),
        ],
        out_specs=pl.BlockSpec((SG_LEN, SG_WIDTH), lambda i: (i, 0)),
        out_shape=jax.ShapeDtypeStruct((TOKENS, SG_WIDTH), jnp.bfloat16),
        name="spatial_gate",
    )(proj, proj, ln_g.reshape(1, SG_WIDTH), ln_b.reshape(1, SG_WIDTH), w_s, b_full)


def _band_bias(rel_bias):
    n_rel = ATT_QBLOCK + ATT_BAND - 1
    rel_rev = (ATT_BAND - 1) - np.arange(n_rel)
    table = rel_bias[:, np.clip(rel_rev, -(CHUNK - 1), MAX_REL) + (CHUNK - 1)].astype(jnp.float32)
    rows = [table[:, ATT_QBLOCK - 1 - i:ATT_QBLOCK - 1 - i + ATT_BAND] for i in range(ATT_QBLOCK)]
    bias = jnp.stack(rows, axis=1)
    i = np.arange(ATT_QBLOCK)[:, None]
    jj = np.arange(ATT_BAND)[None, :]
    in_band = (jj // CHUNK >= i // CHUNK) & (jj // CHUNK <= i // CHUNK + BAND_PAD // CHUNK)
    return jnp.where(jnp.asarray(in_band)[None], bias, NEG_BIG)


def _head_rms(x, gain):
    lane = lax.broadcasted_iota(jnp.int32, x.shape, 1)
    first = lane < ATT_DH
    sq = x * x
    tot = jnp.sum(sq, axis=-1, keepdims=True)
    s0 = jnp.sum(jnp.where(first, sq, 0.0), axis=-1, keepdims=True)
    ms = jnp.where(first, s0, tot - s0) * (1.0 / ATT_DH)
    return x * lax.rsqrt(ms + EPS) * gain


def _band_attn_kernel(q_ref, k_ref, v_ref, bias_ref, qg_ref, kg_ref, o_ref, kpad_ref, vpad_ref):
    n = pl.program_id(2)

    @pl.when(n == 0)
    def _():
        kpad_ref[0:BAND_PAD, :] = jnp.zeros((BAND_PAD, 2 * ATT_DH), jnp.bfloat16)
        vpad_ref[0:BAND_PAD, :] = jnp.zeros((BAND_PAD, 2 * ATT_DH), jnp.bfloat16)
        kpad_ref[BAND_PAD:, :] = _bf16(_head_rms(k_ref[...].astype(jnp.float32), kg_ref[...]))
        vpad_ref[BAND_PAD:, :] = v_ref[...]

    q = _bf16(_head_rms(q_ref[...].astype(jnp.float32), qg_ref[...]) * (ATT_DH ** -0.5))
    start = pl.multiple_of(n * ATT_QBLOCK, ATT_QBLOCK)
    kb = kpad_ref[pl.ds(start, ATT_BAND), :]
    vb = vpad_ref[pl.ds(start, ATT_BAND), :]
    key_pos = lax.broadcasted_iota(jnp.int32, (ATT_QBLOCK, ATT_BAND), 1) + start
    valid = key_pos >= BAND_PAD
    outs = []
    for h in range(2):
        hs = slice(h * ATT_DH, (h + 1) * ATT_DH)
        s = _dot_nt(q[:, hs], kb[:, hs]) + bias_ref[h]
        s = jnp.where(valid, s, NEG_BIG)
        m = jnp.max(s, axis=-1, keepdims=True)
        p = jnp.exp(s - m)
        denom = jnp.sum(p, axis=-1, keepdims=True)
        outs.append(_dot(_bf16(p), vb[:, hs]) * (1.0 / denom))
    o_ref[...] = _bf16(jnp.concatenate(outs, axis=-1))


def _band_attention(qkv, q_g, k_g, rel_bias):
    bias = _band_bias(rel_bias)
    nq = SEQ // ATT_QBLOCK
    pair = 2 * ATT_DH
    n_pairs = ATT_HEADS // 2
    qg2 = jnp.concatenate([q_g, q_g]).reshape(1, pair)
    kg2 = jnp.concatenate([k_g, k_g]).reshape(1, pair)
    return pl.pallas_call(
        _band_attn_kernel,
        grid=(BATCH, n_pairs, nq),
        in_specs=[
            pl.BlockSpec((ATT_QBLOCK, pair), lambda b, hp, n: (b * nq + n, hp)),
            pl.BlockSpec((SEQ, pair), lambda b, hp, n: (b, n_pairs + hp)),
            pl.BlockSpec((SEQ, pair), lambda b, hp, n: (b, 2 * n_pairs + hp)),
            pl.BlockSpec((2, ATT_QBLOCK, ATT_BAND), lambda b, hp, n: (hp, 0, 0)),
            pl.BlockSpec((1, pair), lambda b, hp, n: (0, 0)),
            pl.BlockSpec((1, pair), lambda b, hp, n: (0, 0)),
        ],
        out_specs=pl.BlockSpec((ATT_QBLOCK, pair), lambda b, hp, n: (b * nq + n, hp)),
        out_shape=jax.ShapeDtypeStruct((TOKENS, D_MODEL), jnp.bfloat16),
        scratch_shapes=[pltpu.VMEM((SEQ + BAND_PAD, pair), jnp.bfloat16),
                        pltpu.VMEM((SEQ + BAND_PAD, pair), jnp.bfloat16)],
        compiler_params=pltpu.CompilerParams(
            dimension_semantics=("arbitrary", "arbitrary", "arbitrary")),
        name="band_attention",
    )(qkv, qkv, qkv, bias, qg2, kg2)


def _split_bf16(x):
    hi = _bf16(x)
    lo = _bf16(x - hi.astype(jnp.float32))
    return hi, lo


def _router_kernel(x_ref, g_ref, sc_ref, sh_ref, rw_ref, rb_ref,
                   h_ref, idx_ref, rank_ref, gate_ref, cnt_ref, carry_ref):
    @pl.when(pl.program_id(0) == 0)
    def _():
        carry_ref[...] = jnp.zeros_like(carry_ref)

    h = _norm_mod(x_ref[...], g_ref[...], sc_ref[0], sh_ref[0])
    h_ref[...] = h
    h_hi, h_lo = _split_bf16(h)
    w_hi, w_lo = _split_bf16(rw_ref[...])
    logits = _dot_nt(w_hi, h_hi) + _dot_nt(w_hi, h_lo) + _dot_nt(w_lo, h_hi) + rb_ref[...]

    expert = lax.broadcasted_iota(jnp.int32, logits.shape, 0).astype(jnp.float32)
    work = logits
    vals, idxs, sels = [], [], []
    for _ in range(TOP_K):
        m = jnp.max(work, axis=0, keepdims=True)
        pick = jnp.min(jnp.where(work == m, expert, float(N_EXPERTS)), axis=0, keepdims=True)
        sel = expert == pick
        work = jnp.where(sel, -jnp.inf, work)
        vals.append(m)
        idxs.append(pick)
        sels.append(sel)
    exps = [jnp.exp(v - vals[0]) for v in vals]
    denom = exps[0] + exps[1] + exps[2] + exps[3]
    gate_ref[...] = jnp.concatenate([e / denom for e in exps], axis=0)
    idx_ref[...] = jnp.concatenate(idxs, axis=0).astype(jnp.int32)

    chosen = jnp.zeros(logits.shape, jnp.float32)
    for sel in sels:
        chosen = jnp.where(sel, 1.0, chosen)
    tile = logits.shape[1]
    earlier = (lax.broadcasted_iota(jnp.int32, (tile, tile), 0)
               < lax.broadcasted_iota(jnp.int32, (tile, tile), 1))
    before = _dot(_bf16(chosen), jnp.where(earlier, 1.0, 0.0).astype(jnp.bfloat16))
    rank_full = before + carry_ref[...]
    ranks = [jnp.sum(jnp.where(sel, rank_full, 0.0), axis=0, keepdims=True) for sel in sels]
    rank_ref[...] = jnp.concatenate(ranks, axis=0).astype(jnp.int32)
    carry = carry_ref[...] + jnp.sum(chosen, axis=1, keepdims=True)
    carry_ref[...] = carry
    cnt_ref[...] = jnp.broadcast_to(carry, cnt_ref.shape)


def _router(x, g, sc, sh, router_w, router_b):
    tiles_per_batch = SEQ // ROUTE_TILE
    lane_out = lambda dt: jax.ShapeDtypeStruct((TOP_K, TOKENS), dt)
    lane_spec = pl.BlockSpec((TOP_K, ROUTE_TILE), lambda i: (0, i))
    return pl.pallas_call(
        _router_kernel,
        grid=(TOKENS // ROUTE_TILE,),
        in_specs=[
            pl.BlockSpec((ROUTE_TILE, D_MODEL), lambda i: (i, 0)),
            pl.BlockSpec((1, D_MODEL), lambda i: (0, 0)),
            pl.BlockSpec((1, 1, D_MODEL), lambda i: (i // tiles_per_batch, 0, 0)),
            pl.BlockSpec((1, 1, D_MODEL), lambda i: (i // tiles_per_batch, 0, 0)),
            pl.BlockSpec((N_EXPERTS, D_MODEL), lambda i: (0, 0)),
            pl.BlockSpec((N_EXPERTS, 1), lambda i: (0, 0)),
        ],
        out_specs=[
            pl.BlockSpec((ROUTE_TILE, D_MODEL), lambda i: (i, 0)),
            lane_spec, lane_spec, lane_spec,
            pl.BlockSpec((N_EXPERTS, 128), lambda i: (0, 0)),
        ],
        out_shape=[
            jax.ShapeDtypeStruct((TOKENS, D_MODEL), jnp.float32),
            lane_out(jnp.int32), lane_out(jnp.int32), lane_out(jnp.float32),
            jax.ShapeDtypeStruct((N_EXPERTS, 128), jnp.float32),
        ],
        scratch_shapes=[pltpu.VMEM((N_EXPERTS, 1), jnp.float32)],
        compiler_params=pltpu.CompilerParams(dimension_semantics=("arbitrary",)),
        name="moe_router",
    )(x, g.reshape(1, D_MODEL), sc, sh, router_w.T, router_b.reshape(N_EXPERTS, 1))


def _row_copy(src_ref, src_row, dst_ref, dst_row, sem):
    return pltpu.make_async_copy(src_ref.at[pl.ds(src_row, 1), :],
                                 dst_ref.at[pl.ds(dst_row, 1), :], sem)


def _zero_block_copy(zero_ref, xs_ref, row, sem):
    row = pl.multiple_of(row, MOE_BLOCK)
    return pltpu.make_async_copy(zero_ref, xs_ref.at[pl.ds(row, MOE_BLOCK), :], sem)


def _dispatch_kernel(zstart_ref, nused_ref, pos_ref, h_ref, xs_ref, zero_ref, zsem, sem):
    step = pl.program_id(0)
    n_steps = pl.num_programs(0)

    @pl.when(step == 0)
    def _():
        zero_ref[...] = jnp.zeros_like(zero_ref)

        def each_block(fn):
            def expert_last(e, carry):
                @pl.when(zstart_ref[e] >= 0)
                def _():
                    fn(_zero_block_copy(zero_ref, xs_ref, zstart_ref[e], zsem))
                return carry

            def unused(blk, carry):
                fn(_zero_block_copy(zero_ref, xs_ref, blk * MOE_BLOCK, zsem))
                return carry

            lax.fori_loop(0, N_EXPERTS, expert_last, 0)
            lax.fori_loop(nused_ref[0], MOE_NBLOCKS, unused, 0)

        each_block(lambda copy: copy.start())
        each_block(lambda copy: copy.wait())

    base = step * DISPATCH_TILE

    def issue(t, carry):
        for k in range(TOP_K):
            _row_copy(h_ref, base + t, xs_ref, pos_ref[t * TOP_K + k], sem).start(priority=k % 2)
        return carry

    lax.fori_loop(0, DISPATCH_TILE, issue, 0)

    @pl.when(step == n_steps - 1)
    def _():
        def drain(i, carry):
            pltpu.make_async_copy(h_ref.at[pl.ds(0, DISPATCH_TILE), :],
                                  xs_ref.at[pl.ds(0, DISPATCH_TILE), :], sem).wait()
            return carry

        lax.fori_loop(0, n_steps * TOP_K, drain, 0)


def _dispatch(h, pos_flat, zstart, n_used):
    grid_spec = pltpu.PrefetchScalarGridSpec(
        num_scalar_prefetch=2,
        grid=(TOKENS // DISPATCH_TILE,),
        in_specs=[
            pl.BlockSpec((DISPATCH_TILE * TOP_K,), lambda i, zs, nu: (i,),
                         memory_space=pltpu.SMEM),
            pl.BlockSpec(memory_space=pl.ANY),
        ],
        out_specs=pl.BlockSpec(memory_space=pl.ANY),
        scratch_shapes=[pltpu.VMEM((MOE_BLOCK, D_MODEL), jnp.float32),
                        pltpu.SemaphoreType.DMA, pltpu.SemaphoreType.DMA],
    )
    return pl.pallas_call(
        _dispatch_kernel,
        grid_spec=grid_spec,
        out_shape=jax.ShapeDtypeStruct((MOE_CAP, D_MODEL), jnp.float32),
        compiler_params=pltpu.CompilerParams(dimension_semantics=("arbitrary",)),
        name="moe_dispatch",
    )(zstart, n_used, pos_flat, h)


def _expert_kernel(be_ref, nused_ref, xs_ref, w1_ref, b1_ref, w2_ref, b2_ref, ys_ref,
                   w1b_ref, w2b_ref):
    i = pl.program_id(0)
    new_expert = jnp.logical_or(i == 0, be_ref[i] != be_ref[jnp.maximum(i - 1, 0)])

    @pl.when(new_expert)
    def _():
        for r in range(D_MODEL // 256):
            rows = slice(r * 256, (r + 1) * 256)
            w1b_ref[rows, :] = _bf16(w1_ref[0, rows, :])
            w2b_ref[rows, :] = _bf16(w2_ref[0, rows, :])

    @pl.when(i < nused_ref[0])
    def _():
        x = _bf16(xs_ref[...])
        glu = _dot(x, w1b_ref[:, :D_FF]) + b1_ref[0][:, :D_FF]
        lin = _dot(x, w1b_ref[:, D_FF:]) + b1_ref[0][:, D_FF:]
        glu = jnp.minimum(glu, SWIGLU_LIMIT)
        lin = jnp.clip(lin, -SWIGLU_LIMIT, SWIGLU_LIMIT)
        act = glu * (1.0 / (1.0 + jnp.exp(-SWIGLU_ALPHA * glu))) * (lin + 1.0)
        ys_ref[...] = _dot(_bf16(act), w2b_ref[...]) + b2_ref[0]

    @pl.when(i >= nused_ref[0])
    def _():
        ys_ref[...] = jnp.zeros_like(ys_ref)


def _experts(xs, block_e, n_used, layer, w1, b1, w2, b2):
    grid_spec = pltpu.PrefetchScalarGridSpec(
        num_scalar_prefetch=2,
        grid=(MOE_NBLOCKS,),
        in_specs=[
            pl.BlockSpec((MOE_BLOCK, D_MODEL), lambda i, be, nu: (jnp.minimum(i, nu[0] - 1), 0)),
            pl.BlockSpec((None, 1, D_MODEL, 2 * D_FF), lambda i, be, nu: (layer, be[i], 0, 0)),
            pl.BlockSpec((None, 1, 1, 2 * D_FF), lambda i, be, nu: (layer, be[i], 0, 0)),
            pl.BlockSpec((None, 1, D_FF, D_MODEL), lambda i, be, nu: (layer, be[i], 0, 0)),
            pl.BlockSpec((None, 1, 1, D_MODEL), lambda i, be, nu: (layer, be[i], 0, 0)),
        ],
        out_specs=pl.BlockSpec((MOE_BLOCK, D_MODEL), lambda i, be, nu: (i, 0)),
        scratch_shapes=[pltpu.VMEM((D_MODEL, 2 * D_FF), jnp.bfloat16),
                        pltpu.VMEM((D_FF, D_MODEL), jnp.bfloat16)],
    )
    return pl.pallas_call(
        _expert_kernel,
        grid_spec=grid_spec,
        out_shape=jax.ShapeDtypeStruct((MOE_CAP, D_MODEL), jnp.float32),
        compiler_params=pltpu.CompilerParams(dimension_semantics=("arbitrary",),
                                             vmem_limit_bytes=VMEM_LIMIT_V7X),
        name="moe_experts",
    )(block_e, n_used, xs, w1, b1.reshape(DEPTH, N_EXPERTS, 1, 2 * D_FF), w2,
      b2.reshape(DEPTH, N_EXPERTS, 1, D_MODEL))


def _combine_kernel(pos_ref, pos_next_ref, x_ref, gates_ref, g2_ref, ys_ref, o_ref, buf_ref, sems):
    step = pl.program_id(0)
    n_steps = pl.num_programs(0)
    slot = step % 2

    def gather(p_ref, dst_slot):
        def issue(t, carry):
            for k in range(TOP_K):
                _row_copy(ys_ref, p_ref[t * TOP_K + k], buf_ref.at[dst_slot, k], t,
                          sems.at[dst_slot]).start(priority=k % 2)
            return carry

        lax.fori_loop(0, DISPATCH_TILE, issue, 0)

    @pl.when(step == 0)
    def _():
        gather(pos_ref, 0)

    @pl.when(step + 1 < n_steps)
    def _():
        gather(pos_next_ref, 1 - slot)

    for k in range(TOP_K):
        pltpu.make_async_copy(ys_ref.at[pl.ds(0, DISPATCH_TILE), :], buf_ref.at[slot, k],
                              sems.at[slot]).wait()

    gates = gates_ref[...]
    y = gates[:, 0:1] * buf_ref[slot, 0]
    for k in range(1, TOP_K):
        y = y + gates[:, k:k + 1] * buf_ref[slot, k]
    o_ref[...] = x_ref[...] + g2_ref[0] * y


def _combine(x, pos_flat, gates_tok, g2, ys):
    tiles_per_batch = SEQ // DISPATCH_TILE
    n_steps = TOKENS // DISPATCH_TILE
    return pl.pallas_call(
        _combine_kernel,
        grid=(n_steps,),
        in_specs=[
            pl.BlockSpec((DISPATCH_TILE * TOP_K,), lambda i: (i,), memory_space=pltpu.SMEM),
            pl.BlockSpec((DISPATCH_TILE * TOP_K,), lambda i: (jnp.minimum(i + 1, n_steps - 1),),
                         memory_space=pltpu.SMEM),
            pl.BlockSpec((DISPATCH_TILE, D_MODEL), lambda i: (i, 0)),
            pl.BlockSpec((DISPATCH_TILE, TOP_K), lambda i: (i, 0)),
            pl.BlockSpec((1, 1, D_MODEL), lambda i: (i // tiles_per_batch, 0, 0)),
            pl.BlockSpec(memory_space=pl.ANY),
        ],
        out_specs=pl.BlockSpec((DISPATCH_TILE, D_MODEL), lambda i: (i, 0)),
        out_shape=jax.ShapeDtypeStruct((TOKENS, D_MODEL), jnp.float32),
        scratch_shapes=[pltpu.VMEM((2, TOP_K, DISPATCH_TILE, D_MODEL), jnp.float32),
                        pltpu.SemaphoreType.DMA((2,))],
        compiler_params=pltpu.CompilerParams(dimension_semantics=("arbitrary",),
                                             vmem_limit_bytes=VMEM_LIMIT_V7X),
        name="moe_combine",
    )(pos_flat, pos_flat, x, gates_tok, g2, ys)


def _moe_layer(x, g, sc, sh, g2, router_w, router_b, layer, w1, b1, w2, b2):
    h, idx, rank, gates, cnt = _router(x, g, sc, sh, router_w, router_b)
    counts = cnt[:, 0].astype(jnp.int32)
    padded = ((counts + MOE_BLOCK - 1) // MOE_BLOCK) * MOE_BLOCK
    pends = jnp.cumsum(padded)
    pstarts = pends - padded
    experts = jnp.arange(N_EXPERTS, dtype=jnp.int32)[:, None, None]
    pos = rank + jnp.sum(jnp.where(idx[None] == experts, pstarts[:, None, None], 0), axis=0)
    pos_flat = pos.T.reshape(-1)
    block_start = jnp.arange(MOE_NBLOCKS, dtype=jnp.int32) * MOE_BLOCK
    block_e = jnp.minimum(jnp.sum(block_start[:, None] >= pends[None, :], axis=1),
                          N_EXPERTS - 1).astype(jnp.int32)
    n_used = (pends[-1:] // MOE_BLOCK).astype(jnp.int32)
    zstart = jnp.where(padded > 0, pends - MOE_BLOCK, -1).astype(jnp.int32)
    xs = _dispatch(h, pos_flat, zstart, n_used)
    ys = _experts(xs, block_e, n_used, layer, w1, b1, w2, b2)
    return _combine(x, pos_flat, gates.T, g2, ys)


def kernel(x, c, ada_w, ada_b, norm_mix_g, norm_ffn_g, ev_w_in, ev_w_out, sg_ln_g, sg_ln_b, sg_w, sg_b, od_w_in, od_w_out, od_q_g, od_k_g, od_rel_bias, moe_router_w, moe_router_b, moe_w1, moe_b1, moe_w2, moe_b2):
    mod = _ada_mod(c, ada_w, ada_b)
    xt = x.reshape(TOKENS, D_MODEL)
    for l in range(DEPTH):
        sh1, sc1, g1, sh2, sc2, g2 = [m.reshape(BATCH, 1, D_MODEL)
                                      for m in jnp.split(mod[l], 6, axis=-1)]
        i = l // 2
        if l % 2 == 0:
            proj = _norm_proj(xt, norm_mix_g[l], sc1, sh1, _bf16(ev_w_in[i]))
            a_out = _retention(proj)
            b_out = _spatial_gate(proj, sg_ln_g[i], sg_ln_b[i], sg_w[i], sg_b[i])
            xt = _out_proj([a_out, b_out], _bf16(ev_w_out[i]), xt, g1)
        else:
            qkv = _norm_proj(xt, norm_mix_g[l], sc1, sh1, _bf16(od_w_in[i]))
            att = _band_attention(qkv, od_q_g[i], od_k_g[i], od_rel_bias[i])
            xt = _out_proj([att], _bf16(od_w_out[i]), xt, g1)
        xt = _moe_layer(xt, norm_ffn_g[l], sc2, sh2, g2, moe_router_w[l], moe_router_b[l],
                        l, moe_w1, moe_b1, moe_w2, moe_b2)
    return xt.reshape(BATCH, SEQ, D_MODEL)
```

```python
import functools
import math

import numpy as np
import jax
import jax.numpy as jnp
from jax import lax
from jax.experimental import pallas as pl
from jax.experimental.pallas import tpu as pltpu

D_MODEL = 1024
BATCH = 8
SEQ = 2048
DEPTH = 4
TOKENS = BATCH * SEQ
CHUNK = 64
EPS = 1e-6

RET_HEADS = 4
RET_DK = 128
RET_DV = 256
ROPE_BASE = 10000.0
A_QK = RET_HEADS * RET_DK
A_V = RET_HEADS * RET_DV
SG_GROUPS = 4
SG_CH = 128
SG_WIDTH = SG_GROUPS * SG_CH
SG_LEN = 128
EVEN_IN = 2 * A_QK + 2 * A_V + 2 * SG_WIDTH
EVEN_MIX = A_V + SG_WIDTH

ATT_HEADS = 16
ATT_DH = 64
BAND_PAD = 8 * CHUNK
MAX_REL = 256
REL_SIZE = (CHUNK - 1) + MAX_REL + 1

N_EXPERTS = 32
TOP_K = 4
D_FF = D_MODEL
SWIGLU_ALPHA = 1.702
SWIGLU_LIMIT = 7.0

ROW_TILE = 512
COL_CHUNK = 512
RET_BLOCK = 256
ATT_QBLOCK = 128
ATT_BAND = ATT_QBLOCK + BAND_PAD
ATT_GROUP = 4
ROUTE_TILE = 512
MOE_BLOCK = 256
MOE_CAP = TOKENS * TOP_K + N_EXPERTS * MOE_BLOCK
MOE_NBLOCKS = MOE_CAP // MOE_BLOCK
DISPATCH_TILE = 256
VMEM_LIMIT_V7X = 56 * 1024 * 1024

NEG_BIG = -1e30


def _silu(x):
    return x * (1.0 / (1.0 + jnp.exp(-x)))


def _gelu_tanh(x):
    return 0.5 * x * (1.0 + jnp.tanh(math.sqrt(2.0 / math.pi) * (x + 0.044715 * (x * x * x))))


def _bf16(x):
    return x.astype(jnp.bfloat16)


def _dot(a, b):
    return jnp.dot(a, b, preferred_element_type=jnp.float32)


def _dot_nt(a, b):
    return lax.dot_general(a, b, (((1,), (1,)), ((), ())), preferred_element_type=jnp.float32)


def _dot_tn(a, b):
    return lax.dot_general(a, b, (((0,), (0,)), ((), ())), preferred_element_type=jnp.float32)


def _norm_mod(x, g, sc, sh):
    y = x * lax.rsqrt(jnp.mean(x * x, axis=-1, keepdims=True) + EPS)
    return (y * g) * (1.0 + sc) + sh


def _ada_kernel(c_ref, w_ref, b_ref, o_ref):
    c_act = _silu(c_ref[...])
    o_ref[0] = _dot(c_act, w_ref[0]) + b_ref[0]


def _ada_mod(c, ada_w, ada_b):
    n_col = 6
    return pl.pallas_call(
        _ada_kernel,
        grid=(DEPTH, n_col),
        in_specs=[
            pl.BlockSpec((BATCH, D_MODEL), lambda l, j: (0, 0)),
            pl.BlockSpec((1, D_MODEL, D_MODEL), lambda l, j: (l, 0, j)),
            pl.BlockSpec((1, 1, D_MODEL), lambda l, j: (l, 0, j)),
        ],
        out_specs=pl.BlockSpec((1, BATCH, D_MODEL), lambda l, j: (l, 0, j)),
        out_shape=jax.ShapeDtypeStruct((DEPTH, BATCH, 6 * D_MODEL), jnp.float32),
        name="ada_mod",
    )(c, ada_w, ada_b.reshape(DEPTH, 1, 6 * D_MODEL))


def _norm_proj_kernel(x_ref, g_ref, sc_ref, sh_ref, w_ref, o_ref):
    h = _bf16(_norm_mod(x_ref[...], g_ref[...], sc_ref[0], sh_ref[0]))
    n_out = o_ref.shape[1]
    for j in range(n_out // COL_CHUNK):
        cols = slice(j * COL_CHUNK, (j + 1) * COL_CHUNK)
        o_ref[:, cols] = _bf16(_dot(h, w_ref[:, cols]))


def _norm_proj(x, g, sc, sh, w_bf16):
    n_out = w_bf16.shape[1]
    tiles_per_batch = SEQ // ROW_TILE
    return pl.pallas_call(
        _norm_proj_kernel,
        grid=(TOKENS // ROW_TILE,),
        in_specs=[
            pl.BlockSpec((ROW_TILE, D_MODEL), lambda i: (i, 0)),
            pl.BlockSpec((1, D_MODEL), lambda i: (0, 0)),
            pl.BlockSpec((1, 1, D_MODEL), lambda i: (i // tiles_per_batch, 0, 0)),
            pl.BlockSpec((1, 1, D_MODEL), lambda i: (i // tiles_per_batch, 0, 0)),
            pl.BlockSpec((D_MODEL, n_out), lambda i: (0, 0)),
        ],
        out_specs=pl.BlockSpec((ROW_TILE, n_out), lambda i: (i, 0)),
        out_shape=jax.ShapeDtypeStruct((TOKENS, n_out), jnp.bfloat16),
        compiler_params=pltpu.CompilerParams(vmem_limit_bytes=VMEM_LIMIT_V7X),
        name="norm_proj",
    )(x, g.reshape(1, D_MODEL), sc, sh, w_bf16)


def _out_proj_kernel(*refs, widths):
    part_refs = refs[:len(widths)]
    w_ref, x_ref, g_ref, o_ref = refs[len(widths):]
    for j in range(D_MODEL // COL_CHUNK):
        cols = slice(j * COL_CHUNK, (j + 1) * COL_CHUNK)
        acc = None
        row0 = 0
        for p_ref, width in zip(part_refs, widths):
            term = _dot(p_ref[...], w_ref[row0:row0 + width, cols])
            acc = term if acc is None else acc + term
            row0 += width
        o_ref[:, cols] = x_ref[:, cols] + g_ref[0][:, cols] * acc


def _out_proj(parts, w_bf16, x, gate):
    widths = tuple(p.shape[1] for p in parts)
    tiles_per_batch = SEQ // ROW_TILE
    in_specs = [pl.BlockSpec((ROW_TILE, width), lambda i: (i, 0)) for width in widths]
    in_specs += [
        pl.BlockSpec((sum(widths), D_MODEL), lambda i: (0, 0)),
        pl.BlockSpec((ROW_TILE, D_MODEL), lambda i: (i, 0)),
        pl.BlockSpec((1, 1, D_MODEL), lambda i: (i // tiles_per_batch, 0, 0)),
    ]
    return pl.pallas_call(
        functools.partial(_out_proj_kernel, widths=widths),
        grid=(TOKENS // ROW_TILE,),
        in_specs=in_specs,
        out_specs=pl.BlockSpec((ROW_TILE, D_MODEL), lambda i: (i, 0)),
        out_shape=jax.ShapeDtypeStruct((TOKENS, D_MODEL), jnp.float32),
        compiler_params=pltpu.CompilerParams(vmem_limit_bytes=VMEM_LIMIT_V7X),
        name="out_proj",
    )(*parts, w_bf16, x, gate)


def _retention_tables():
    heads = np.arange(RET_HEADS, dtype=np.float64)
    log_g = np.log1p(-np.exp2(-5.0 - heads))
    idx = np.arange(RET_BLOCK, dtype=np.float64)
    diff = idx[:, None] - idx[None, :]
    ci, cj = (idx // CHUNK)[:, None], (idx // CHUNK)[None, :]
    expo = np.where(ci == cj, np.abs(diff), diff)
    decay = np.where(cj <= ci, np.exp(log_g[:, None, None] * expo[None]), 0.0)
    q_dec = np.exp(log_g[:, None] * (idx[None, :] + 1.0))
    k_dec = np.exp(log_g[:, None] * (RET_BLOCK - 1.0 - idx[None, :]))
    blk_dec = np.exp(log_g * RET_BLOCK)
    q_dec = np.broadcast_to(q_dec[:, :, None], (RET_HEADS, RET_BLOCK, RET_DK))
    k_dec = np.broadcast_to(k_dec[:, :, None], (RET_HEADS, RET_BLOCK, RET_DK))
    blk_dec = np.broadcast_to(blk_dec[:, None, None], (RET_HEADS, 1, RET_DV))
    half = RET_DK // 2
    inv = ROPE_BASE ** (-np.arange(half, dtype=np.float64) / half)
    ang = np.arange(SEQ, dtype=np.float64)[:, None] * inv[None, :]
    cos = np.concatenate([np.cos(ang), np.cos(ang)], axis=1)
    sin = np.concatenate([-np.sin(ang), np.sin(ang)], axis=1)
    f32 = lambda a: jnp.asarray(np.ascontiguousarray(a), jnp.float32)
    return f32(decay), f32(q_dec), f32(k_dec), f32(blk_dec), f32(cos), f32(sin)


def _retention_kernel(q_ref, k_ref, v_ref, gate_ref, cos_ref, sin_ref, dec_ref, qd_ref, kd_ref,
                      bd_ref, o_ref, state_ref):
    @pl.when(pl.program_id(2) == 0)
    def _():
        state_ref[...] = jnp.zeros_like(state_ref)

    cos, sin = cos_ref[...], sin_ref[...]
    q = q_ref[...].astype(jnp.float32)
    k = k_ref[...].astype(jnp.float32)
    qr = q * cos + pltpu.roll(q, RET_DK // 2, axis=1) * sin
    kr = (k * cos + pltpu.roll(k, RET_DK // 2, axis=1) * sin) * (RET_DK ** -0.5)
    v = v_ref[...]

    scores = _dot_nt(_bf16(qr), _bf16(kr)) * dec_ref[0]
    intra = _dot(_bf16(scores), v)
    state = state_ref[...]
    inter = _dot(_bf16(qr * qd_ref[0]), _bf16(state))
    state_ref[...] = state * bd_ref[0] + _dot_tn(_bf16(kr * kd_ref[0]), v)

    o = intra + inter
    mu = jnp.mean(o, axis=-1, keepdims=True)
    cen = o - mu
    var = jnp.mean(cen * cen, axis=-1, keepdims=True)
    normed = cen * lax.rsqrt(var + EPS)
    o_ref[...] = _bf16(_silu(gate_ref[...].astype(jnp.float32)) * normed)


def _retention(proj):
    decay, q_dec, k_dec, blk_dec, cos, sin = _retention_tables()
    nblk = SEQ // RET_BLOCK
    row = lambda b, h, n: b * nblk + n
    return pl.pallas_call(
        _retention_kernel,
        grid=(BATCH, RET_HEADS, nblk),
        in_specs=[
            pl.BlockSpec((RET_BLOCK, RET_DK), lambda b, h, n: (row(b, h, n), h)),
            pl.BlockSpec((RET_BLOCK, RET_DK), lambda b, h, n: (row(b, h, n), A_QK // RET_DK + h)),
            pl.BlockSpec((RET_BLOCK, RET_DV), lambda b, h, n: (row(b, h, n), 2 * A_QK // RET_DV + h)),
            pl.BlockSpec((RET_BLOCK, RET_DV),
                         lambda b, h, n: (row(b, h, n), (2 * A_QK + A_V) // RET_DV + h)),
            pl.BlockSpec((RET_BLOCK, RET_DK), lambda b, h, n: (n, 0)),
            pl.BlockSpec((RET_BLOCK, RET_DK), lambda b, h, n: (n, 0)),
            pl.BlockSpec((1, RET_BLOCK, RET_BLOCK), lambda b, h, n: (h, 0, 0)),
            pl.BlockSpec((1, RET_BLOCK, RET_DK), lambda b, h, n: (h, 0, 0)),
            pl.BlockSpec((1, RET_BLOCK, RET_DK), lambda b, h, n: (h, 0, 0)),
            pl.BlockSpec((1, 1, RET_DV), lambda b, h, n: (h, 0, 0)),
        ],
        out_specs=pl.BlockSpec((RET_BLOCK, RET_DV), lambda b, h, n: (row(b, h, n), h)),
        out_shape=jax.ShapeDtypeStruct((TOKENS, A_V), jnp.bfloat16),
        scratch_shapes=[pltpu.VMEM((RET_DK, RET_DV), jnp.float32)],
        compiler_params=pltpu.CompilerParams(
            dimension_semantics=("arbitrary", "arbitrary", "arbitrary")),
        name="retention",
    )(proj, proj, proj, proj, cos, sin, decay, q_dec, k_dec, blk_dec)


def _spatial_gate_kernel(u_ref, z_ref, lng_ref, lnb_ref, w_ref, b_ref, o_ref):
    u = _gelu_tanh(u_ref[...].astype(jnp.float32))
    z = _gelu_tanh(z_ref[...].astype(jnp.float32))
    mu = jnp.mean(z, axis=-1, keepdims=True)
    cen = z - mu
    var = jnp.mean(cen * cen, axis=-1, keepdims=True)
    zn = _bf16(cen * lax.rsqrt(var + EPS) * lng_ref[...] + lnb_ref[...])
    rows = lax.broadcasted_iota(jnp.int32, (SG_LEN, SG_LEN), 0)
    cols = lax.broadcasted_iota(jnp.int32, (SG_LEN, SG_LEN), 1)
    keep = (rows // CHUNK) >= (cols // CHUNK)
    for g in range(SG_GROUPS):
        cs = slice(g * SG_CH, (g + 1) * SG_CH)
        w = _bf16(jnp.where(keep, w_ref[g], 0.0))
        mixed = _dot(w, zn[:, cs]) + b_ref[g]
        o_ref[:, cs] = _bf16(u[:, cs] * mixed)


def _spatial_gate(proj, ln_g, ln_b, w_s, b_s):
    u_col = (2 * A_QK + 2 * A_V) // SG_WIDTH
    b_full = jnp.broadcast_to(b_s[:, :, None], (SG_GROUPS, SG_LEN, SG_CH))
    return pl.pallas_call(
        _spatial_gate_kernel,
        grid=(TOKENS // SG_LEN,),
        in_specs=[
            pl.BlockSpec((SG_LEN, SG_WIDTH), lambda i: (i, u_col)),
            pl.BlockSpec((SG_LEN, SG_WIDTH), lambda i: (i, u_col + 1)),
            pl.BlockSpec((1, SG_WIDTH), lambda i: (0, 0)),
            pl.BlockSpec((1, SG_WIDTH), lambda i: (0, 0)),
            pl.BlockSpec((SG_GROUPS, SG_LEN, SG_LEN), lambda i: (0, 0, 0)),
            pl.BlockSpec((SG_GROUPS, SG_LEN, SG_CH), lambda i: (0, 0, 0)),
        ],
        out_specs=pl.BlockSpec((SG_LEN, SG_WIDTH), lambda i: (i, 0)),
        out_shape=jax.ShapeDtypeStruct((TOKENS, SG_WIDTH), jnp.bfloat16),
        name="spatial_gate",
    )(proj, proj, ln_g.reshape(1, SG_WIDTH), ln_b.reshape(1, SG_WIDTH), w_s, b_full)


def _band_bias(rel_bias):
    n_rel = ATT_QBLOCK + ATT_BAND - 1
    rel_rev = (ATT_BAND - 1) - np.arange(n_rel)
    table = rel_bias[:, np.clip(rel_rev, -(CHUNK - 1), MAX_REL) + (CHUNK - 1)].astype(jnp.float32)
    rows = [table[:, ATT_QBLOCK - 1 - i:ATT_QBLOCK - 1 - i + ATT_BAND] for i in range(ATT_QBLOCK)]
    bias = jnp.stack(rows, axis=1)
    i = np.arange(ATT_QBLOCK)[:, None]
    jj = np.arange(ATT_BAND)[None, :]
    in_band = (jj // CHUNK >= i // CHUNK) & (jj // CHUNK <= i // CHUNK + BAND_PAD // CHUNK)
    return jnp.where(jnp.asarray(in_band)[None], bias, NEG_BIG)


def _head_rms(x, gain):
    head_of_lane = lax.broadcasted_iota(jnp.int32, x.shape, 1) // ATT_DH
    sq = x * x
    ms = jnp.zeros_like(x)
    for h in range(ATT_GROUP):
        in_head = head_of_lane == h
        ms = jnp.where(in_head, jnp.sum(jnp.where(in_head, sq, 0.0), axis=-1, keepdims=True), ms)
    return x * lax.rsqrt(ms * (1.0 / ATT_DH) + EPS) * gain


def _band_attn_kernel(q_ref, k_ref, v_ref, bias_ref, qg_ref, kg_ref, o_ref, kpad_ref, vpad_ref):
    n = pl.program_id(2)
    width = ATT_GROUP * ATT_DH

    @pl.when(n == 0)
    def _():
        kpad_ref[0:BAND_PAD, :] = jnp.zeros((BAND_PAD, width), jnp.bfloat16)
        vpad_ref[0:BAND_PAD, :] = jnp.zeros((BAND_PAD, width), jnp.bfloat16)
        kpad_ref[BAND_PAD:, :] = _bf16(_head_rms(k_ref[...].astype(jnp.float32), kg_ref[...]))
        vpad_ref[BAND_PAD:, :] = v_ref[...]

    q = _bf16(_head_rms(q_ref[...].astype(jnp.float32), qg_ref[...]) * (ATT_DH ** -0.5))
    start = pl.multiple_of(n * ATT_QBLOCK, ATT_QBLOCK)
    kb = kpad_ref[pl.ds(start, ATT_BAND), :]
    vb = vpad_ref[pl.ds(start, ATT_BAND), :]
    key_pos = lax.broadcasted_iota(jnp.int32, (ATT_QBLOCK, ATT_BAND), 1) + start
    valid = key_pos >= BAND_PAD
    outs = []
    for h in range(ATT_GROUP):
        hs = slice(h * ATT_DH, (h + 1) * ATT_DH)
        s = _dot_nt(q[:, hs], kb[:, hs]) + bias_ref[h]
        s = jnp.where(valid, s, NEG_BIG)
        m = jnp.max(s, axis=-1, keepdims=True)
        p = jnp.exp(s - m)
        denom = jnp.sum(p, axis=-1, keepdims=True)
        outs.append(_dot(_bf16(p), vb[:, hs]) * (1.0 / denom))
    o_ref[...] = _bf16(jnp.concatenate(outs, axis=-1))


def _band_attention(qkv, q_g, k_g, rel_bias):
    bias = _band_bias(rel_bias)
    nq = SEQ // ATT_QBLOCK
    width = ATT_GROUP * ATT_DH
    n_groups = ATT_HEADS // ATT_GROUP
    qg = jnp.tile(q_g, ATT_GROUP).reshape(1, width)
    kg = jnp.tile(k_g, ATT_GROUP).reshape(1, width)
    return pl.pallas_call(
        _band_attn_kernel,
        grid=(BATCH, n_groups, nq),
        in_specs=[
            pl.BlockSpec((ATT_QBLOCK, width), lambda b, hg, n: (b * nq + n, hg)),
            pl.BlockSpec((SEQ, width), lambda b, hg, n: (b, n_groups + hg)),
            pl.BlockSpec((SEQ, width), lambda b, hg, n: (b, 2 * n_groups + hg)),
            pl.BlockSpec((ATT_GROUP, ATT_QBLOCK, ATT_BAND), lambda b, hg, n: (hg, 0, 0)),
            pl.BlockSpec((1, width), lambda b, hg, n: (0, 0)),
            pl.BlockSpec((1, width), lambda b, hg, n: (0, 0)),
        ],
        out_specs=pl.BlockSpec((ATT_QBLOCK, width), lambda b, hg, n: (b * nq + n, hg)),
        out_shape=jax.ShapeDtypeStruct((TOKENS, D_MODEL), jnp.bfloat16),
        scratch_shapes=[pltpu.VMEM((SEQ + BAND_PAD, width), jnp.bfloat16),
                        pltpu.VMEM((SEQ + BAND_PAD, width), jnp.bfloat16)],
        compiler_params=pltpu.CompilerParams(
            dimension_semantics=("arbitrary", "arbitrary", "arbitrary")),
        name="band_attention",
    )(qkv, qkv, qkv, bias, qg, kg)


def _split_bf16(x):
    hi = _bf16(x)
    lo = _bf16(x - hi.astype(jnp.float32))
    return hi, lo


def _router_kernel(x_ref, g_ref, sc_ref, sh_ref, rw_ref, rb_ref,
                   h_ref, idx_ref, rank_ref, gate_ref, cnt_ref, carry_ref):
    @pl.when(pl.program_id(0) == 0)
    def _():
        carry_ref[...] = jnp.zeros_like(carry_ref)

    h = _norm_mod(x_ref[...], g_ref[...], sc_ref[0], sh_ref[0])
    h_ref[...] = h
    h_hi, h_lo = _split_bf16(h)
    w_hi, w_lo = _split_bf16(rw_ref[...])
    logits = _dot_nt(w_hi, h_hi) + _dot_nt(w_hi, h_lo) + _dot_nt(w_lo, h_hi) + rb_ref[...]

    expert = lax.broadcasted_iota(jnp.int32, logits.shape, 0).astype(jnp.float32)
    work = logits
    vals, idxs, sels = [], [], []
    for _ in range(TOP_K):
        m = jnp.max(work, axis=0, keepdims=True)
        pick = jnp.min(jnp.where(work == m, expert, float(N_EXPERTS)), axis=0, keepdims=True)
        sel = expert == pick
        work = jnp.where(sel, -jnp.inf, work)
        vals.append(m)
        idxs.append(pick)
        sels.append(sel)
    exps = [jnp.exp(v - vals[0]) for v in vals]
    denom = exps[0] + exps[1] + exps[2] + exps[3]
    gate_ref[...] = jnp.concatenate([e / denom for e in exps], axis=0)
    idx_ref[...] = jnp.concatenate(idxs, axis=0).astype(jnp.int32)

    chosen = jnp.zeros(logits.shape, jnp.float32)
    for sel in sels:
        chosen = jnp.where(sel, 1.0, chosen)
    tile = logits.shape[1]
    earlier = (lax.broadcasted_iota(jnp.int32, (tile, tile), 0)
               < lax.broadcasted_iota(jnp.int32, (tile, tile), 1))
    before = _dot(_bf16(chosen), jnp.where(earlier, 1.0, 0.0).astype(jnp.bfloat16))
    rank_full = before + carry_ref[...]
    ranks = [jnp.sum(jnp.where(sel, rank_full, 0.0), axis=0, keepdims=True) for sel in sels]
    rank_ref[...] = jnp.concatenate(ranks, axis=0).astype(jnp.int32)
    carry = carry_ref[...] + jnp.sum(chosen, axis=1, keepdims=True)
    carry_ref[...] = carry
    cnt_ref[...] = jnp.broadcast_to(carry, cnt_ref.shape)


def _router(x, g, sc, sh, router_w, router_b):
    tiles_per_batch = SEQ // ROUTE_TILE
    lane_out = lambda dt: jax.ShapeDtypeStruct((TOP_K, TOKENS), dt)
    lane_spec = pl.BlockSpec((TOP_K, ROUTE_TILE), lambda i: (0, i))
    return pl.pallas_call(
        _router_kernel,
        grid=(TOKENS // ROUTE_TILE,),
        in_specs=[
            pl.BlockSpec((ROUTE_TILE, D_MODEL), lambda i: (i, 0)),
            pl.BlockSpec((1, D_MODEL), lambda i: (0, 0)),
            pl.BlockSpec((1, 1, D_MODEL), lambda i: (i // tiles_per_batch, 0, 0)),
            pl.BlockSpec((1, 1, D_MODEL), lambda i: (i // tiles_per_batch, 0, 0)),
            pl.BlockSpec((N_EXPERTS, D_MODEL), lambda i: (0, 0)),
            pl.BlockSpec((N_EXPERTS, 1), lambda i: (0, 0)),
        ],
        out_specs=[
            pl.BlockSpec((ROUTE_TILE, D_MODEL), lambda i: (i, 0)),
            lane_spec, lane_spec, lane_spec,
            pl.BlockSpec((N_EXPERTS, 128), lambda i: (0, 0)),
        ],
        out_shape=[
            jax.ShapeDtypeStruct((TOKENS, D_MODEL), jnp.float32),
            lane_out(jnp.int32), lane_out(jnp.int32), lane_out(jnp.float32),
            jax.ShapeDtypeStruct((N_EXPERTS, 128), jnp.float32),
        ],
        scratch_shapes=[pltpu.VMEM((N_EXPERTS, 1), jnp.float32)],
        compiler_params=pltpu.CompilerParams(dimension_semantics=("arbitrary",)),
        name="moe_router",
    )(x, g.reshape(1, D_MODEL), sc, sh, router_w.T, router_b.reshape(N_EXPERTS, 1))


def _row_copy(src_ref, src_row, dst_ref, dst_row, sem):
    return pltpu.make_async_copy(src_ref.at[pl.ds(src_row, 1), :],
                                 dst_ref.at[pl.ds(dst_row, 1), :], sem)


def _zero_block_copy(zero_ref, xs_ref, row, sem):
    row = pl.multiple_of(row, MOE_BLOCK)
    return pltpu.make_async_copy(zero_ref, xs_ref.at[pl.ds(row, MOE_BLOCK), :], sem)


def _dispatch_kernel(zstart_ref, nused_ref, pos_ref, h_ref, xs_ref, zero_ref, zsem, sem):
    step = pl.program_id(0)

    @pl.when(step == 0)
    def _():
        zero_ref[...] = jnp.zeros_like(zero_ref)

        def each_block(fn):
            def expert_last(e, carry):
                @pl.when(zstart_ref[e] >= 0)
                def _():
                    fn(_zero_block_copy(zero_ref, xs_ref, zstart_ref[e], zsem))
                return carry

            def unused(blk, carry):
                fn(_zero_block_copy(zero_ref, xs_ref, blk * MOE_BLOCK, zsem))
                return carry

            lax.fori_loop(0, N_EXPERTS, expert_last, 0)
            lax.fori_loop(nused_ref[0], MOE_NBLOCKS, unused, 0)

        each_block(lambda copy: copy.start())
        each_block(lambda copy: copy.wait())

    def issue(t, carry):
        for k in range(TOP_K):
            _row_copy(h_ref, t, xs_ref, pos_ref[t * TOP_K + k], sem).start(priority=k % 2)
        return carry

    lax.fori_loop(0, DISPATCH_TILE, issue, 0)

    for _ in range(TOP_K):
        pltpu.make_async_copy(h_ref, xs_ref.at[pl.ds(0, DISPATCH_TILE), :], sem).wait()


def _dispatch(h, pos_flat, zstart, n_used):
    grid_spec = pltpu.PrefetchScalarGridSpec(
        num_scalar_prefetch=2,
        grid=(TOKENS // DISPATCH_TILE,),
        in_specs=[
            pl.BlockSpec((DISPATCH_TILE * TOP_K,), lambda i, zs, nu: (i,),
                         memory_space=pltpu.SMEM),
            pl.BlockSpec((DISPATCH_TILE, D_MODEL), lambda i, zs, nu: (i, 0)),
        ],
        out_specs=pl.BlockSpec(memory_space=pl.ANY),
        scratch_shapes=[pltpu.VMEM((MOE_BLOCK, D_MODEL), jnp.float32),
                        pltpu.SemaphoreType.DMA, pltpu.SemaphoreType.DMA],
    )
    return pl.pallas_call(
        _dispatch_kernel,
        grid_spec=grid_spec,
        out_shape=jax.ShapeDtypeStruct((MOE_CAP, D_MODEL), jnp.float32),
        compiler_params=pltpu.CompilerParams(dimension_semantics=("arbitrary",)),
        name="moe_dispatch",
    )(zstart, n_used, pos_flat, h)


def _expert_kernel(be_ref, nused_ref, xs_ref, w1_ref, b1_ref, w2_ref, b2_ref, ys_ref,
                   w1b_ref, w2b_ref):
    i = pl.program_id(0)
    new_expert = jnp.logical_or(i == 0, be_ref[i] != be_ref[jnp.maximum(i - 1, 0)])

    @pl.when(new_expert)
    def _():
        for r in range(D_MODEL // 256):
            rows = slice(r * 256, (r + 1) * 256)
            w1b_ref[rows, :] = _bf16(w1_ref[0, rows, :])
            w2b_ref[rows, :] = _bf16(w2_ref[0, rows, :])

    @pl.when(i < nused_ref[0])
    def _():
        x = _bf16(xs_ref[...])
        glu = _dot(x, w1b_ref[:, :D_FF]) + b1_ref[0][:, :D_FF]
        lin = _dot(x, w1b_ref[:, D_FF:]) + b1_ref[0][:, D_FF:]
        glu = jnp.minimum(glu, SWIGLU_LIMIT)
        lin = jnp.clip(lin, -SWIGLU_LIMIT, SWIGLU_LIMIT)
        act = glu * (1.0 / (1.0 + jnp.exp(-SWIGLU_ALPHA * glu))) * (lin + 1.0)
        ys_ref[...] = _dot(_bf16(act), w2b_ref[...]) + b2_ref[0]

    @pl.when(i >= nused_ref[0])
    def _():
        ys_ref[...] = jnp.zeros_like(ys_ref)


def _experts(xs, block_e, n_used, layer, w1, b1, w2, b2):
    grid_spec = pltpu.PrefetchScalarGridSpec(
        num_scalar_prefetch=2,
        grid=(MOE_NBLOCKS,),
        in_specs=[
            pl.BlockSpec((MOE_BLOCK, D_MODEL), lambda i, be, nu: (jnp.minimum(i, nu[0] - 1), 0)),
            pl.BlockSpec((None, 1, D_MODEL, 2 * D_FF), lambda i, be, nu: (layer, be[i], 0, 0)),
            pl.BlockSpec((None, 1, 1, 2 * D_FF), lambda i, be, nu: (layer, be[i], 0, 0)),
            pl.BlockSpec((None, 1, D_FF, D_MODEL), lambda i, be, nu: (layer, be[i], 0, 0)),
            pl.BlockSpec((None, 1, 1, D_MODEL), lambda i, be, nu: (layer, be[i], 0, 0)),
        ],
        out_specs=pl.BlockSpec((MOE_BLOCK, D_MODEL), lambda i, be, nu: (i, 0)),
        scratch_shapes=[pltpu.VMEM((D_MODEL, 2 * D_FF), jnp.bfloat16),
                        pltpu.VMEM((D_FF, D_MODEL), jnp.bfloat16)],
    )
    return pl.pallas_call(
        _expert_kernel,
        grid_spec=grid_spec,
        out_shape=jax.ShapeDtypeStruct((MOE_CAP, D_MODEL), jnp.float32),
        compiler_params=pltpu.CompilerParams(dimension_semantics=("arbitrary",),
                                             vmem_limit_bytes=VMEM_LIMIT_V7X),
        name="moe_experts",
    )(block_e, n_used, xs, w1, b1.reshape(DEPTH, N_EXPERTS, 1, 2 * D_FF), w2,
      b2.reshape(DEPTH, N_EXPERTS, 1, D_MODEL))


def _combine_kernel(pos_ref, pos_next_ref, x_ref, gates_ref, g2_ref, ys_ref, o_ref, buf_ref, sems):
    step = pl.program_id(0)
    n_steps = pl.num_programs(0)
    slot = step % 2

    def gather(p_ref, dst_slot):
        def issue(t, carry):
            for k in range(TOP_K):
                _row_copy(ys_ref, p_ref[t * TOP_K + k], buf_ref.at[dst_slot, k], t,
                          sems.at[dst_slot]).start(priority=k % 2)
            return carry

        lax.fori_loop(0, DISPATCH_TILE, issue, 0)

    @pl.when(step == 0)
    def _():
        gather(pos_ref, 0)

    @pl.when(step + 1 < n_steps)
    def _():
        gather(pos_next_ref, 1 - slot)

    for k in range(TOP_K):
        pltpu.make_async_copy(ys_ref.at[pl.ds(0, DISPATCH_TILE), :], buf_ref.at[slot, k],
                              sems.at[slot]).wait()

    gates = gates_ref[...]
    y = gates[:, 0:1] * buf_ref[slot, 0]
    for k in range(1, TOP_K):
        y = y + gates[:, k:k + 1] * buf_ref[slot, k]
    o_ref[...] = x_ref[...] + g2_ref[0] * y


def _combine(x, pos_flat, gates_tok, g2, ys):
    tiles_per_batch = SEQ // DISPATCH_TILE
    n_steps = TOKENS // DISPATCH_TILE
    return pl.pallas_call(
        _combine_kernel,
        grid=(n_steps,),
        in_specs=[
            pl.BlockSpec((DISPATCH_TILE * TOP_K,), lambda i: (i,), memory_space=pltpu.SMEM),
            pl.BlockSpec((DISPATCH_TILE * TOP_K,), lambda i: (jnp.minimum(i + 1, n_steps - 1),),
                         memory_space=pltpu.SMEM),
            pl.BlockSpec((DISPATCH_TILE, D_MODEL), lambda i: (i, 0)),
            pl.BlockSpec((DISPATCH_TILE, TOP_K), lambda i: (i, 0)),
            pl.BlockSpec((1, 1, D_MODEL), lambda i: (i // tiles_per_batch, 0, 0)),
            pl.BlockSpec(memory_space=pl.ANY),
        ],
        out_specs=pl.BlockSpec((DISPATCH_TILE, D_MODEL), lambda i: (i, 0)),
        out_shape=jax.ShapeDtypeStruct((TOKENS, D_MODEL), jnp.float32),
        scratch_shapes=[pltpu.VMEM((2, TOP_K, DISPATCH_TILE, D_MODEL), jnp.float32),
                        pltpu.SemaphoreType.DMA((2,))],
        compiler_params=pltpu.CompilerParams(dimension_semantics=("arbitrary",),
                                             vmem_limit_bytes=VMEM_LIMIT_V7X),
        name="moe_combine",
    )(pos_flat, pos_flat, x, gates_tok, g2, ys)


def _moe_layer(x, g, sc, sh, g2, router_w, router_b, layer, w1, b1, w2, b2):
    h, idx, rank, gates, cnt = _router(x, g, sc, sh, router_w, router_b)
    counts = cnt[:, 0].astype(jnp.int32)
    padded = ((counts + MOE_BLOCK - 1) // MOE_BLOCK) * MOE_BLOCK
    pends = jnp.cumsum(padded)
    pstarts = pends - padded
    experts = jnp.arange(N_EXPERTS, dtype=jnp.int32)[:, None, None]
    pos = rank + jnp.sum(jnp.where(idx[None] == experts, pstarts[:, None, None], 0), axis=0)
    pos_flat = pos.T.reshape(-1)
    block_start = jnp.arange(MOE_NBLOCKS, dtype=jnp.int32) * MOE_BLOCK
    block_e = jnp.minimum(jnp.sum(block_start[:, None] >= pends[None, :], axis=1),
                          N_EXPERTS - 1).astype(jnp.int32)
    n_used = (pends[-1:] // MOE_BLOCK).astype(jnp.int32)
    zstart = jnp.where(padded > 0, pends - MOE_BLOCK, -1).astype(jnp.int32)
    xs = _dispatch(h, pos_flat, zstart, n_used)
    ys = _experts(xs, block_e, n_used, layer, w1, b1, w2, b2)
    return _combine(x, pos_flat, gates.T, g2, ys)


def kernel(x, c, ada_w, ada_b, norm_mix_g, norm_ffn_g, ev_w_in, ev_w_out, sg_ln_g, sg_ln_b, sg_w, sg_b, od_w_in, od_w_out, od_q_g, od_k_g, od_rel_bias, moe_router_w, moe_router_b, moe_w1, moe_b1, moe_w2, moe_b2):
    mod = _ada_mod(c, ada_w, ada_b)
    xt = x.reshape(TOKENS, D_MODEL)
    for l in range(DEPTH):
        sh1, sc1, g1, sh2, sc2, g2 = [m.reshape(BATCH, 1, D_MODEL)
                                      for m in jnp.split(mod[l], 6, axis=-1)]
        i = l // 2
        if l % 2 == 0:
            proj = _norm_proj(xt, norm_mix_g[l], sc1, sh1, _bf16(ev_w_in[i]))
            a_out = _retention(proj)
            b_out = _spatial_gate(proj, sg_ln_g[i], sg_ln_b[i], sg_w[i], sg_b[i])
            xt = _out_proj([a_out, b_out], _bf16(ev_w_out[i]), xt, g1)
        else:
            qkv = _norm_proj(xt, norm_mix_g[l], sc1, sh1, _bf16(od_w_in[i]))
            att = _band_attention(qkv, od_q_g[i], od_k_g[i], od_rel_bias[i])
            xt = _out_proj([att], _bf16(od_w_out[i]), xt, g1)
        xt = _moe_layer(xt, norm_ffn_g[l], sc2, sh2, g2, moe_router_w[l], moe_router_b[l],
                        l, moe_w1, moe_b1, moe_w2, moe_b2)
    return xt.reshape(BATCH, SEQ, D_MODEL)
```

```python
import functools
import math

import numpy as np
import jax
import jax.numpy as jnp
from jax import lax
from jax.experimental import pallas as pl
from jax.experimental.pallas import tpu as pltpu

D_MODEL = 1024
BATCH = 8
SEQ = 2048
DEPTH = 4
TOKENS = BATCH * SEQ
CHUNK = 64
EPS = 1e-6

RET_HEADS = 4
RET_DK = 128
RET_DV = 256
ROPE_BASE = 10000.0
A_QK = RET_HEADS * RET_DK
A_V = RET_HEADS * RET_DV
SG_GROUPS = 4
SG_CH = 128
SG_WIDTH = SG_GROUPS * SG_CH
SG_LEN = 128
EVEN_IN = 2 * A_QK + 2 * A_V + 2 * SG_WIDTH
EVEN_MIX = A_V + SG_WIDTH

ATT_HEADS = 16
ATT_DH = 64
BAND_PAD = 8 * CHUNK
MAX_REL = 256
REL_SIZE = (CHUNK - 1) + MAX_REL + 1

N_EXPERTS = 32
TOP_K = 4
D_FF = D_MODEL
SWIGLU_ALPHA = 1.702
SWIGLU_LIMIT = 7.0

ROW_TILE = 512
COL_CHUNK = 512
RET_BLOCK = 256
ATT_QBLOCK = 128
ATT_BAND = ATT_QBLOCK + BAND_PAD
ATT_GROUP = 4
ROUTE_TILE = 512
MOE_BLOCK = 256
MOE_CAP = TOKENS * TOP_K + N_EXPERTS * MOE_BLOCK
MOE_NBLOCKS = MOE_CAP // MOE_BLOCK
DISPATCH_TILE = 256
VMEM_LIMIT_V7X = 56 * 1024 * 1024
ROW_TILES = D_MODEL // 128

NEG_BIG = -1e30
LOG2_E = math.log2(math.e)


def _silu(x):
    return x * (1.0 / (1.0 + jnp.exp(-x)))


def _gelu_tanh(x):
    return 0.5 * x * (1.0 + jnp.tanh(math.sqrt(2.0 / math.pi) * (x + 0.044715 * (x * x * x))))


def _bf16(x):
    return x.astype(jnp.bfloat16)


def _dot(a, b):
    return jnp.dot(a, b, preferred_element_type=jnp.float32)


def _dot_nt(a, b):
    return lax.dot_general(a, b, (((1,), (1,)), ((), ())), preferred_element_type=jnp.float32)


def _dot_tn(a, b):
    return lax.dot_general(a, b, (((0,), (0,)), ((), ())), preferred_element_type=jnp.float32)


def _norm_mod(x, g, sc, sh):
    y = x * lax.rsqrt(jnp.mean(x * x, axis=-1, keepdims=True) + EPS)
    return (y * g) * (1.0 + sc) + sh


def _ada_kernel(c_ref, w_ref, b_ref, o_ref):
    c_act = _silu(c_ref[...])
    o_ref[0] = _dot(c_act, w_ref[0]) + b_ref[0]


def _ada_mod(c, ada_w, ada_b):
    n_col = 6
    return pl.pallas_call(
        _ada_kernel,
        grid=(DEPTH, n_col),
        in_specs=[
            pl.BlockSpec((BATCH, D_MODEL), lambda l, j: (0, 0)),
            pl.BlockSpec((1, D_MODEL, D_MODEL), lambda l, j: (l, 0, j)),
            pl.BlockSpec((1, 1, D_MODEL), lambda l, j: (l, 0, j)),
        ],
        out_specs=pl.BlockSpec((1, BATCH, D_MODEL), lambda l, j: (l, 0, j)),
        out_shape=jax.ShapeDtypeStruct((DEPTH, BATCH, 6 * D_MODEL), jnp.float32),
        name="ada_mod",
    )(c, ada_w, ada_b.reshape(DEPTH, 1, 6 * D_MODEL))


def _norm_proj_kernel(x_ref, g_ref, sc_ref, sh_ref, w_ref, o_ref):
    h = _bf16(_norm_mod(x_ref[...], g_ref[...], sc_ref[0], sh_ref[0]))
    n_out = o_ref.shape[1]
    for j in range(n_out // COL_CHUNK):
        cols = slice(j * COL_CHUNK, (j + 1) * COL_CHUNK)
        o_ref[:, cols] = _bf16(_dot(h, w_ref[:, cols]))


def _norm_proj(x, g, sc, sh, w_bf16):
    n_out = w_bf16.shape[1]
    tiles_per_batch = SEQ // ROW_TILE
    return pl.pallas_call(
        _norm_proj_kernel,
        grid=(TOKENS // ROW_TILE,),
        in_specs=[
            pl.BlockSpec((ROW_TILE, D_MODEL), lambda i: (i, 0)),
            pl.BlockSpec((1, D_MODEL), lambda i: (0, 0)),
            pl.BlockSpec((1, 1, D_MODEL), lambda i: (i // tiles_per_batch, 0, 0)),
            pl.BlockSpec((1, 1, D_MODEL), lambda i: (i // tiles_per_batch, 0, 0)),
            pl.BlockSpec((D_MODEL, n_out), lambda i: (0, 0)),
        ],
        out_specs=pl.BlockSpec((ROW_TILE, n_out), lambda i: (i, 0)),
        out_shape=jax.ShapeDtypeStruct((TOKENS, n_out), jnp.bfloat16),
        compiler_params=pltpu.CompilerParams(vmem_limit_bytes=VMEM_LIMIT_V7X),
        name="norm_proj",
    )(x, g.reshape(1, D_MODEL), sc, sh, w_bf16)


def _out_proj_kernel(*refs, widths):
    part_refs = refs[:len(widths)]
    w_ref, x_ref, g_ref, o_ref = refs[len(widths):]
    for j in range(D_MODEL // COL_CHUNK):
        cols = slice(j * COL_CHUNK, (j + 1) * COL_CHUNK)
        acc = None
        row0 = 0
        for p_ref, width in zip(part_refs, widths):
            term = _dot(p_ref[...], w_ref[row0:row0 + width, cols])
            acc = term if acc is None else acc + term
            row0 += width
        o_ref[:, cols] = x_ref[:, cols] + g_ref[0][:, cols] * acc


def _out_proj(parts, w_bf16, x, gate):
    widths = tuple(p.shape[1] for p in parts)
    tiles_per_batch = SEQ // ROW_TILE
    in_specs = [pl.BlockSpec((ROW_TILE, width), lambda i: (i, 0)) for width in widths]
    in_specs += [
        pl.BlockSpec((sum(widths), D_MODEL), lambda i: (0, 0)),
        pl.BlockSpec((ROW_TILE, D_MODEL), lambda i: (i, 0)),
        pl.BlockSpec((1, 1, D_MODEL), lambda i: (i // tiles_per_batch, 0, 0)),
    ]
    return pl.pallas_call(
        functools.partial(_out_proj_kernel, widths=widths),
        grid=(TOKENS // ROW_TILE,),
        in_specs=in_specs,
        out_specs=pl.BlockSpec((ROW_TILE, D_MODEL), lambda i: (i, 0)),
        out_shape=jax.ShapeDtypeStruct((TOKENS, D_MODEL), jnp.float32),
        compiler_params=pltpu.CompilerParams(vmem_limit_bytes=VMEM_LIMIT_V7X),
        name="out_proj",
    )(*parts, w_bf16, x, gate)


def _retention_tables():
    heads = np.arange(RET_HEADS, dtype=np.float64)
    log_g = np.log1p(-np.exp2(-5.0 - heads))
    idx = np.arange(RET_BLOCK, dtype=np.float64)
    diff = idx[:, None] - idx[None, :]
    ci, cj = (idx // CHUNK)[:, None], (idx // CHUNK)[None, :]
    expo = np.where(ci == cj, np.abs(diff), diff)
    decay = np.where(cj <= ci, np.exp(log_g[:, None, None] * expo[None]), 0.0)
    q_dec = np.exp(log_g[:, None] * (idx[None, :] + 1.0))
    k_dec = np.exp(log_g[:, None] * (RET_BLOCK - 1.0 - idx[None, :]))
    blk_dec = np.exp(log_g * RET_BLOCK)
    q_dec = np.broadcast_to(q_dec[:, :, None], (RET_HEADS, RET_BLOCK, RET_DK))
    k_dec = np.broadcast_to(k_dec[:, :, None], (RET_HEADS, RET_BLOCK, RET_DK))
    blk_dec = np.broadcast_to(blk_dec[:, None, None], (RET_HEADS, 1, RET_DV))
    half = RET_DK // 2
    inv = ROPE_BASE ** (-np.arange(half, dtype=np.float64) / half)
    ang = np.arange(SEQ, dtype=np.float64)[:, None] * inv[None, :]
    cos = np.concatenate([np.cos(ang), np.cos(ang)], axis=1)
    sin = np.concatenate([-np.sin(ang), np.sin(ang)], axis=1)
    f32 = lambda a: jnp.asarray(np.ascontiguousarray(a), jnp.float32)
    return f32(decay), f32(q_dec), f32(k_dec), f32(blk_dec), f32(cos), f32(sin)


def _retention_kernel(q_ref, k_ref, v_ref, gate_ref, cos_ref, sin_ref, dec_ref, qd_ref, kd_ref,
                      bd_ref, o_ref, state_ref):
    @pl.when(pl.program_id(2) == 0)
    def _():
        state_ref[...] = jnp.zeros_like(state_ref)

    cos, sin = cos_ref[...], sin_ref[...]
    q = q_ref[...].astype(jnp.float32)
    k = k_ref[...].astype(jnp.float32)
    qr = q * cos + pltpu.roll(q, RET_DK // 2, axis=1) * sin
    kr = (k * cos + pltpu.roll(k, RET_DK // 2, axis=1) * sin) * (RET_DK ** -0.5)
    v = v_ref[...]

    scores = _dot_nt(_bf16(qr), _bf16(kr)) * dec_ref[0]
    intra = _dot(_bf16(scores), v)
    state = state_ref[...]
    inter = _dot(_bf16(qr * qd_ref[0]), _bf16(state))
    state_ref[...] = state * bd_ref[0] + _dot_tn(_bf16(kr * kd_ref[0]), v)

    o = intra + inter
    mu = jnp.mean(o, axis=-1, keepdims=True)
    cen = o - mu
    var = jnp.mean(cen * cen, axis=-1, keepdims=True)
    normed = cen * lax.rsqrt(var + EPS)
    o_ref[...] = _bf16(_silu(gate_ref[...].astype(jnp.float32)) * normed)


def _retention(proj):
    decay, q_dec, k_dec, blk_dec, cos, sin = _retention_tables()
    nblk = SEQ // RET_BLOCK
    row = lambda b, h, n: b * nblk + n
    return pl.pallas_call(
        _retention_kernel,
        grid=(BATCH, RET_HEADS, nblk),
        in_specs=[
            pl.BlockSpec((RET_BLOCK, RET_DK), lambda b, h, n: (row(b, h, n), h)),
            pl.BlockSpec((RET_BLOCK, RET_DK), lambda b, h, n: (row(b, h, n), A_QK // RET_DK + h)),
            pl.BlockSpec((RET_BLOCK, RET_DV), lambda b, h, n: (row(b, h, n), 2 * A_QK // RET_DV + h)),
            pl.BlockSpec((RET_BLOCK, RET_DV),
                         lambda b, h, n: (row(b, h, n), (2 * A_QK + A_V) // RET_DV + h)),
            pl.BlockSpec((RET_BLOCK, RET_DK), lambda b, h, n: (n, 0)),
            pl.BlockSpec((RET_BLOCK, RET_DK), lambda b, h, n: (n, 0)),
            pl.BlockSpec((1, RET_BLOCK, RET_BLOCK), lambda b, h, n: (h, 0, 0)),
            pl.BlockSpec((1, RET_BLOCK, RET_DK), lambda b, h, n: (h, 0, 0)),
            pl.BlockSpec((1, RET_BLOCK, RET_DK), lambda b, h, n: (h, 0, 0)),
            pl.BlockSpec((1, 1, RET_DV), lambda b, h, n: (h, 0, 0)),
        ],
        out_specs=pl.BlockSpec((RET_BLOCK, RET_DV), lambda b, h, n: (row(b, h, n), h)),
        out_shape=jax.ShapeDtypeStruct((TOKENS, A_V), jnp.bfloat16),
        scratch_shapes=[pltpu.VMEM((RET_DK, RET_DV), jnp.float32)],
        compiler_params=pltpu.CompilerParams(
            dimension_semantics=("arbitrary", "arbitrary", "arbitrary")),
        name="retention",
    )(proj, proj, proj, proj, cos, sin, decay, q_dec, k_dec, blk_dec)


def _spatial_gate_kernel(u_ref, z_ref, lng_ref, lnb_ref, w_ref, b_ref, o_ref):
    u = _gelu_tanh(u_ref[...].astype(jnp.float32))
    z = _gelu_tanh(z_ref[...].astype(jnp.float32))
    mu = jnp.mean(z, axis=-1, keepdims=True)
    cen = z - mu
    var = jnp.mean(cen * cen, axis=-1, keepdims=True)
    zn = _bf16(cen * lax.rsqrt(var + EPS) * lng_ref[...] + lnb_ref[...])
    rows = lax.broadcasted_iota(jnp.int32, (SG_LEN, SG_LEN), 0)
    cols = lax.broadcasted_iota(jnp.int32, (SG_LEN, SG_LEN), 1)
    keep = (rows // CHUNK) >= (cols // CHUNK)
    for g in range(SG_GROUPS):
        cs = slice(g * SG_CH, (g + 1) * SG_CH)
        w = _bf16(jnp.where(keep, w_ref[g], 0.0))
        mixed = _dot(w, zn[:, cs]) + b_ref[g]
        o_ref[:, cs] = _bf16(u[:, cs] * mixed)


def _spatial_gate(proj, ln_g, ln_b, w_s, b_s):
    u_col = (2 * A_QK + 2 * A_V) // SG_WIDTH
    b_full = jnp.broadcast_to(b_s[:, :, None], (SG_GROUPS, SG_LEN, SG_CH))
    return pl.pallas_call(
        _spatial_gate_kernel,
        grid=(TOKENS // SG_LEN,),
        in_specs=[
            pl.BlockSpec((SG_LEN, SG_WIDTH), lambda i: (i, u_col)),
            pl.BlockSpec((SG_LEN, SG_WIDTH), lambda i: (i, u_col + 1)),
            pl.BlockSpec((1, SG_WIDTH), lambda i: (0, 0)),
            pl.BlockSpec((1, SG_WIDTH), lambda i: (0, 0)),
            pl.BlockSpec((SG_GROUPS, SG_LEN, SG_LEN), lambda i: (0, 0, 0)),
            pl.BlockSpec((SG_GROUPS, SG_LEN, SG_CH), lambda i: (0, 0, 0)),
        ],
        out_specs=pl.BlockSpec((SG_LEN, SG_WIDTH), lambda i: (i, 0)),
        out_shape=jax.ShapeDtypeStruct((TOKENS, SG_WIDTH), jnp.bfloat16),
        name="spatial_gate",
    )(proj, proj, ln_g.reshape(1, SG_WIDTH), ln_b.reshape(1, SG_WIDTH), w_s, b_full)


def _band_bias(rel_bias):
    n_rel = ATT_QBLOCK + ATT_BAND - 1
    rel_rev = (ATT_BAND - 1) - np.arange(n_rel)
    table = rel_bias[:, np.clip(rel_rev, -(CHUNK - 1), MAX_REL) + (CHUNK - 1)].astype(jnp.float32)
    rows = [table[:, ATT_QBLOCK - 1 - i:ATT_QBLOCK - 1 - i + ATT_BAND] for i in range(ATT_QBLOCK)]
    bias = jnp.stack(rows, axis=1)
    i = np.arange(ATT_QBLOCK)[:, None]
    jj = np.arange(ATT_BAND)[None, :]
    in_band = (jj // CHUNK >= i // CHUNK) & (jj // CHUNK <= i // CHUNK + BAND_PAD // CHUNK)
    return jnp.where(jnp.asarray(in_band)[None], bias * LOG2_E, NEG_BIG)


def _head_rms(x, gain):
    head_of_lane = lax.broadcasted_iota(jnp.int32, x.shape, 1) // ATT_DH
    sq = x * x
    ms = jnp.zeros_like(x)
    for h in range(ATT_GROUP):
        in_head = head_of_lane == h
        ms = jnp.where(in_head, jnp.sum(jnp.where(in_head, sq, 0.0), axis=-1, keepdims=True), ms)
    return x * lax.rsqrt(ms * (1.0 / ATT_DH) + EPS) * gain


def _band_attn_kernel(q_ref, k_ref, v_ref, bias_ref, qg_ref, kg_ref, o_ref, kpad_ref, vpad_ref,
                      s_ref, p_ref, den_ref):
    n = pl.program_id(2)
    pair = 2 * ATT_DH

    @pl.when(n == 0)
    def _():
        kn = _bf16(_head_rms(k_ref[...].astype(jnp.float32), kg_ref[...]))
        v = v_ref[...]
        low_half = lax.broadcasted_iota(jnp.int32, (SEQ, pair), 1) < ATT_DH
        zero = jnp.zeros((SEQ, pair), jnp.bfloat16)
        for h in range(ATT_GROUP):
            cols = slice((h // 2) * pair, (h // 2 + 1) * pair)
            own = low_half if h % 2 == 0 else jnp.logical_not(low_half)
            kpad_ref[h, 0:BAND_PAD, :] = jnp.zeros((BAND_PAD, pair), jnp.bfloat16)
            vpad_ref[h, 0:BAND_PAD, :] = jnp.zeros((BAND_PAD, pair), jnp.bfloat16)
            kpad_ref[h, BAND_PAD:, :] = jnp.where(own, kn[:, cols], zero)
            vpad_ref[h, BAND_PAD:, :] = jnp.where(own, v[:, cols], zero)

    q = _bf16(_head_rms(q_ref[...].astype(jnp.float32), qg_ref[...]) * (LOG2_E * ATT_DH ** -0.5))
    start = pl.multiple_of(n * ATT_QBLOCK, ATT_QBLOCK)
    for h in range(ATT_GROUP):
        q_pair = q[:, (h // 2) * pair:(h // 2 + 1) * pair]
        s_ref[h] = _dot_nt(q_pair, kpad_ref[h, pl.ds(start, ATT_BAND), :])

    key_pos = lax.broadcasted_iota(jnp.int32, (ATT_QBLOCK, ATT_BAND), 1) + start
    valid = key_pos >= BAND_PAD
    for h in range(ATT_GROUP):
        s = jnp.where(valid, s_ref[h] + bias_ref[h], NEG_BIG)
        m = jnp.max(s, axis=-1, keepdims=True)
        p = jnp.exp2(s - m)
        den_ref[h] = jnp.sum(p, axis=-1, keepdims=True)
        p_ref[h] = _bf16(p)
    outs = []
    for h0 in range(0, ATT_GROUP, 2):
        pv = [_dot(p_ref[h], vpad_ref[h, pl.ds(start, ATT_BAND), :]) * (1.0 / den_ref[h])
              for h in (h0, h0 + 1)]
        outs.append(pv[0] + pv[1])
    o_ref[...] = _bf16(jnp.concatenate(outs, axis=-1))


def _band_attention(qkv, q_g, k_g, rel_bias):
    bias = _band_bias(rel_bias)
    nq = SEQ // ATT_QBLOCK
    width = ATT_GROUP * ATT_DH
    n_groups = ATT_HEADS // ATT_GROUP
    qg = jnp.tile(q_g, ATT_GROUP).reshape(1, width)
    kg = jnp.tile(k_g, ATT_GROUP).reshape(1, width)
    return pl.pallas_call(
        _band_attn_kernel,
        grid=(BATCH, n_groups, nq),
        in_specs=[
            pl.BlockSpec((ATT_QBLOCK, width), lambda b, hg, n: (b * nq + n, hg)),
            pl.BlockSpec((SEQ, width), lambda b, hg, n: (b, n_groups + hg)),
            pl.BlockSpec((SEQ, width), lambda b, hg, n: (b, 2 * n_groups + hg)),
            pl.BlockSpec((ATT_GROUP, ATT_QBLOCK, ATT_BAND), lambda b, hg, n: (hg, 0, 0)),
            pl.BlockSpec((1, width), lambda b, hg, n: (0, 0)),
            pl.BlockSpec((1, width), lambda b, hg, n: (0, 0)),
        ],
        out_specs=pl.BlockSpec((ATT_QBLOCK, width), lambda b, hg, n: (b * nq + n, hg)),
        out_shape=jax.ShapeDtypeStruct((TOKENS, D_MODEL), jnp.bfloat16),
        scratch_shapes=[pltpu.VMEM((ATT_GROUP, SEQ + BAND_PAD, 2 * ATT_DH), jnp.bfloat16),
                        pltpu.VMEM((ATT_GROUP, SEQ + BAND_PAD, 2 * ATT_DH), jnp.bfloat16),
                        pltpu.VMEM((ATT_GROUP, ATT_QBLOCK, ATT_BAND), jnp.float32),
                        pltpu.VMEM((ATT_GROUP, ATT_QBLOCK, ATT_BAND), jnp.bfloat16),
                        pltpu.VMEM((ATT_GROUP, ATT_QBLOCK, 1), jnp.float32)],
        compiler_params=pltpu.CompilerParams(
            dimension_semantics=("arbitrary", "arbitrary", "arbitrary")),
        name="band_attention",
    )(qkv, qkv, qkv, bias, qg, kg)


def _store_token_major(ref, value, lead=()):
    rows = value.shape[0]
    for c in range(ROW_TILES):
        ref[(*lead, pl.ds(c, rows, stride=ROW_TILES), slice(None))] = value[:, c * 128:(c + 1) * 128]


def _load_token_major(ref, rows, c, lead=()):
    return ref[(*lead, pl.ds(c, rows, stride=ROW_TILES), slice(None))]


def _split_bf16(x):
    hi = _bf16(x)
    lo = _bf16(x - hi.astype(jnp.float32))
    return hi, lo


def _router_kernel(x_ref, g_ref, sc_ref, sh_ref, rw_ref, rb_ref,
                   h_ref, idx_ref, rank_ref, gate_ref, cnt_ref, carry_ref):
    @pl.when(pl.program_id(0) == 0)
    def _():
        carry_ref[...] = jnp.zeros_like(carry_ref)

    h = _norm_mod(x_ref[...], g_ref[...], sc_ref[0], sh_ref[0])
    _store_token_major(h_ref, h)
    h_hi, h_lo = _split_bf16(h)
    w_hi, w_lo = _split_bf16(rw_ref[...])
    logits = _dot_nt(w_hi, h_hi) + _dot_nt(w_hi, h_lo) + _dot_nt(w_lo, h_hi) + rb_ref[...]

    expert = lax.broadcasted_iota(jnp.int32, logits.shape, 0).astype(jnp.float32)
    work = logits
    vals, idxs, sels = [], [], []
    for _ in range(TOP_K):
        m = jnp.max(work, axis=0, keepdims=True)
        pick = jnp.min(jnp.where(work == m, expert, float(N_EXPERTS)), axis=0, keepdims=True)
        sel = expert == pick
        work = jnp.where(sel, -jnp.inf, work)
        vals.append(m)
        idxs.append(pick)
        sels.append(sel)
    exps = [jnp.exp(v - vals[0]) for v in vals]
    denom = exps[0] + exps[1] + exps[2] + exps[3]
    gate_ref[...] = jnp.concatenate([e / denom for e in exps], axis=0)
    idx_ref[...] = jnp.concatenate(idxs, axis=0).astype(jnp.int32)

    chosen = jnp.zeros(logits.shape, jnp.float32)
    for sel in sels:
        chosen = jnp.where(sel, 1.0, chosen)
    tile = logits.shape[1]
    earlier = (lax.broadcasted_iota(jnp.int32, (tile, tile), 0)
               < lax.broadcasted_iota(jnp.int32, (tile, tile), 1))
    before = _dot(_bf16(chosen), jnp.where(earlier, 1.0, 0.0).astype(jnp.bfloat16))
    rank_full = before + carry_ref[...]
    ranks = [jnp.sum(jnp.where(sel, rank_full, 0.0), axis=0, keepdims=True) for sel in sels]
    rank_ref[...] = jnp.concatenate(ranks, axis=0).astype(jnp.int32)
    carry = carry_ref[...] + jnp.sum(chosen, axis=1, keepdims=True)
    carry_ref[...] = carry
    cnt_ref[...] = jnp.broadcast_to(carry, cnt_ref.shape)


def _router(x, g, sc, sh, router_w, router_b):
    tiles_per_batch = SEQ // ROUTE_TILE
    lane_out = lambda dt: jax.ShapeDtypeStruct((TOP_K, TOKENS), dt)
    lane_spec = pl.BlockSpec((TOP_K, ROUTE_TILE), lambda i: (0, i))
    return pl.pallas_call(
        _router_kernel,
        grid=(TOKENS // ROUTE_TILE,),
        in_specs=[
            pl.BlockSpec((ROUTE_TILE, D_MODEL), lambda i: (i, 0)),
            pl.BlockSpec((1, D_MODEL), lambda i: (0, 0)),
            pl.BlockSpec((1, 1, D_MODEL), lambda i: (i // tiles_per_batch, 0, 0)),
            pl.BlockSpec((1, 1, D_MODEL), lambda i: (i // tiles_per_batch, 0, 0)),
            pl.BlockSpec((N_EXPERTS, D_MODEL), lambda i: (0, 0)),
            pl.BlockSpec((N_EXPERTS, 1), lambda i: (0, 0)),
        ],
        out_specs=[
            pl.BlockSpec((ROUTE_TILE * ROW_TILES, 128), lambda i: (i, 0)),
            lane_spec, lane_spec, lane_spec,
            pl.BlockSpec((N_EXPERTS, 128), lambda i: (0, 0)),
        ],
        out_shape=[
            jax.ShapeDtypeStruct((TOKENS * ROW_TILES, 128), jnp.float32),
            lane_out(jnp.int32), lane_out(jnp.int32), lane_out(jnp.float32),
            jax.ShapeDtypeStruct((N_EXPERTS, 128), jnp.float32),
        ],
        scratch_shapes=[pltpu.VMEM((N_EXPERTS, 1), jnp.float32)],
        compiler_params=pltpu.CompilerParams(dimension_semantics=("arbitrary",)),
        name="moe_router",
    )(x, g.reshape(1, D_MODEL), sc, sh, router_w.T, router_b.reshape(N_EXPERTS, 1))


def _row_copy(src_ref, src_row, dst_ref, dst_row, sem):
    src = pl.multiple_of(src_row * ROW_TILES, ROW_TILES)
    dst = pl.multiple_of(dst_row * ROW_TILES, ROW_TILES)
    return pltpu.make_async_copy(src_ref.at[pl.ds(src, ROW_TILES), :],
                                 dst_ref.at[pl.ds(dst, ROW_TILES), :], sem)


def _zero_block_copy(zero_ref, xs_ref, row, sem):
    row = pl.multiple_of(row * ROW_TILES, MOE_BLOCK * ROW_TILES)
    return pltpu.make_async_copy(zero_ref, xs_ref.at[pl.ds(row, MOE_BLOCK * ROW_TILES), :], sem)


def _dispatch_kernel(zstart_ref, nused_ref, pos_ref, h_ref, xs_ref, zero_ref, zsem, sem):
    step = pl.program_id(0)

    @pl.when(step == 0)
    def _():
        zero_ref[...] = jnp.zeros_like(zero_ref)

        def each_block(fn):
            def expert_last(e, carry):
                @pl.when(zstart_ref[e] >= 0)
                def _():
                    fn(_zero_block_copy(zero_ref, xs_ref, zstart_ref[e], zsem))
                return carry

            def unused(blk, carry):
                fn(_zero_block_copy(zero_ref, xs_ref, blk * MOE_BLOCK, zsem))
                return carry

            lax.fori_loop(0, N_EXPERTS, expert_last, 0)
            lax.fori_loop(nused_ref[0], MOE_NBLOCKS, unused, 0)

        each_block(lambda copy: copy.start())
        each_block(lambda copy: copy.wait())

    def issue(t, carry):
        for k in range(TOP_K):
            _row_copy(h_ref, t, xs_ref, pos_ref[t * TOP_K + k], sem).start(priority=k % 2)
        return carry

    lax.fori_loop(0, DISPATCH_TILE, issue, 0)

    for _ in range(TOP_K):
        pltpu.make_async_copy(h_ref, xs_ref.at[pl.ds(0, DISPATCH_TILE * ROW_TILES), :], sem).wait()


def _dispatch(h, pos_flat, zstart, n_used):
    grid_spec = pltpu.PrefetchScalarGridSpec(
        num_scalar_prefetch=2,
        grid=(TOKENS // DISPATCH_TILE,),
        in_specs=[
            pl.BlockSpec((DISPATCH_TILE * TOP_K,), lambda i, zs, nu: (i,),
                         memory_space=pltpu.SMEM),
            pl.BlockSpec((DISPATCH_TILE * ROW_TILES, 128), lambda i, zs, nu: (i, 0)),
        ],
        out_specs=pl.BlockSpec(memory_space=pl.ANY),
        scratch_shapes=[pltpu.VMEM((MOE_BLOCK * ROW_TILES, 128), jnp.float32),
                        pltpu.SemaphoreType.DMA, pltpu.SemaphoreType.DMA],
    )
    return pl.pallas_call(
        _dispatch_kernel,
        grid_spec=grid_spec,
        out_shape=jax.ShapeDtypeStruct((MOE_CAP * ROW_TILES, 128), jnp.float32),
        compiler_params=pltpu.CompilerParams(dimension_semantics=("arbitrary",)),
        name="moe_dispatch",
    )(zstart, n_used, pos_flat, h)


def _expert_kernel(be_ref, nused_ref, xs_ref, w1_ref, b1_ref, w2_ref, b2_ref, ys_ref,
                   w1b_ref, w2b_ref):
    i = pl.program_id(0)
    new_expert = jnp.logical_or(i == 0, be_ref[i] != be_ref[jnp.maximum(i - 1, 0)])

    @pl.when(new_expert)
    def _():
        for r in range(D_MODEL // 256):
            rows = slice(r * 256, (r + 1) * 256)
            w1b_ref[rows, :] = _bf16(w1_ref[0, rows, :])
            w2b_ref[rows, :] = _bf16(w2_ref[0, rows, :])

    @pl.when(i < nused_ref[0])
    def _():
        x = _bf16(jnp.concatenate([_load_token_major(xs_ref, MOE_BLOCK, c)
                                   for c in range(ROW_TILES)], axis=-1))
        glu = _dot(x, w1b_ref[:, :D_FF]) + b1_ref[0][:, :D_FF]
        lin = _dot(x, w1b_ref[:, D_FF:]) + b1_ref[0][:, D_FF:]
        glu = jnp.minimum(glu, SWIGLU_LIMIT)
        lin = jnp.clip(lin, -SWIGLU_LIMIT, SWIGLU_LIMIT)
        act = glu * (1.0 / (1.0 + jnp.exp(-SWIGLU_ALPHA * glu))) * (lin + 1.0)
        _store_token_major(ys_ref, _dot(_bf16(act), w2b_ref[...]) + b2_ref[0])

    @pl.when(i >= nused_ref[0])
    def _():
        ys_ref[...] = jnp.zeros_like(ys_ref)


def _experts(xs, block_e, n_used, layer, w1, b1, w2, b2):
    grid_spec = pltpu.PrefetchScalarGridSpec(
        num_scalar_prefetch=2,
        grid=(MOE_NBLOCKS,),
        in_specs=[
            pl.BlockSpec((MOE_BLOCK * ROW_TILES, 128),
                         lambda i, be, nu: (jnp.minimum(i, nu[0] - 1), 0)),
            pl.BlockSpec((None, 1, D_MODEL, 2 * D_FF), lambda i, be, nu: (layer, be[i], 0, 0)),
            pl.BlockSpec((None, 1, 1, 2 * D_FF), lambda i, be, nu: (layer, be[i], 0, 0)),
            pl.BlockSpec((None, 1, D_FF, D_MODEL), lambda i, be, nu: (layer, be[i], 0, 0)),
            pl.BlockSpec((None, 1, 1, D_MODEL), lambda i, be, nu: (layer, be[i], 0, 0)),
        ],
        out_specs=pl.BlockSpec((MOE_BLOCK * ROW_TILES, 128), lambda i, be, nu: (i, 0)),
        scratch_shapes=[pltpu.VMEM((D_MODEL, 2 * D_FF), jnp.bfloat16),
                        pltpu.VMEM((D_FF, D_MODEL), jnp.bfloat16)],
    )
    return pl.pallas_call(
        _expert_kernel,
        grid_spec=grid_spec,
        out_shape=jax.ShapeDtypeStruct((MOE_CAP * ROW_TILES, 128), jnp.float32),
        compiler_params=pltpu.CompilerParams(dimension_semantics=("arbitrary",),
                                             vmem_limit_bytes=VMEM_LIMIT_V7X),
        name="moe_experts",
    )(block_e, n_used, xs, w1, b1.reshape(DEPTH, N_EXPERTS, 1, 2 * D_FF), w2,
      b2.reshape(DEPTH, N_EXPERTS, 1, D_MODEL))


def _combine_kernel(pos_ref, pos_next_ref, x_ref, gates_ref, g2_ref, ys_ref, o_ref, buf_ref, sems):
    step = pl.program_id(0)
    n_steps = pl.num_programs(0)
    slot = step % 2

    def gather(p_ref, dst_slot):
        def issue(t, carry):
            for k in range(TOP_K):
                _row_copy(ys_ref, p_ref[t * TOP_K + k], buf_ref.at[dst_slot, k], t,
                          sems.at[dst_slot]).start(priority=k % 2)
            return carry

        lax.fori_loop(0, DISPATCH_TILE, issue, 0)

    @pl.when(step == 0)
    def _():
        gather(pos_ref, 0)

    @pl.when(step + 1 < n_steps)
    def _():
        gather(pos_next_ref, 1 - slot)

    for k in range(TOP_K):
        pltpu.make_async_copy(ys_ref.at[pl.ds(0, DISPATCH_TILE * ROW_TILES), :],
                              buf_ref.at[slot, k], sems.at[slot]).wait()

    gates = gates_ref[...]
    for c in range(ROW_TILES):
        cols = slice(c * 128, (c + 1) * 128)
        y = gates[:, 0:1] * _load_token_major(buf_ref, DISPATCH_TILE, c, (slot, 0))
        for k in range(1, TOP_K):
            y = y + gates[:, k:k + 1] * _load_token_major(buf_ref, DISPATCH_TILE, c, (slot, k))
        o_ref[:, cols] = x_ref[:, cols] + g2_ref[0][:, cols] * y


def _combine(x, pos_flat, gates_tok, g2, ys):
    tiles_per_batch = SEQ // DISPATCH_TILE
    n_steps = TOKENS // DISPATCH_TILE
    return pl.pallas_call(
        _combine_kernel,
        grid=(n_steps,),
        in_specs=[
            pl.BlockSpec((DISPATCH_TILE * TOP_K,), lambda i: (i,), memory_space=pltpu.SMEM),
            pl.BlockSpec((DISPATCH_TILE * TOP_K,), lambda i: (jnp.minimum(i + 1, n_steps - 1),),
                         memory_space=pltpu.SMEM),
            pl.BlockSpec((DISPATCH_TILE, D_MODEL), lambda i: (i, 0)),
            pl.BlockSpec((DISPATCH_TILE, TOP_K), lambda i: (i, 0)),
            pl.BlockSpec((1, 1, D_MODEL), lambda i: (i // tiles_per_batch, 0, 0)),
            pl.BlockSpec(memory_space=pl.ANY),
        ],
        out_specs=pl.BlockSpec((DISPATCH_TILE, D_MODEL), lambda i: (i, 0)),
        out_shape=jax.ShapeDtypeStruct((TOKENS, D_MODEL), jnp.float32),
        scratch_shapes=[pltpu.VMEM((2, TOP_K, DISPATCH_TILE * ROW_TILES, 128), jnp.float32),
                        pltpu.SemaphoreType.DMA((2,))],
        compiler_params=pltpu.CompilerParams(dimension_semantics=("arbitrary",),
                                             vmem_limit_bytes=VMEM_LIMIT_V7X),
        name="moe_combine",
    )(pos_flat, pos_flat, x, gates_tok, g2, ys)


def _moe_layer(x, g, sc, sh, g2, router_w, router_b, layer, w1, b1, w2, b2):
    h, idx, rank, gates, cnt = _router(x, g, sc, sh, router_w, router_b)
    counts = cnt[:, 0].astype(jnp.int32)
    padded = ((counts + MOE_BLOCK - 1) // MOE_BLOCK) * MOE_BLOCK
    pends = jnp.cumsum(padded)
    pstarts = pends - padded
    experts = jnp.arange(N_EXPERTS, dtype=jnp.int32)[:, None, None]
    pos = rank + jnp.sum(jnp.where(idx[None] == experts, pstarts[:, None, None], 0), axis=0)
    pos_flat = pos.T.reshape(-1)
    block_start = jnp.arange(MOE_NBLOCKS, dtype=jnp.int32) * MOE_BLOCK
    block_e = jnp.minimum(jnp.sum(block_start[:, None] >= pends[None, :], axis=1),
                          N_EXPERTS - 1).astype(jnp.int32)
    n_used = (pends[-1:] // MOE_BLOCK).astype(jnp.int32)
    zstart = jnp.where(padded > 0, pends - MOE_BLOCK, -1).astype(jnp.int32)
    xs = _dispatch(h, pos_flat, zstart, n_used)
    ys = _experts(xs, block_e, n_used, layer, w1, b1, w2, b2)
    return _combine(x, pos_flat, gates.T, g2, ys)


def kernel(x, c, ada_w, ada_b, norm_mix_g, norm_ffn_g, ev_w_in, ev_w_out, sg_ln_g, sg_ln_b, sg_w, sg_b, od_w_in, od_w_out, od_q_g, od_k_g, od_rel_bias, moe_router_w, moe_router_b, moe_w1, moe_b1, moe_w2, moe_b2):
    mod = _ada_mod(c, ada_w, ada_b)
    xt = x.reshape(TOKENS, D_MODEL)
    for l in range(DEPTH):
        sh1, sc1, g1, sh2, sc2, g2 = [m.reshape(BATCH, 1, D_MODEL)
                                      for m in jnp.split(mod[l], 6, axis=-1)]
        i = l // 2
        if l % 2 == 0:
            proj = _norm_proj(xt, norm_mix_g[l], sc1, sh1, _bf16(ev_w_in[i]))
            a_out = _retention(proj)
            b_out = _spatial_gate(proj, sg_ln_g[i], sg_ln_b[i], sg_w[i], sg_b[i])
            xt = _out_proj([a_out, b_out], _bf16(ev_w_out[i]), xt, g1)
        else:
            qkv = _norm_proj(xt, norm_mix_g[l], sc1, sh1, _bf16(od_w_in[i]))
            att = _band_attention(qkv, od_q_g[i], od_k_g[i], od_rel_bias[i])
            xt = _out_proj([att], _bf16(od_w_out[i]), xt, g1)
        xt = _moe_layer(xt, norm_ffn_g[l], sc2, sh2, g2, moe_router_w[l], moe_router_b[l],
                        l, moe_w1, moe_b1, moe_w2, moe_b2)
    return xt.reshape(BATCH, SEQ, D_MODEL)
```

```python
import functools
import math

import numpy as np
import jax
import jax.numpy as jnp
from jax import lax
from jax.experimental import pallas as pl
from jax.experimental.pallas import tpu as pltpu

D_MODEL = 1024
BATCH = 8
SEQ = 2048
DEPTH = 4
TOKENS = BATCH * SEQ
CHUNK = 64
EPS = 1e-6

RET_HEADS = 4
RET_DK = 128
RET_DV = 256
ROPE_BASE = 10000.0
A_QK = RET_HEADS * RET_DK
A_V = RET_HEADS * RET_DV
SG_GROUPS = 4
SG_CH = 128
SG_WIDTH = SG_GROUPS * SG_CH
SG_LEN = 128
EVEN_IN = 2 * A_QK + 2 * A_V + 2 * SG_WIDTH
EVEN_MIX = A_V + SG_WIDTH

ATT_HEADS = 16
ATT_DH = 64
BAND_PAD = 8 * CHUNK
MAX_REL = 256
REL_SIZE = (CHUNK - 1) + MAX_REL + 1

N_EXPERTS = 32
TOP_K = 4
D_FF = D_MODEL
SWIGLU_ALPHA = 1.702
SWIGLU_LIMIT = 7.0

ROW_TILE = 512
COL_CHUNK = 512
RET_BLOCK = 256
ATT_QBLOCK = 128
ATT_BAND = ATT_QBLOCK + BAND_PAD
ATT_GROUP = 4
SG_TILE = 512
ROUTE_TILE = 512
MOE_BLOCK = 256
MOE_CAP = TOKENS * TOP_K + N_EXPERTS * MOE_BLOCK
MOE_NBLOCKS = MOE_CAP // MOE_BLOCK
DISPATCH_TILE = 256
ISSUE_UNROLL = 4
VMEM_LIMIT_V7X = 56 * 1024 * 1024
ROW_TILES = D_MODEL // 128

NEG_BIG = -1e30
LOG2_E = math.log2(math.e)


def _silu(x):
    return x * (1.0 / (1.0 + jnp.exp(-x)))


def _gelu_tanh(x):
    return 0.5 * x * (1.0 + jnp.tanh(math.sqrt(2.0 / math.pi) * (x + 0.044715 * (x * x * x))))


def _bf16(x):
    return x.astype(jnp.bfloat16)


def _dot(a, b):
    return jnp.dot(a, b, preferred_element_type=jnp.float32)


def _dot_nt(a, b):
    return lax.dot_general(a, b, (((1,), (1,)), ((), ())), preferred_element_type=jnp.float32)


def _dot_tn(a, b):
    return lax.dot_general(a, b, (((0,), (0,)), ((), ())), preferred_element_type=jnp.float32)


def _norm_mod(x, g, sc, sh):
    y = x * lax.rsqrt(jnp.mean(x * x, axis=-1, keepdims=True) + EPS)
    return (y * g) * (1.0 + sc) + sh


def _ada_kernel(c_ref, w_ref, b_ref, o_ref):
    c_act = _silu(c_ref[...])
    o_ref[0] = _dot(c_act, w_ref[0]) + b_ref[0]


def _ada_mod(c, ada_w, ada_b):
    n_col = 6
    return pl.pallas_call(
        _ada_kernel,
        grid=(DEPTH, n_col),
        in_specs=[
            pl.BlockSpec((BATCH, D_MODEL), lambda l, j: (0, 0)),
            pl.BlockSpec((1, D_MODEL, D_MODEL), lambda l, j: (l, 0, j)),
            pl.BlockSpec((1, 1, D_MODEL), lambda l, j: (l, 0, j)),
        ],
        out_specs=pl.BlockSpec((1, BATCH, D_MODEL), lambda l, j: (l, 0, j)),
        out_shape=jax.ShapeDtypeStruct((DEPTH, BATCH, 6 * D_MODEL), jnp.float32),
        name="ada_mod",
    )(c, ada_w, ada_b.reshape(DEPTH, 1, 6 * D_MODEL))


def _norm_proj_kernel(x_ref, g_ref, sc_ref, sh_ref, w_ref, o_ref):
    h = _bf16(_norm_mod(x_ref[...], g_ref[...], sc_ref[0], sh_ref[0]))
    n_out = o_ref.shape[1]
    for j in range(n_out // COL_CHUNK):
        cols = slice(j * COL_CHUNK, (j + 1) * COL_CHUNK)
        o_ref[:, cols] = _bf16(_dot(h, w_ref[:, cols]))


def _norm_proj(x, g, sc, sh, w_bf16):
    n_out = w_bf16.shape[1]
    tiles_per_batch = SEQ // ROW_TILE
    return pl.pallas_call(
        _norm_proj_kernel,
        grid=(TOKENS // ROW_TILE,),
        in_specs=[
            pl.BlockSpec((ROW_TILE, D_MODEL), lambda i: (i, 0)),
            pl.BlockSpec((1, D_MODEL), lambda i: (0, 0)),
            pl.BlockSpec((1, 1, D_MODEL), lambda i: (i // tiles_per_batch, 0, 0)),
            pl.BlockSpec((1, 1, D_MODEL), lambda i: (i // tiles_per_batch, 0, 0)),
            pl.BlockSpec((D_MODEL, n_out), lambda i: (0, 0)),
        ],
        out_specs=pl.BlockSpec((ROW_TILE, n_out), lambda i: (i, 0)),
        out_shape=jax.ShapeDtypeStruct((TOKENS, n_out), jnp.bfloat16),
        compiler_params=pltpu.CompilerParams(vmem_limit_bytes=VMEM_LIMIT_V7X),
        name="norm_proj",
    )(x, g.reshape(1, D_MODEL), sc, sh, w_bf16)


def _out_proj_kernel(*refs, widths):
    part_refs = refs[:len(widths)]
    w_ref, x_ref, g_ref, o_ref = refs[len(widths):]
    for j in range(D_MODEL // COL_CHUNK):
        cols = slice(j * COL_CHUNK, (j + 1) * COL_CHUNK)
        acc = None
        row0 = 0
        for p_ref, width in zip(part_refs, widths):
            term = _dot(p_ref[...], w_ref[row0:row0 + width, cols])
            acc = term if acc is None else acc + term
            row0 += width
        o_ref[:, cols] = x_ref[:, cols] + g_ref[0][:, cols] * acc


def _out_proj(parts, w_bf16, x, gate):
    widths = tuple(p.shape[1] for p in parts)
    tiles_per_batch = SEQ // ROW_TILE
    in_specs = [pl.BlockSpec((ROW_TILE, width), lambda i: (i, 0)) for width in widths]
    in_specs += [
        pl.BlockSpec((sum(widths), D_MODEL), lambda i: (0, 0)),
        pl.BlockSpec((ROW_TILE, D_MODEL), lambda i: (i, 0)),
        pl.BlockSpec((1, 1, D_MODEL), lambda i: (i // tiles_per_batch, 0, 0)),
    ]
    return pl.pallas_call(
        functools.partial(_out_proj_kernel, widths=widths),
        grid=(TOKENS // ROW_TILE,),
        in_specs=in_specs,
        out_specs=pl.BlockSpec((ROW_TILE, D_MODEL), lambda i: (i, 0)),
        out_shape=jax.ShapeDtypeStruct((TOKENS, D_MODEL), jnp.float32),
        compiler_params=pltpu.CompilerParams(vmem_limit_bytes=VMEM_LIMIT_V7X),
        name="out_proj",
    )(*parts, w_bf16, x, gate)


def _retention_tables():
    heads = np.arange(RET_HEADS, dtype=np.float64)
    log_g = np.log1p(-np.exp2(-5.0 - heads))
    idx = np.arange(RET_BLOCK, dtype=np.float64)
    diff = idx[:, None] - idx[None, :]
    ci, cj = (idx // CHUNK)[:, None], (idx // CHUNK)[None, :]
    expo = np.where(ci == cj, np.abs(diff), diff)
    decay = np.where(cj <= ci, np.exp(log_g[:, None, None] * expo[None]), 0.0)
    q_dec = np.exp(log_g[:, None] * (idx[None, :] + 1.0))
    k_dec = np.exp(log_g[:, None] * (RET_BLOCK - 1.0 - idx[None, :]))
    blk_dec = np.exp(log_g * RET_BLOCK)
    q_dec = np.broadcast_to(q_dec[:, :, None], (RET_HEADS, RET_BLOCK, RET_DK))
    k_dec = np.broadcast_to(k_dec[:, :, None], (RET_HEADS, RET_BLOCK, RET_DK))
    blk_dec = np.broadcast_to(blk_dec[:, None, None], (RET_HEADS, 1, RET_DV))
    half = RET_DK // 2
    inv = ROPE_BASE ** (-np.arange(half, dtype=np.float64) / half)
    ang = np.arange(SEQ, dtype=np.float64)[:, None] * inv[None, :]
    cos = np.concatenate([np.cos(ang), np.cos(ang)], axis=1)
    sin = np.concatenate([-np.sin(ang), np.sin(ang)], axis=1)
    f32 = lambda a: jnp.asarray(np.ascontiguousarray(a), jnp.float32)
    return f32(decay), f32(q_dec), f32(k_dec), f32(blk_dec), f32(cos), f32(sin)


def _retention_kernel(q_ref, k_ref, v_ref, gate_ref, cos_ref, sin_ref, dec_ref, qd_ref, kd_ref,
                      bd_ref, o_ref, state_ref):
    @pl.when(pl.program_id(1) == 0)
    def _():
        state_ref[...] = jnp.zeros_like(state_ref)

    cos, sin = cos_ref[...], sin_ref[...]
    for h in range(RET_HEADS):
        ks = slice(h * RET_DK, (h + 1) * RET_DK)
        vs = slice(h * RET_DV, (h + 1) * RET_DV)
        q = q_ref[:, ks].astype(jnp.float32)
        k = k_ref[:, ks].astype(jnp.float32)
        qr = q * cos + pltpu.roll(q, RET_DK // 2, axis=1) * sin
        kr = (k * cos + pltpu.roll(k, RET_DK // 2, axis=1) * sin) * (RET_DK ** -0.5)
        v = v_ref[:, vs]

        scores = _dot_nt(_bf16(qr), _bf16(kr)) * dec_ref[h]
        intra = _dot(_bf16(scores), v)
        state = state_ref[h]
        inter = _dot(_bf16(qr * qd_ref[h]), _bf16(state))
        state_ref[h] = state * bd_ref[h] + _dot_tn(_bf16(kr * kd_ref[h]), v)

        o = intra + inter
        mu = jnp.mean(o, axis=-1, keepdims=True)
        cen = o - mu
        var = jnp.mean(cen * cen, axis=-1, keepdims=True)
        normed = cen * lax.rsqrt(var + EPS)
        o_ref[:, vs] = _bf16(_silu(gate_ref[:, vs].astype(jnp.float32)) * normed)


def _retention(proj):
    decay, q_dec, k_dec, blk_dec, cos, sin = _retention_tables()
    nblk = SEQ // RET_BLOCK
    row = lambda b, n: b * nblk + n
    whole = lambda b, n: (0, 0, 0)
    return pl.pallas_call(
        _retention_kernel,
        grid=(BATCH, nblk),
        in_specs=[
            pl.BlockSpec((RET_BLOCK, A_QK), lambda b, n: (row(b, n), 0)),
            pl.BlockSpec((RET_BLOCK, A_QK), lambda b, n: (row(b, n), 1)),
            pl.BlockSpec((RET_BLOCK, A_V), lambda b, n: (row(b, n), 2 * A_QK // A_V)),
            pl.BlockSpec((RET_BLOCK, A_V), lambda b, n: (row(b, n), 2 * A_QK // A_V + 1)),
            pl.BlockSpec((RET_BLOCK, RET_DK), lambda b, n: (n, 0)),
            pl.BlockSpec((RET_BLOCK, RET_DK), lambda b, n: (n, 0)),
            pl.BlockSpec((RET_HEADS, RET_BLOCK, RET_BLOCK), whole),
            pl.BlockSpec((RET_HEADS, RET_BLOCK, RET_DK), whole),
            pl.BlockSpec((RET_HEADS, RET_BLOCK, RET_DK), whole),
            pl.BlockSpec((RET_HEADS, 1, RET_DV), whole),
        ],
        out_specs=pl.BlockSpec((RET_BLOCK, A_V), lambda b, n: (row(b, n), 0)),
        out_shape=jax.ShapeDtypeStruct((TOKENS, A_V), jnp.bfloat16),
        scratch_shapes=[pltpu.VMEM((RET_HEADS, RET_DK, RET_DV), jnp.float32)],
        compiler_params=pltpu.CompilerParams(dimension_semantics=("arbitrary", "arbitrary")),
        name="retention",
    )(proj, proj, proj, proj, cos, sin, decay, q_dec, k_dec, blk_dec)


def _spatial_gate_kernel(u_ref, z_ref, lng_ref, lnb_ref, w_ref, b_ref, o_ref):
    u = _gelu_tanh(u_ref[...].astype(jnp.float32))
    z = _gelu_tanh(z_ref[...].astype(jnp.float32))
    mu = jnp.mean(z, axis=-1, keepdims=True)
    cen = z - mu
    var = jnp.mean(cen * cen, axis=-1, keepdims=True)
    zn = _bf16(cen * lax.rsqrt(var + EPS) * lng_ref[...] + lnb_ref[...])
    rows = lax.broadcasted_iota(jnp.int32, (SG_LEN, SG_LEN), 0)
    cols = lax.broadcasted_iota(jnp.int32, (SG_LEN, SG_LEN), 1)
    keep = (rows // CHUNK) >= (cols // CHUNK)
    for g in range(SG_GROUPS):
        cs = slice(g * SG_CH, (g + 1) * SG_CH)
        w = _bf16(jnp.where(keep, w_ref[g], 0.0))
        for blk in range(SG_TILE // SG_LEN):
            rs = slice(blk * SG_LEN, (blk + 1) * SG_LEN)
            mixed = _dot(w, zn[rs, cs]) + b_ref[g]
            o_ref[rs, cs] = _bf16(u[rs, cs] * mixed)


def _spatial_gate(proj, ln_g, ln_b, w_s, b_s):
    u_col = (2 * A_QK + 2 * A_V) // SG_WIDTH
    b_full = jnp.broadcast_to(b_s[:, :, None], (SG_GROUPS, SG_LEN, SG_CH))
    return pl.pallas_call(
        _spatial_gate_kernel,
        grid=(TOKENS // SG_TILE,),
        in_specs=[
            pl.BlockSpec((SG_TILE, SG_WIDTH), lambda i: (i, u_col)),
            pl.BlockSpec((SG_TILE, SG_WIDTH), lambda i: (i, u_col + 1)),
            pl.BlockSpec((1, SG_WIDTH), lambda i: (0, 0)),
            pl.BlockSpec((1, SG_WIDTH), lambda i: (0, 0)),
            pl.BlockSpec((SG_GROUPS, SG_LEN, SG_LEN), lambda i: (0, 0, 0)),
            pl.BlockSpec((SG_GROUPS, SG_LEN, SG_CH), lambda i: (0, 0, 0)),
        ],
        out_specs=pl.BlockSpec((SG_TILE, SG_WIDTH), lambda i: (i, 0)),
        out_shape=jax.ShapeDtypeStruct((TOKENS, SG_WIDTH), jnp.bfloat16),
        name="spatial_gate",
    )(proj, proj, ln_g.reshape(1, SG_WIDTH), ln_b.reshape(1, SG_WIDTH), w_s, b_full)


def _band_bias(rel_bias):
    n_rel = ATT_QBLOCK + ATT_BAND - 1
    rel_rev = (ATT_BAND - 1) - np.arange(n_rel)
    table = rel_bias[:, np.clip(rel_rev, -(CHUNK - 1), MAX_REL) + (CHUNK - 1)].astype(jnp.float32)
    rows = [table[:, ATT_QBLOCK - 1 - i:ATT_QBLOCK - 1 - i + ATT_BAND] for i in range(ATT_QBLOCK)]
    bias = jnp.stack(rows, axis=1)
    i = np.arange(ATT_QBLOCK)[:, None]
    jj = np.arange(ATT_BAND)[None, :]
    in_band = (jj // CHUNK >= i // CHUNK) & (jj // CHUNK <= i // CHUNK + BAND_PAD // CHUNK)
    return jnp.where(jnp.asarray(in_band)[None], bias * LOG2_E, NEG_BIG)


def _head_rms(x, gain):
    head_of_lane = lax.broadcasted_iota(jnp.int32, x.shape, 1) // ATT_DH
    sq = x * x
    ms = jnp.zeros_like(x)
    for h in range(ATT_GROUP):
        in_head = head_of_lane == h
        ms = jnp.where(in_head, jnp.sum(jnp.where(in_head, sq, 0.0), axis=-1, keepdims=True), ms)
    return x * lax.rsqrt(ms * (1.0 / ATT_DH) + EPS) * gain


def _band_attn_kernel(q_ref, k_ref, v_ref, bias_ref, qg_ref, kg_ref, o_ref, kpad_ref, vpad_ref,
                      s_ref, p_ref, den_ref):
    n = pl.program_id(2)
    pair = 2 * ATT_DH

    @pl.when(n == 0)
    def _():
        kn = _bf16(_head_rms(k_ref[...].astype(jnp.float32), kg_ref[...]))
        v = v_ref[...]
        low_half = lax.broadcasted_iota(jnp.int32, (SEQ, pair), 1) < ATT_DH
        zero = jnp.zeros((SEQ, pair), jnp.bfloat16)
        for h in range(ATT_GROUP):
            cols = slice((h // 2) * pair, (h // 2 + 1) * pair)
            own = low_half if h % 2 == 0 else jnp.logical_not(low_half)
            kpad_ref[h, 0:BAND_PAD, :] = jnp.zeros((BAND_PAD, pair), jnp.bfloat16)
            vpad_ref[h, 0:BAND_PAD, :] = jnp.zeros((BAND_PAD, pair), jnp.bfloat16)
            kpad_ref[h, BAND_PAD:, :] = jnp.where(own, kn[:, cols], zero)
            vpad_ref[h, BAND_PAD:, :] = jnp.where(own, v[:, cols], zero)

    q = _bf16(_head_rms(q_ref[...].astype(jnp.float32), qg_ref[...]) * (LOG2_E * ATT_DH ** -0.5))
    start = pl.multiple_of(n * ATT_QBLOCK, ATT_QBLOCK)
    for h in range(ATT_GROUP):
        q_pair = q[:, (h // 2) * pair:(h // 2 + 1) * pair]
        s_ref[h] = _dot_nt(q_pair, kpad_ref[h, pl.ds(start, ATT_BAND), :])

    key_pos = lax.broadcasted_iota(jnp.int32, (ATT_QBLOCK, ATT_BAND), 1) + start
    valid = key_pos >= BAND_PAD
    for h in range(ATT_GROUP):
        s = jnp.where(valid, s_ref[h] + bias_ref[h], NEG_BIG)
        m = jnp.max(s, axis=-1, keepdims=True)
        p = jnp.exp2(s - m)
        den_ref[h] = jnp.sum(p, axis=-1, keepdims=True)
        p_ref[h] = _bf16(p)
    outs = []
    for h0 in range(0, ATT_GROUP, 2):
        pv = [_dot(p_ref[h], vpad_ref[h, pl.ds(start, ATT_BAND), :]) * (1.0 / den_ref[h])
              for h in (h0, h0 + 1)]
        outs.append(pv[0] + pv[1])
    o_ref[...] = _bf16(jnp.concatenate(outs, axis=-1))


def _band_attention(qkv, q_g, k_g, rel_bias):
    bias = _band_bias(rel_bias)
    nq = SEQ // ATT_QBLOCK
    width = ATT_GROUP * ATT_DH
    n_groups = ATT_HEADS // ATT_GROUP
    qg = jnp.tile(q_g, ATT_GROUP).reshape(1, width)
    kg = jnp.tile(k_g, ATT_GROUP).reshape(1, width)
    return pl.pallas_call(
        _band_attn_kernel,
        grid=(BATCH, n_groups, nq),
        in_specs=[
            pl.BlockSpec((ATT_QBLOCK, width), lambda b, hg, n: (b * nq + n, hg)),
            pl.BlockSpec((SEQ, width), lambda b, hg, n: (b, n_groups + hg)),
            pl.BlockSpec((SEQ, width), lambda b, hg, n: (b, 2 * n_groups + hg)),
            pl.BlockSpec((ATT_GROUP, ATT_QBLOCK, ATT_BAND), lambda b, hg, n: (hg, 0, 0)),
            pl.BlockSpec((1, width), lambda b, hg, n: (0, 0)),
            pl.BlockSpec((1, width), lambda b, hg, n: (0, 0)),
        ],
        out_specs=pl.BlockSpec((ATT_QBLOCK, width), lambda b, hg, n: (b * nq + n, hg)),
        out_shape=jax.ShapeDtypeStruct((TOKENS, D_MODEL), jnp.bfloat16),
        scratch_shapes=[pltpu.VMEM((ATT_GROUP, SEQ + BAND_PAD, 2 * ATT_DH), jnp.bfloat16),
                        pltpu.VMEM((ATT_GROUP, SEQ + BAND_PAD, 2 * ATT_DH), jnp.bfloat16),
                        pltpu.VMEM((ATT_GROUP, ATT_QBLOCK, ATT_BAND), jnp.float32),
                        pltpu.VMEM((ATT_GROUP, ATT_QBLOCK, ATT_BAND), jnp.bfloat16),
                        pltpu.VMEM((ATT_GROUP, ATT_QBLOCK, 1), jnp.float32)],
        compiler_params=pltpu.CompilerParams(
            dimension_semantics=("arbitrary", "arbitrary", "arbitrary")),
        name="band_attention",
    )(qkv, qkv, qkv, bias, qg, kg)


def _store_token_major(ref, value, lead=()):
    rows = value.shape[0]
    for c in range(ROW_TILES):
        ref[(*lead, pl.ds(c, rows, stride=ROW_TILES), slice(None))] = value[:, c * 128:(c + 1) * 128]


def _load_token_major(ref, rows, c, lead=()):
    return ref[(*lead, pl.ds(c, rows, stride=ROW_TILES), slice(None))]


def _split_bf16(x):
    hi = _bf16(x)
    lo = _bf16(x - hi.astype(jnp.float32))
    return hi, lo


def _router_kernel(x_ref, g_ref, sc_ref, sh_ref, rw_ref, rb_ref,
                   h_ref, idx_ref, rank_ref, gate_ref, cnt_ref, carry_ref):
    @pl.when(pl.program_id(0) == 0)
    def _():
        carry_ref[...] = jnp.zeros_like(carry_ref)

    h = _norm_mod(x_ref[...], g_ref[...], sc_ref[0], sh_ref[0])
    _store_token_major(h_ref, h)
    h_hi, h_lo = _split_bf16(h)
    w_hi, w_lo = _split_bf16(rw_ref[...])
    logits = _dot_nt(w_hi, h_hi) + _dot_nt(w_hi, h_lo) + _dot_nt(w_lo, h_hi) + rb_ref[...]

    expert = lax.broadcasted_iota(jnp.int32, logits.shape, 0).astype(jnp.float32)
    work = logits
    vals, idxs, sels = [], [], []
    for _ in range(TOP_K):
        m = jnp.max(work, axis=0, keepdims=True)
        pick = jnp.min(jnp.where(work == m, expert, float(N_EXPERTS)), axis=0, keepdims=True)
        sel = expert == pick
        work = jnp.where(sel, -jnp.inf, work)
        vals.append(m)
        idxs.append(pick)
        sels.append(sel)
    exps = [jnp.exp(v - vals[0]) for v in vals]
    denom = exps[0] + exps[1] + exps[2] + exps[3]
    gate_ref[...] = jnp.concatenate([e / denom for e in exps], axis=0)
    idx_ref[...] = jnp.concatenate(idxs, axis=0).astype(jnp.int32)

    chosen = jnp.zeros(logits.shape, jnp.float32)
    for sel in sels:
        chosen = jnp.where(sel, 1.0, chosen)
    tile = logits.shape[1]
    earlier = (lax.broadcasted_iota(jnp.int32, (tile, tile), 0)
               < lax.broadcasted_iota(jnp.int32, (tile, tile), 1))
    before = _dot(_bf16(chosen), jnp.where(earlier, 1.0, 0.0).astype(jnp.bfloat16))
    rank_full = before + carry_ref[...]
    ranks = [jnp.sum(jnp.where(sel, rank_full, 0.0), axis=0, keepdims=True) for sel in sels]
    rank_ref[...] = jnp.concatenate(ranks, axis=0).astype(jnp.int32)
    carry = carry_ref[...] + jnp.sum(chosen, axis=1, keepdims=True)
    carry_ref[...] = carry
    cnt_ref[...] = jnp.broadcast_to(carry, cnt_ref.shape)


def _router(x, g, sc, sh, router_w, router_b):
    tiles_per_batch = SEQ // ROUTE_TILE
    lane_out = lambda dt: jax.ShapeDtypeStruct((TOP_K, TOKENS), dt)
    lane_spec = pl.BlockSpec((TOP_K, ROUTE_TILE), lambda i: (0, i))
    return pl.pallas_call(
        _router_kernel,
        grid=(TOKENS // ROUTE_TILE,),
        in_specs=[
            pl.BlockSpec((ROUTE_TILE, D_MODEL), lambda i: (i, 0)),
            pl.BlockSpec((1, D_MODEL), lambda i: (0, 0)),
            pl.BlockSpec((1, 1, D_MODEL), lambda i: (i // tiles_per_batch, 0, 0)),
            pl.BlockSpec((1, 1, D_MODEL), lambda i: (i // tiles_per_batch, 0, 0)),
            pl.BlockSpec((N_EXPERTS, D_MODEL), lambda i: (0, 0)),
            pl.BlockSpec((N_EXPERTS, 1), lambda i: (0, 0)),
        ],
        out_specs=[
            pl.BlockSpec((ROUTE_TILE * ROW_TILES, 128), lambda i: (i, 0)),
            lane_spec, lane_spec, lane_spec,
            pl.BlockSpec((N_EXPERTS, 128), lambda i: (0, 0)),
        ],
        out_shape=[
            jax.ShapeDtypeStruct((TOKENS * ROW_TILES, 128), jnp.float32),
            lane_out(jnp.int32), lane_out(jnp.int32), lane_out(jnp.float32),
            jax.ShapeDtypeStruct((N_EXPERTS, 128), jnp.float32),
        ],
        scratch_shapes=[pltpu.VMEM((N_EXPERTS, 1), jnp.float32)],
        compiler_params=pltpu.CompilerParams(dimension_semantics=("arbitrary",)),
        name="moe_router",
    )(x, g.reshape(1, D_MODEL), sc, sh, router_w.T, router_b.reshape(N_EXPERTS, 1))


def _row_copy(src_ref, src_row, dst_ref, dst_row, sem):
    src = pl.multiple_of(src_row * ROW_TILES, ROW_TILES)
    dst = pl.multiple_of(dst_row * ROW_TILES, ROW_TILES)
    return pltpu.make_async_copy(src_ref.at[pl.ds(src, ROW_TILES), :],
                                 dst_ref.at[pl.ds(dst, ROW_TILES), :], sem)


def _zero_block_copy(zero_ref, xs_ref, row, sem):
    row = pl.multiple_of(row * ROW_TILES, MOE_BLOCK * ROW_TILES)
    return pltpu.make_async_copy(zero_ref, xs_ref.at[pl.ds(row, MOE_BLOCK * ROW_TILES), :], sem)


def _dispatch_kernel(zstart_ref, nused_ref, pos_ref, h_ref, xs_ref, zero_ref, zsem, sem):
    step = pl.program_id(0)

    @pl.when(step == 0)
    def _():
        zero_ref[...] = jnp.zeros_like(zero_ref)

        def each_block(fn):
            def expert_last(e, carry):
                @pl.when(zstart_ref[e] >= 0)
                def _():
                    fn(_zero_block_copy(zero_ref, xs_ref, zstart_ref[e], zsem))
                return carry

            def unused(blk, carry):
                fn(_zero_block_copy(zero_ref, xs_ref, blk * MOE_BLOCK, zsem))
                return carry

            lax.fori_loop(0, N_EXPERTS, expert_last, 0)
            lax.fori_loop(nused_ref[0], MOE_NBLOCKS, unused, 0)

        each_block(lambda copy: copy.start())
        each_block(lambda copy: copy.wait())

    def issue(i, carry):
        for u in range(ISSUE_UNROLL):
            t = i * ISSUE_UNROLL + u
            for k in range(TOP_K):
                _row_copy(h_ref, t, xs_ref, pos_ref[t * TOP_K + k], sem).start(priority=k % 2)
        return carry

    lax.fori_loop(0, DISPATCH_TILE // ISSUE_UNROLL, issue, 0)

    for _ in range(TOP_K):
        pltpu.make_async_copy(h_ref, xs_ref.at[pl.ds(0, DISPATCH_TILE * ROW_TILES), :], sem).wait()


def _dispatch(h, pos_flat, zstart, n_used):
    grid_spec = pltpu.PrefetchScalarGridSpec(
        num_scalar_prefetch=2,
        grid=(TOKENS // DISPATCH_TILE,),
        in_specs=[
            pl.BlockSpec((DISPATCH_TILE * TOP_K,), lambda i, zs, nu: (i,),
                         memory_space=pltpu.SMEM),
            pl.BlockSpec((DISPATCH_TILE * ROW_TILES, 128), lambda i, zs, nu: (i, 0)),
        ],
        out_specs=pl.BlockSpec(memory_space=pl.ANY),
        scratch_shapes=[pltpu.VMEM((MOE_BLOCK * ROW_TILES, 128), jnp.float32),
                        pltpu.SemaphoreType.DMA, pltpu.SemaphoreType.DMA],
    )
    return pl.pallas_call(
        _dispatch_kernel,
        grid_spec=grid_spec,
        out_shape=jax.ShapeDtypeStruct((MOE_CAP * ROW_TILES, 128), jnp.float32),
        compiler_params=pltpu.CompilerParams(dimension_semantics=("arbitrary",)),
        name="moe_dispatch",
    )(zstart, n_used, pos_flat, h)


def _expert_kernel(be_ref, nused_ref, xs_ref, w1_ref, b1_ref, w2_ref, b2_ref, ys_ref,
                   w1b_ref, w2b_ref):
    i = pl.program_id(0)
    new_expert = jnp.logical_or(i == 0, be_ref[i] != be_ref[jnp.maximum(i - 1, 0)])

    @pl.when(new_expert)
    def _():
        for r in range(D_MODEL // 256):
            rows = slice(r * 256, (r + 1) * 256)
            w1b_ref[rows, :] = _bf16(w1_ref[0, rows, :])
            w2b_ref[rows, :] = _bf16(w2_ref[0, rows, :])

    @pl.when(i < nused_ref[0])
    def _():
        x = _bf16(jnp.concatenate([_load_token_major(xs_ref, MOE_BLOCK, c)
                                   for c in range(ROW_TILES)], axis=-1))
        glu = _dot(x, w1b_ref[:, :D_FF]) + b1_ref[0][:, :D_FF]
        lin = _dot(x, w1b_ref[:, D_FF:]) + b1_ref[0][:, D_FF:]
        glu = jnp.minimum(glu, SWIGLU_LIMIT)
        lin = jnp.clip(lin, -SWIGLU_LIMIT, SWIGLU_LIMIT)
        act = glu * (1.0 / (1.0 + jnp.exp(-SWIGLU_ALPHA * glu))) * (lin + 1.0)
        _store_token_major(ys_ref, _dot(_bf16(act), w2b_ref[...]) + b2_ref[0])

    @pl.when(i >= nused_ref[0])
    def _():
        ys_ref[...] = jnp.zeros_like(ys_ref)


def _experts(xs, block_e, n_used, layer, w1, b1, w2, b2):
    grid_spec = pltpu.PrefetchScalarGridSpec(
        num_scalar_prefetch=2,
        grid=(MOE_NBLOCKS,),
        in_specs=[
            pl.BlockSpec((MOE_BLOCK * ROW_TILES, 128),
                         lambda i, be, nu: (jnp.minimum(i, nu[0] - 1), 0)),
            pl.BlockSpec((None, 1, D_MODEL, 2 * D_FF), lambda i, be, nu: (layer, be[i], 0, 0)),
            pl.BlockSpec((None, 1, 1, 2 * D_FF), lambda i, be, nu: (layer, be[i], 0, 0)),
            pl.BlockSpec((None, 1, D_FF, D_MODEL), lambda i, be, nu: (layer, be[i], 0, 0)),
            pl.BlockSpec((None, 1, 1, D_MODEL), lambda i, be, nu: (layer, be[i], 0, 0)),
        ],
        out_specs=pl.BlockSpec((MOE_BLOCK * ROW_TILES, 128), lambda i, be, nu: (i, 0)),
        scratch_shapes=[pltpu.VMEM((D_MODEL, 2 * D_FF), jnp.bfloat16),
                        pltpu.VMEM((D_FF, D_MODEL), jnp.bfloat16)],
    )
    return pl.pallas_call(
        _expert_kernel,
        grid_spec=grid_spec,
        out_shape=jax.ShapeDtypeStruct((MOE_CAP * ROW_TILES, 128), jnp.float32),
        compiler_params=pltpu.CompilerParams(dimension_semantics=("arbitrary",),
                                             vmem_limit_bytes=VMEM_LIMIT_V7X),
        name="moe_experts",
    )(block_e, n_used, xs, w1, b1.reshape(DEPTH, N_EXPERTS, 1, 2 * D_FF), w2,
      b2.reshape(DEPTH, N_EXPERTS, 1, D_MODEL))


def _combine_kernel(pos_ref, pos_next_ref, x_ref, gates_ref, g2_ref, ys_ref, o_ref, buf_ref, sems):
    step = pl.program_id(0)
    n_steps = pl.num_programs(0)
    slot = step % 2

    def gather(p_ref, dst_slot):
        def issue(i, carry):
            for u in range(ISSUE_UNROLL):
                t = i * ISSUE_UNROLL + u
                for k in range(TOP_K):
                    _row_copy(ys_ref, p_ref[t * TOP_K + k], buf_ref.at[dst_slot, k], t,
                              sems.at[dst_slot]).start(priority=k % 2)
            return carry

        lax.fori_loop(0, DISPATCH_TILE // ISSUE_UNROLL, issue, 0)

    @pl.when(step == 0)
    def _():
        gather(pos_ref, 0)

    @pl.when(step + 1 < n_steps)
    def _():
        gather(pos_next_ref, 1 - slot)

    for k in range(TOP_K):
        pltpu.make_async_copy(ys_ref.at[pl.ds(0, DISPATCH_TILE * ROW_TILES), :],
                              buf_ref.at[slot, k], sems.at[slot]).wait()

    gates = gates_ref[...]
    gate_cols = [jnp.broadcast_to(gates[:, k:k + 1], (DISPATCH_TILE, 128)) for k in range(TOP_K)]
    for c in range(ROW_TILES):
        cols = slice(c * 128, (c + 1) * 128)
        y = gate_cols[0] * _load_token_major(buf_ref, DISPATCH_TILE, c, (slot, 0))
        for k in range(1, TOP_K):
            y = y + gate_cols[k] * _load_token_major(buf_ref, DISPATCH_TILE, c, (slot, k))
        o_ref[:, cols] = x_ref[:, cols] + g2_ref[0][:, cols] * y


def _combine(x, pos_flat, gates_tok, g2, ys):
    tiles_per_batch = SEQ // DISPATCH_TILE
    n_steps = TOKENS // DISPATCH_TILE
    return pl.pallas_call(
        _combine_kernel,
        grid=(n_steps,),
        in_specs=[
            pl.BlockSpec((DISPATCH_TILE * TOP_K,), lambda i: (i,), memory_space=pltpu.SMEM),
            pl.BlockSpec((DISPATCH_TILE * TOP_K,), lambda i: (jnp.minimum(i + 1, n_steps - 1),),
                         memory_space=pltpu.SMEM),
            pl.BlockSpec((DISPATCH_TILE, D_MODEL), lambda i: (i, 0)),
            pl.BlockSpec((DISPATCH_TILE, TOP_K), lambda i: (i, 0)),
            pl.BlockSpec((1, 1, D_MODEL), lambda i: (i // tiles_per_batch, 0, 0)),
            pl.BlockSpec(memory_space=pl.ANY),
        ],
        out_specs=pl.BlockSpec((DISPATCH_TILE, D_MODEL), lambda i: (i, 0)),
        out_shape=jax.ShapeDtypeStruct((TOKENS, D_MODEL), jnp.float32),
        scratch_shapes=[pltpu.VMEM((2, TOP_K, DISPATCH_TILE * ROW_TILES, 128), jnp.float32),
                        pltpu.SemaphoreType.DMA((2,))],
        compiler_params=pltpu.CompilerParams(dimension_semantics=("arbitrary",),
                                             vmem_limit_bytes=VMEM_LIMIT_V7X),
        name="moe_combine",
    )(pos_flat, pos_flat, x, gates_tok, g2, ys)


def _moe_layer(x, g, sc, sh, g2, router_w, router_b, layer, w1, b1, w2, b2):
    h, idx, rank, gates, cnt = _router(x, g, sc, sh, router_w, router_b)
    counts = cnt[:, 0].astype(jnp.int32)
    padded = ((counts + MOE_BLOCK - 1) // MOE_BLOCK) * MOE_BLOCK
    pends = jnp.cumsum(padded)
    pstarts = pends - padded
    experts = jnp.arange(N_EXPERTS, dtype=jnp.int32)[:, None, None]
    pos = rank + jnp.sum(jnp.where(idx[None] == experts, pstarts[:, None, None], 0), axis=0)
    pos_flat = pos.T.reshape(-1)
    block_start = jnp.arange(MOE_NBLOCKS, dtype=jnp.int32) * MOE_BLOCK
    block_e = jnp.minimum(jnp.sum(block_start[:, None] >= pends[None, :], axis=1),
                          N_EXPERTS - 1).astype(jnp.int32)
    n_used = (pends[-1:] // MOE_BLOCK).astype(jnp.int32)
    zstart = jnp.where(padded > 0, pends - MOE_BLOCK, -1).astype(jnp.int32)
    xs = _dispatch(h, pos_flat, zstart, n_used)
    ys = _experts(xs, block_e, n_used, layer, w1, b1, w2, b2)
    return _combine(x, pos_flat, gates.T, g2, ys)


def kernel(x, c, ada_w, ada_b, norm_mix_g, norm_ffn_g, ev_w_in, ev_w_out, sg_ln_g, sg_ln_b, sg_w, sg_b, od_w_in, od_w_out, od_q_g, od_k_g, od_rel_bias, moe_router_w, moe_router_b, moe_w1, moe_b1, moe_w2, moe_b2):
    mod = _ada_mod(c, ada_w, ada_b)
    xt = x.reshape(TOKENS, D_MODEL)
    for l in range(DEPTH):
        sh1, sc1, g1, sh2, sc2, g2 = [m.reshape(BATCH, 1, D_MODEL)
                                      for m in jnp.split(mod[l], 6, axis=-1)]
        i = l // 2
        if l % 2 == 0:
            proj = _norm_proj(xt, norm_mix_g[l], sc1, sh1, _bf16(ev_w_in[i]))
            a_out = _retention(proj)
            b_out = _spatial_gate(proj, sg_ln_g[i], sg_ln_b[i], sg_w[i], sg_b[i])
            xt = _out_proj([a_out, b_out], _bf16(ev_w_out[i]), xt, g1)
        else:
            qkv = _norm_proj(xt, norm_mix_g[l], sc1, sh1, _bf16(od_w_in[i]))
            att = _band_attention(qkv, od_q_g[i], od_k_g[i], od_rel_bias[i])
            xt = _out_proj([att], _bf16(od_w_out[i]), xt, g1)
        xt = _moe_layer(xt, norm_ffn_g[l], sc2, sh2, g2, moe_router_w[l], moe_router_b[l],
                        l, moe_w1, moe_b1, moe_w2, moe_b2)
    return xt.reshape(BATCH, SEQ, D_MODEL)
```

```python
import functools
import math

import numpy as np
import jax
import jax.numpy as jnp
from jax import lax
from jax.experimental import pallas as pl
from jax.experimental.pallas import tpu as pltpu

D_MODEL = 1024
BATCH = 8
SEQ = 2048
DEPTH = 4
TOKENS = BATCH * SEQ
CHUNK = 64
EPS = 1e-6

RET_HEADS = 4
RET_DK = 128
RET_DV = 256
ROPE_BASE = 10000.0
A_QK = RET_HEADS * RET_DK
A_V = RET_HEADS * RET_DV
SG_GROUPS = 4
SG_CH = 128
SG_WIDTH = SG_GROUPS * SG_CH
SG_LEN = 128
EVEN_IN = 2 * A_QK + 2 * A_V + 2 * SG_WIDTH
EVEN_MIX = A_V + SG_WIDTH

ATT_HEADS = 16
ATT_DH = 64
BAND_PAD = 8 * CHUNK
MAX_REL = 256
REL_SIZE = (CHUNK - 1) + MAX_REL + 1

N_EXPERTS = 32
TOP_K = 4
D_FF = D_MODEL
SWIGLU_ALPHA = 1.702
SWIGLU_LIMIT = 7.0

ROW_TILE = 512
COL_CHUNK = 512
RET_BLOCK = 256
ATT_QBLOCK = 128
ATT_BAND = ATT_QBLOCK + BAND_PAD
ATT_GROUP = 4
SG_TILE = 512
ROUTE_TILE = 512
MOE_BLOCK = 256
MOE_CAP = TOKENS * TOP_K + N_EXPERTS * MOE_BLOCK
MOE_NBLOCKS = MOE_CAP // MOE_BLOCK
DISPATCH_TILE = 256
ISSUE_UNROLL = 4
VMEM_LIMIT_V7X = 56 * 1024 * 1024
ROW_TILES = D_MODEL // 128

NEG_BIG = -1e30
LOG2_E = math.log2(math.e)


def _silu(x):
    return x * (1.0 / (1.0 + jnp.exp(-x)))


def _gelu_tanh(x):
    return 0.5 * x * (1.0 + jnp.tanh(math.sqrt(2.0 / math.pi) * (x + 0.044715 * (x * x * x))))


def _bf16(x):
    return x.astype(jnp.bfloat16)


def _dot(a, b):
    return jnp.dot(a, b, preferred_element_type=jnp.float32)


def _dot_nt(a, b):
    return lax.dot_general(a, b, (((1,), (1,)), ((), ())), preferred_element_type=jnp.float32)


def _dot_tn(a, b):
    return lax.dot_general(a, b, (((0,), (0,)), ((), ())), preferred_element_type=jnp.float32)


def _norm_mod(x, g, sc, sh):
    y = x * lax.rsqrt(jnp.mean(x * x, axis=-1, keepdims=True) + EPS)
    return (y * g) * (1.0 + sc) + sh


def _ada_kernel(c_ref, w_ref, b_ref, o_ref):
    c_act = _silu(c_ref[...])
    o_ref[0] = _dot(c_act, w_ref[0]) + b_ref[0]


def _ada_mod(c, ada_w, ada_b):
    n_col = 6
    return pl.pallas_call(
        _ada_kernel,
        grid=(DEPTH, n_col),
        in_specs=[
            pl.BlockSpec((BATCH, D_MODEL), lambda l, j: (0, 0)),
            pl.BlockSpec((1, D_MODEL, D_MODEL), lambda l, j: (l, 0, j)),
            pl.BlockSpec((1, 1, D_MODEL), lambda l, j: (l, 0, j)),
        ],
        out_specs=pl.BlockSpec((1, BATCH, D_MODEL), lambda l, j: (l, 0, j)),
        out_shape=jax.ShapeDtypeStruct((DEPTH, BATCH, 6 * D_MODEL), jnp.float32),
        name="ada_mod",
    )(c, ada_w, ada_b.reshape(DEPTH, 1, 6 * D_MODEL))


def _norm_proj_kernel(x_ref, g_ref, sc_ref, sh_ref, w_ref, o_ref):
    h = _bf16(_norm_mod(x_ref[...], g_ref[...], sc_ref[0], sh_ref[0]))
    n_out = o_ref.shape[1]
    for j in range(n_out // COL_CHUNK):
        cols = slice(j * COL_CHUNK, (j + 1) * COL_CHUNK)
        o_ref[:, cols] = _bf16(_dot(h, w_ref[:, cols]))


def _norm_proj(x, g, sc, sh, w_bf16):
    n_out = w_bf16.shape[1]
    tiles_per_batch = SEQ // ROW_TILE
    return pl.pallas_call(
        _norm_proj_kernel,
        grid=(TOKENS // ROW_TILE,),
        in_specs=[
            pl.BlockSpec((ROW_TILE, D_MODEL), lambda i: (i, 0)),
            pl.BlockSpec((1, D_MODEL), lambda i: (0, 0)),
            pl.BlockSpec((1, 1, D_MODEL), lambda i: (i // tiles_per_batch, 0, 0)),
            pl.BlockSpec((1, 1, D_MODEL), lambda i: (i // tiles_per_batch, 0, 0)),
            pl.BlockSpec((D_MODEL, n_out), lambda i: (0, 0)),
        ],
        out_specs=pl.BlockSpec((ROW_TILE, n_out), lambda i: (i, 0)),
        out_shape=jax.ShapeDtypeStruct((TOKENS, n_out), jnp.bfloat16),
        compiler_params=pltpu.CompilerParams(vmem_limit_bytes=VMEM_LIMIT_V7X),
        name="norm_proj",
    )(x, g.reshape(1, D_MODEL), sc, sh, w_bf16)


def _out_proj_kernel(*refs, widths):
    part_refs = refs[:len(widths)]
    w_ref, x_ref, g_ref, o_ref = refs[len(widths):]
    for j in range(D_MODEL // COL_CHUNK):
        cols = slice(j * COL_CHUNK, (j + 1) * COL_CHUNK)
        acc = None
        row0 = 0
        for p_ref, width in zip(part_refs, widths):
            term = _dot(p_ref[...], w_ref[row0:row0 + width, cols])
            acc = term if acc is None else acc + term
            row0 += width
        o_ref[:, cols] = x_ref[:, cols] + g_ref[0][:, cols] * acc


def _out_proj(parts, w_bf16, x, gate):
    widths = tuple(p.shape[1] for p in parts)
    tiles_per_batch = SEQ // ROW_TILE
    in_specs = [pl.BlockSpec((ROW_TILE, width), lambda i: (i, 0)) for width in widths]
    in_specs += [
        pl.BlockSpec((sum(widths), D_MODEL), lambda i: (0, 0)),
        pl.BlockSpec((ROW_TILE, D_MODEL), lambda i: (i, 0)),
        pl.BlockSpec((1, 1, D_MODEL), lambda i: (i // tiles_per_batch, 0, 0)),
    ]
    return pl.pallas_call(
        functools.partial(_out_proj_kernel, widths=widths),
        grid=(TOKENS // ROW_TILE,),
        in_specs=in_specs,
        out_specs=pl.BlockSpec((ROW_TILE, D_MODEL), lambda i: (i, 0)),
        out_shape=jax.ShapeDtypeStruct((TOKENS, D_MODEL), jnp.float32),
        compiler_params=pltpu.CompilerParams(vmem_limit_bytes=VMEM_LIMIT_V7X),
        name="out_proj",
    )(*parts, w_bf16, x, gate)


def _retention_tables():
    heads = np.arange(RET_HEADS, dtype=np.float64)
    log_g = np.log1p(-np.exp2(-5.0 - heads))
    idx = np.arange(RET_BLOCK, dtype=np.float64)
    diff = idx[:, None] - idx[None, :]
    ci, cj = (idx // CHUNK)[:, None], (idx // CHUNK)[None, :]
    expo = np.where(ci == cj, np.abs(diff), diff)
    decay = np.where(cj <= ci, np.exp(log_g[:, None, None] * expo[None]), 0.0)
    q_dec = np.exp(log_g[:, None] * (idx[None, :] + 1.0))
    k_dec = np.exp(log_g[:, None] * (RET_BLOCK - 1.0 - idx[None, :]))
    blk_dec = np.exp(log_g * RET_BLOCK)
    q_dec = np.broadcast_to(q_dec[:, :, None], (RET_HEADS, RET_BLOCK, RET_DK))
    k_dec = np.broadcast_to(k_dec[:, :, None], (RET_HEADS, RET_BLOCK, RET_DK))
    blk_dec = np.broadcast_to(blk_dec[:, None, None], (RET_HEADS, 1, RET_DV))
    half = RET_DK // 2
    inv = ROPE_BASE ** (-np.arange(half, dtype=np.float64) / half)
    ang = np.arange(SEQ, dtype=np.float64)[:, None] * inv[None, :]
    cos = np.concatenate([np.cos(ang), np.cos(ang)], axis=1)
    sin = np.concatenate([-np.sin(ang), np.sin(ang)], axis=1)
    f32 = lambda a: jnp.asarray(np.ascontiguousarray(a), jnp.float32)
    return f32(decay), f32(q_dec), f32(k_dec), f32(blk_dec), f32(cos), f32(sin)


def _retention_kernel(q_ref, k_ref, v_ref, gate_ref, cos_ref, sin_ref, dec_ref, qd_ref, kd_ref,
                      bd_ref, o_ref, state_ref):
    @pl.when(pl.program_id(1) == 0)
    def _():
        state_ref[...] = jnp.zeros_like(state_ref)

    cos, sin = cos_ref[...], sin_ref[...]
    for h in range(RET_HEADS):
        ks = slice(h * RET_DK, (h + 1) * RET_DK)
        vs = slice(h * RET_DV, (h + 1) * RET_DV)
        q = q_ref[:, ks].astype(jnp.float32)
        k = k_ref[:, ks].astype(jnp.float32)
        qr = q * cos + pltpu.roll(q, RET_DK // 2, axis=1) * sin
        kr = (k * cos + pltpu.roll(k, RET_DK // 2, axis=1) * sin) * (RET_DK ** -0.5)
        v = v_ref[:, vs]

        scores = _dot_nt(_bf16(qr), _bf16(kr)) * dec_ref[h]
        intra = _dot(_bf16(scores), v)
        state = state_ref[h]
        inter = _dot(_bf16(qr * qd_ref[h]), _bf16(state))
        state_ref[h] = state * bd_ref[h] + _dot_tn(_bf16(kr * kd_ref[h]), v)

        o = intra + inter
        mu = jnp.mean(o, axis=-1, keepdims=True)
        cen = o - mu
        var = jnp.mean(cen * cen, axis=-1, keepdims=True)
        normed = cen * lax.rsqrt(var + EPS)
        o_ref[:, vs] = _bf16(_silu(gate_ref[:, vs].astype(jnp.float32)) * normed)


def _retention(proj):
    decay, q_dec, k_dec, blk_dec, cos, sin = _retention_tables()
    nblk = SEQ // RET_BLOCK
    row = lambda b, n: b * nblk + n
    whole = lambda b, n: (0, 0, 0)
    return pl.pallas_call(
        _retention_kernel,
        grid=(BATCH, nblk),
        in_specs=[
            pl.BlockSpec((RET_BLOCK, A_QK), lambda b, n: (row(b, n), 0)),
            pl.BlockSpec((RET_BLOCK, A_QK), lambda b, n: (row(b, n), 1)),
            pl.BlockSpec((RET_BLOCK, A_V), lambda b, n: (row(b, n), 2 * A_QK // A_V)),
            pl.BlockSpec((RET_BLOCK, A_V), lambda b, n: (row(b, n), 2 * A_QK // A_V + 1)),
            pl.BlockSpec((RET_BLOCK, RET_DK), lambda b, n: (n, 0)),
            pl.BlockSpec((RET_BLOCK, RET_DK), lambda b, n: (n, 0)),
            pl.BlockSpec((RET_HEADS, RET_BLOCK, RET_BLOCK), whole),
            pl.BlockSpec((RET_HEADS, RET_BLOCK, RET_DK), whole),
            pl.BlockSpec((RET_HEADS, RET_BLOCK, RET_DK), whole),
            pl.BlockSpec((RET_HEADS, 1, RET_DV), whole),
        ],
        out_specs=pl.BlockSpec((RET_BLOCK, A_V), lambda b, n: (row(b, n), 0)),
        out_shape=jax.ShapeDtypeStruct((TOKENS, A_V), jnp.bfloat16),
        scratch_shapes=[pltpu.VMEM((RET_HEADS, RET_DK, RET_DV), jnp.float32)],
        compiler_params=pltpu.CompilerParams(dimension_semantics=("arbitrary", "arbitrary")),
        name="retention",
    )(proj, proj, proj, proj, cos, sin, decay, q_dec, k_dec, blk_dec)


def _spatial_gate_kernel(u_ref, z_ref, lng_ref, lnb_ref, w_ref, b_ref, o_ref):
    u = _gelu_tanh(u_ref[...].astype(jnp.float32))
    z = _gelu_tanh(z_ref[...].astype(jnp.float32))
    mu = jnp.mean(z, axis=-1, keepdims=True)
    cen = z - mu
    var = jnp.mean(cen * cen, axis=-1, keepdims=True)
    zn = _bf16(cen * lax.rsqrt(var + EPS) * lng_ref[...] + lnb_ref[...])
    rows = lax.broadcasted_iota(jnp.int32, (SG_LEN, SG_LEN), 0)
    cols = lax.broadcasted_iota(jnp.int32, (SG_LEN, SG_LEN), 1)
    keep = (rows // CHUNK) >= (cols // CHUNK)
    for g in range(SG_GROUPS):
        cs = slice(g * SG_CH, (g + 1) * SG_CH)
        w = _bf16(jnp.where(keep, w_ref[g], 0.0))
        for blk in range(SG_TILE // SG_LEN):
            rs = slice(blk * SG_LEN, (blk + 1) * SG_LEN)
            mixed = _dot(w, zn[rs, cs]) + b_ref[g]
            o_ref[rs, cs] = _bf16(u[rs, cs] * mixed)


def _spatial_gate(proj, ln_g, ln_b, w_s, b_s):
    u_col = (2 * A_QK + 2 * A_V) // SG_WIDTH
    b_full = jnp.broadcast_to(b_s[:, :, None], (SG_GROUPS, SG_LEN, SG_CH))
    return pl.pallas_call(
        _spatial_gate_kernel,
        grid=(TOKENS // SG_TILE,),
        in_specs=[
            pl.BlockSpec((SG_TILE, SG_WIDTH), lambda i: (i, u_col)),
            pl.BlockSpec((SG_TILE, SG_WIDTH), lambda i: (i, u_col + 1)),
            pl.BlockSpec((1, SG_WIDTH), lambda i: (0, 0)),
            pl.BlockSpec((1, SG_WIDTH), lambda i: (0, 0)),
            pl.BlockSpec((SG_GROUPS, SG_LEN, SG_LEN), lambda i: (0, 0, 0)),
            pl.BlockSpec((SG_GROUPS, SG_LEN, SG_CH), lambda i: (0, 0, 0)),
        ],
        out_specs=pl.BlockSpec((SG_TILE, SG_WIDTH), lambda i: (i, 0)),
        out_shape=jax.ShapeDtypeStruct((TOKENS, SG_WIDTH), jnp.bfloat16),
        name="spatial_gate",
    )(proj, proj, ln_g.reshape(1, SG_WIDTH), ln_b.reshape(1, SG_WIDTH), w_s, b_full)


def _band_bias(rel_bias):
    period = ATT_QBLOCK + ATT_BAND
    m = np.arange(period)
    shift = np.where(m < ATT_BAND, m, m - period)
    rel = BAND_PAD - shift
    table = rel_bias[:, np.clip(rel, -(CHUNK - 1), MAX_REL) + (CHUNK - 1)].astype(jnp.float32)
    flat = jnp.tile(table, (1, ATT_QBLOCK))[:, :ATT_QBLOCK * (period - 1)]
    bias = flat.reshape(ATT_HEADS, ATT_QBLOCK, period - 1)[:, :, :ATT_BAND]
    i = np.arange(ATT_QBLOCK)[:, None]
    jj = np.arange(ATT_BAND)[None, :]
    in_band = (jj // CHUNK >= i // CHUNK) & (jj // CHUNK <= i // CHUNK + BAND_PAD // CHUNK)
    return jnp.where(jnp.asarray(in_band)[None], bias * LOG2_E, NEG_BIG)


def _head_rms(x, gain):
    head_of_lane = lax.broadcasted_iota(jnp.int32, x.shape, 1) // ATT_DH
    sq = x * x
    ms = jnp.zeros_like(x)
    for h in range(ATT_GROUP):
        in_head = head_of_lane == h
        ms = jnp.where(in_head, jnp.sum(jnp.where(in_head, sq, 0.0), axis=-1, keepdims=True), ms)
    return x * lax.rsqrt(ms * (1.0 / ATT_DH) + EPS) * gain


def _band_attn_kernel(q_ref, k_ref, v_ref, bias_ref, qg_ref, kg_ref, o_ref, kpad_ref, vpad_ref,
                      s_ref, p_ref, den_ref):
    n = pl.program_id(2)
    pair = 2 * ATT_DH

    @pl.when(n == 0)
    def _():
        kn = _bf16(_head_rms(k_ref[...].astype(jnp.float32), kg_ref[...]))
        v = v_ref[...]
        low_half = lax.broadcasted_iota(jnp.int32, (SEQ, pair), 1) < ATT_DH
        zero = jnp.zeros((SEQ, pair), jnp.bfloat16)
        for h in range(ATT_GROUP):
            cols = slice((h // 2) * pair, (h // 2 + 1) * pair)
            own = low_half if h % 2 == 0 else jnp.logical_not(low_half)
            kpad_ref[h, 0:BAND_PAD, :] = jnp.zeros((BAND_PAD, pair), jnp.bfloat16)
            vpad_ref[h, 0:BAND_PAD, :] = jnp.zeros((BAND_PAD, pair), jnp.bfloat16)
            kpad_ref[h, BAND_PAD:, :] = jnp.where(own, kn[:, cols], zero)
            vpad_ref[h, BAND_PAD:, :] = jnp.where(own, v[:, cols], zero)

    q = _bf16(_head_rms(q_ref[...].astype(jnp.float32), qg_ref[...]) * (LOG2_E * ATT_DH ** -0.5))
    start = pl.multiple_of(n * ATT_QBLOCK, ATT_QBLOCK)
    for h in range(ATT_GROUP):
        q_pair = q[:, (h // 2) * pair:(h // 2 + 1) * pair]
        s_ref[h] = _dot_nt(q_pair, kpad_ref[h, pl.ds(start, ATT_BAND), :])

    key_pos = lax.broadcasted_iota(jnp.int32, (ATT_QBLOCK, ATT_BAND), 1) + start
    valid = key_pos >= BAND_PAD
    for h in range(ATT_GROUP):
        s = jnp.where(valid, s_ref[h] + bias_ref[h], NEG_BIG)
        m = jnp.max(s, axis=-1, keepdims=True)
        p = jnp.exp2(s - m)
        den_ref[h] = jnp.sum(p, axis=-1, keepdims=True)
        p_ref[h] = _bf16(p)
    outs = []
    for h0 in range(0, ATT_GROUP, 2):
        pv = [_dot(p_ref[h], vpad_ref[h, pl.ds(start, ATT_BAND), :]) * (1.0 / den_ref[h])
              for h in (h0, h0 + 1)]
        outs.append(pv[0] + pv[1])
    o_ref[...] = _bf16(jnp.concatenate(outs, axis=-1))


def _band_attention(qkv, q_g, k_g, rel_bias):
    bias = _band_bias(rel_bias)
    nq = SEQ // ATT_QBLOCK
    width = ATT_GROUP * ATT_DH
    n_groups = ATT_HEADS // ATT_GROUP
    qg = jnp.tile(q_g, ATT_GROUP).reshape(1, width)
    kg = jnp.tile(k_g, ATT_GROUP).reshape(1, width)
    return pl.pallas_call(
        _band_attn_kernel,
        grid=(BATCH, n_groups, nq),
        in_specs=[
            pl.BlockSpec((ATT_QBLOCK, width), lambda b, hg, n: (b * nq + n, hg)),
            pl.BlockSpec((SEQ, width), lambda b, hg, n: (b, n_groups + hg)),
            pl.BlockSpec((SEQ, width), lambda b, hg, n: (b, 2 * n_groups + hg)),
            pl.BlockSpec((ATT_GROUP, ATT_QBLOCK, ATT_BAND), lambda b, hg, n: (hg, 0, 0)),
            pl.BlockSpec((1, width), lambda b, hg, n: (0, 0)),
            pl.BlockSpec((1, width), lambda b, hg, n: (0, 0)),
        ],
        out_specs=pl.BlockSpec((ATT_QBLOCK, width), lambda b, hg, n: (b * nq + n, hg)),
        out_shape=jax.ShapeDtypeStruct((TOKENS, D_MODEL), jnp.bfloat16),
        scratch_shapes=[pltpu.VMEM((ATT_GROUP, SEQ + BAND_PAD, 2 * ATT_DH), jnp.bfloat16),
                        pltpu.VMEM((ATT_GROUP, SEQ + BAND_PAD, 2 * ATT_DH), jnp.bfloat16),
                        pltpu.VMEM((ATT_GROUP, ATT_QBLOCK, ATT_BAND), jnp.float32),
                        pltpu.VMEM((ATT_GROUP, ATT_QBLOCK, ATT_BAND), jnp.bfloat16),
                        pltpu.VMEM((ATT_GROUP, ATT_QBLOCK, 1), jnp.float32)],
        compiler_params=pltpu.CompilerParams(
            dimension_semantics=("arbitrary", "arbitrary", "arbitrary")),
        name="band_attention",
    )(qkv, qkv, qkv, bias, qg, kg)


def _store_token_major(ref, value, lead=()):
    rows = value.shape[0]
    for c in range(ROW_TILES):
        ref[(*lead, pl.ds(c, rows, stride=ROW_TILES), slice(None))] = value[:, c * 128:(c + 1) * 128]


def _load_token_major(ref, rows, c, lead=()):
    return ref[(*lead, pl.ds(c, rows, stride=ROW_TILES), slice(None))]


def _split_bf16(x):
    hi = _bf16(x)
    lo = _bf16(x - hi.astype(jnp.float32))
    return hi, lo


def _router_kernel(x_ref, g_ref, sc_ref, sh_ref, rw_ref, rb_ref,
                   h_ref, idx_ref, rank_ref, gate_ref, cnt_ref, carry_ref):
    @pl.when(pl.program_id(0) == 0)
    def _():
        carry_ref[...] = jnp.zeros_like(carry_ref)

    h = _norm_mod(x_ref[...], g_ref[...], sc_ref[0], sh_ref[0])
    _store_token_major(h_ref, h)
    h_hi, h_lo = _split_bf16(h)
    w_hi, w_lo = _split_bf16(rw_ref[...])
    logits = _dot_nt(w_hi, h_hi) + _dot_nt(w_hi, h_lo) + _dot_nt(w_lo, h_hi) + rb_ref[...]

    expert = lax.broadcasted_iota(jnp.int32, logits.shape, 0).astype(jnp.float32)
    work = logits
    vals, idxs, sels = [], [], []
    for _ in range(TOP_K):
        m = jnp.max(work, axis=0, keepdims=True)
        pick = jnp.min(jnp.where(work == m, expert, float(N_EXPERTS)), axis=0, keepdims=True)
        sel = expert == pick
        work = jnp.where(sel, -jnp.inf, work)
        vals.append(m)
        idxs.append(pick)
        sels.append(sel)
    exps = [jnp.exp(v - vals[0]) for v in vals]
    denom = exps[0] + exps[1] + exps[2] + exps[3]
    gate_ref[...] = jnp.concatenate([e / denom for e in exps], axis=0)
    idx_ref[...] = jnp.concatenate(idxs, axis=0).astype(jnp.int32)

    chosen = jnp.zeros(logits.shape, jnp.float32)
    for sel in sels:
        chosen = jnp.where(sel, 1.0, chosen)
    tile = logits.shape[1]
    earlier = (lax.broadcasted_iota(jnp.int32, (tile, tile), 0)
               < lax.broadcasted_iota(jnp.int32, (tile, tile), 1))
    before = _dot(_bf16(chosen), jnp.where(earlier, 1.0, 0.0).astype(jnp.bfloat16))
    rank_full = before + carry_ref[...]
    ranks = [jnp.sum(jnp.where(sel, rank_full, 0.0), axis=0, keepdims=True) for sel in sels]
    rank_ref[...] = jnp.concatenate(ranks, axis=0).astype(jnp.int32)
    carry = carry_ref[...] + jnp.sum(chosen, axis=1, keepdims=True)
    carry_ref[...] = carry
    cnt_ref[...] = jnp.broadcast_to(carry, cnt_ref.shape)


def _router(x, g, sc, sh, router_w, router_b):
    tiles_per_batch = SEQ // ROUTE_TILE
    lane_out = lambda dt: jax.ShapeDtypeStruct((TOP_K, TOKENS), dt)
    lane_spec = pl.BlockSpec((TOP_K, ROUTE_TILE), lambda i: (0, i))
    return pl.pallas_call(
        _router_kernel,
        grid=(TOKENS // ROUTE_TILE,),
        in_specs=[
            pl.BlockSpec((ROUTE_TILE, D_MODEL), lambda i: (i, 0)),
            pl.BlockSpec((1, D_MODEL), lambda i: (0, 0)),
            pl.BlockSpec((1, 1, D_MODEL), lambda i: (i // tiles_per_batch, 0, 0)),
            pl.BlockSpec((1, 1, D_MODEL), lambda i: (i // tiles_per_batch, 0, 0)),
            pl.BlockSpec((N_EXPERTS, D_MODEL), lambda i: (0, 0)),
            pl.BlockSpec((N_EXPERTS, 1), lambda i: (0, 0)),
        ],
        out_specs=[
            pl.BlockSpec((ROUTE_TILE * ROW_TILES, 128), lambda i: (i, 0)),
            lane_spec, lane_spec, lane_spec,
            pl.BlockSpec((N_EXPERTS, 128), lambda i: (0, 0)),
        ],
        out_shape=[
            jax.ShapeDtypeStruct((TOKENS * ROW_TILES, 128), jnp.float32),
            lane_out(jnp.int32), lane_out(jnp.int32), lane_out(jnp.float32),
            jax.ShapeDtypeStruct((N_EXPERTS, 128), jnp.float32),
        ],
        scratch_shapes=[pltpu.VMEM((N_EXPERTS, 1), jnp.float32)],
        compiler_params=pltpu.CompilerParams(dimension_semantics=("arbitrary",)),
        name="moe_router",
    )(x, g.reshape(1, D_MODEL), sc, sh, router_w.T, router_b.reshape(N_EXPERTS, 1))


def _row_copy(src_ref, src_row, dst_ref, dst_row, sem):
    src = pl.multiple_of(src_row * ROW_TILES, ROW_TILES)
    dst = pl.multiple_of(dst_row * ROW_TILES, ROW_TILES)
    return pltpu.make_async_copy(src_ref.at[pl.ds(src, ROW_TILES), :],
                                 dst_ref.at[pl.ds(dst, ROW_TILES), :], sem)


def _zero_block_copy(zero_ref, xs_ref, row, sem):
    row = pl.multiple_of(row * ROW_TILES, MOE_BLOCK * ROW_TILES)
    return pltpu.make_async_copy(zero_ref, xs_ref.at[pl.ds(row, MOE_BLOCK * ROW_TILES), :], sem)


def _dispatch_kernel(zstart_ref, nused_ref, pos_ref, h_ref, xs_ref, zero_ref, zsem, sem):
    step = pl.program_id(0)

    @pl.when(step == 0)
    def _():
        zero_ref[...] = jnp.zeros_like(zero_ref)

        def each_block(fn):
            def expert_last(e, carry):
                @pl.when(zstart_ref[e] >= 0)
                def _():
                    fn(_zero_block_copy(zero_ref, xs_ref, zstart_ref[e], zsem))
                return carry

            def unused(blk, carry):
                fn(_zero_block_copy(zero_ref, xs_ref, blk * MOE_BLOCK, zsem))
                return carry

            lax.fori_loop(0, N_EXPERTS, expert_last, 0)
            lax.fori_loop(nused_ref[0], MOE_NBLOCKS, unused, 0)

        each_block(lambda copy: copy.start())
        each_block(lambda copy: copy.wait())

    def issue(i, carry):
        for u in range(ISSUE_UNROLL):
            t = i * ISSUE_UNROLL + u
            for k in range(TOP_K):
                _row_copy(h_ref, t, xs_ref, pos_ref[t * TOP_K + k], sem).start(priority=k % 2)
        return carry

    lax.fori_loop(0, DISPATCH_TILE // ISSUE_UNROLL, issue, 0)

    for _ in range(TOP_K):
        pltpu.make_async_copy(h_ref, xs_ref.at[pl.ds(0, DISPATCH_TILE * ROW_TILES), :], sem).wait()


def _dispatch(h, pos_flat, zstart, n_used):
    grid_spec = pltpu.PrefetchScalarGridSpec(
        num_scalar_prefetch=2,
        grid=(TOKENS // DISPATCH_TILE,),
        in_specs=[
            pl.BlockSpec((DISPATCH_TILE * TOP_K,), lambda i, zs, nu: (i,),
                         memory_space=pltpu.SMEM),
            pl.BlockSpec((DISPATCH_TILE * ROW_TILES, 128), lambda i, zs, nu: (i, 0)),
        ],
        out_specs=pl.BlockSpec(memory_space=pl.ANY),
        scratch_shapes=[pltpu.VMEM((MOE_BLOCK * ROW_TILES, 128), jnp.float32),
                        pltpu.SemaphoreType.DMA, pltpu.SemaphoreType.DMA],
    )
    return pl.pallas_call(
        _dispatch_kernel,
        grid_spec=grid_spec,
        out_shape=jax.ShapeDtypeStruct((MOE_CAP * ROW_TILES, 128), jnp.float32),
        compiler_params=pltpu.CompilerParams(dimension_semantics=("arbitrary",)),
        name="moe_dispatch",
    )(zstart, n_used, pos_flat, h)


def _expert_kernel(be_ref, nused_ref, next_ref, xs_ref, b1_ref, b2_ref, w1_hbm, w2_hbm, ys_ref,
                   w1f_ref, w2f_ref, w1b_ref, w2b_ref, sems, *, layer):
    i = pl.program_id(0)
    expert = be_ref[i]
    used = i < nused_ref[0]
    new_expert = jnp.logical_or(i == 0, expert != be_ref[jnp.maximum(i - 1, 0)])

    def weight_copies(e):
        return (pltpu.make_async_copy(w1_hbm.at[layer, e], w1f_ref, sems.at[0]),
                pltpu.make_async_copy(w2_hbm.at[layer, e], w2f_ref, sems.at[1]))

    @pl.when(jnp.logical_and(used, new_expert))
    def _():
        @pl.when(i == 0)
        def _():
            for copy in weight_copies(expert):
                copy.start()

        for copy in weight_copies(expert):
            copy.wait()
        for r in range(D_MODEL // 256):
            rows = slice(r * 256, (r + 1) * 256)
            w1b_ref[rows, :] = _bf16(w1f_ref[rows, :])
            w2b_ref[rows, :] = _bf16(w2f_ref[rows, :])

        @pl.when(next_ref[i] >= 0)
        def _():
            for copy in weight_copies(next_ref[i]):
                copy.start()

    @pl.when(used)
    def _():
        x = _bf16(jnp.concatenate([_load_token_major(xs_ref, MOE_BLOCK, c)
                                   for c in range(ROW_TILES)], axis=-1))
        glu = _dot(x, w1b_ref[:, :D_FF]) + b1_ref[0][:, :D_FF]
        lin = _dot(x, w1b_ref[:, D_FF:]) + b1_ref[0][:, D_FF:]
        glu = jnp.minimum(glu, SWIGLU_LIMIT)
        lin = jnp.clip(lin, -SWIGLU_LIMIT, SWIGLU_LIMIT)
        act = glu * (1.0 / (1.0 + jnp.exp(-SWIGLU_ALPHA * glu))) * (lin + 1.0)
        _store_token_major(ys_ref, _dot(_bf16(act), w2b_ref[...]) + b2_ref[0])

    @pl.when(jnp.logical_not(used))
    def _():
        ys_ref[...] = jnp.zeros_like(ys_ref)


def _experts(xs, block_e, n_used, next_e, layer, w1, b1, w2, b2):
    grid_spec = pltpu.PrefetchScalarGridSpec(
        num_scalar_prefetch=3,
        grid=(MOE_NBLOCKS,),
        in_specs=[
            pl.BlockSpec((MOE_BLOCK * ROW_TILES, 128),
                         lambda i, be, nu, nx: (jnp.minimum(i, nu[0] - 1), 0)),
            pl.BlockSpec((None, 1, 1, 2 * D_FF), lambda i, be, nu, nx: (layer, be[i], 0, 0)),
            pl.BlockSpec((None, 1, 1, D_MODEL), lambda i, be, nu, nx: (layer, be[i], 0, 0)),
            pl.BlockSpec(memory_space=pl.ANY),
            pl.BlockSpec(memory_space=pl.ANY),
        ],
        out_specs=pl.BlockSpec((MOE_BLOCK * ROW_TILES, 128), lambda i, be, nu, nx: (i, 0)),
        scratch_shapes=[pltpu.VMEM((D_MODEL, 2 * D_FF), jnp.float32),
                        pltpu.VMEM((D_FF, D_MODEL), jnp.float32),
                        pltpu.VMEM((D_MODEL, 2 * D_FF), jnp.bfloat16),
                        pltpu.VMEM((D_FF, D_MODEL), jnp.bfloat16),
                        pltpu.SemaphoreType.DMA((2,))],
    )
    return pl.pallas_call(
        functools.partial(_expert_kernel, layer=layer),
        grid_spec=grid_spec,
        out_shape=jax.ShapeDtypeStruct((MOE_CAP * ROW_TILES, 128), jnp.float32),
        compiler_params=pltpu.CompilerParams(dimension_semantics=("arbitrary",),
                                             vmem_limit_bytes=VMEM_LIMIT_V7X),
        name="moe_experts",
    )(block_e, n_used, next_e, xs, b1.reshape(DEPTH, N_EXPERTS, 1, 2 * D_FF),
      b2.reshape(DEPTH, N_EXPERTS, 1, D_MODEL), w1, w2)


def _combine_kernel(pos_ref, pos_next_ref, x_ref, gates_ref, g2_ref, ys_ref, o_ref, buf_ref, sems):
    step = pl.program_id(0)
    n_steps = pl.num_programs(0)
    slot = step % 2

    def gather(p_ref, dst_slot):
        def issue(i, carry):
            for u in range(ISSUE_UNROLL):
                t = i * ISSUE_UNROLL + u
                for k in range(TOP_K):
                    _row_copy(ys_ref, p_ref[t * TOP_K + k], buf_ref.at[dst_slot, k], t,
                              sems.at[dst_slot]).start(priority=k % 2)
            return carry

        lax.fori_loop(0, DISPATCH_TILE // ISSUE_UNROLL, issue, 0)

    @pl.when(step == 0)
    def _():
        gather(pos_ref, 0)

    @pl.when(step + 1 < n_steps)
    def _():
        gather(pos_next_ref, 1 - slot)

    for k in range(TOP_K):
        pltpu.make_async_copy(ys_ref.at[pl.ds(0, DISPATCH_TILE * ROW_TILES), :],
                              buf_ref.at[slot, k], sems.at[slot]).wait()

    gates = gates_ref[...]
    gate_cols = [jnp.broadcast_to(gates[:, k:k + 1], (DISPATCH_TILE, 128)) for k in range(TOP_K)]
    for c in range(ROW_TILES):
        cols = slice(c * 128, (c + 1) * 128)
        y = gate_cols[0] * _load_token_major(buf_ref, DISPATCH_TILE, c, (slot, 0))
        for k in range(1, TOP_K):
            y = y + gate_cols[k] * _load_token_major(buf_ref, DISPATCH_TILE, c, (slot, k))
        o_ref[:, cols] = x_ref[:, cols] + g2_ref[0][:, cols] * y


def _combine(x, pos_flat, gates_tok, g2, ys):
    tiles_per_batch = SEQ // DISPATCH_TILE
    n_steps = TOKENS // DISPATCH_TILE
    return pl.pallas_call(
        _combine_kernel,
        grid=(n_steps,),
        in_specs=[
            pl.BlockSpec((DISPATCH_TILE * TOP_K,), lambda i: (i,), memory_space=pltpu.SMEM),
            pl.BlockSpec((DISPATCH_TILE * TOP_K,), lambda i: (jnp.minimum(i + 1, n_steps - 1),),
                         memory_space=pltpu.SMEM),
            pl.BlockSpec((DISPATCH_TILE, D_MODEL), lambda i: (i, 0)),
            pl.BlockSpec((DISPATCH_TILE, TOP_K), lambda i: (i, 0)),
            pl.BlockSpec((1, 1, D_MODEL), lambda i: (i // tiles_per_batch, 0, 0)),
            pl.BlockSpec(memory_space=pl.ANY),
        ],
        out_specs=pl.BlockSpec((DISPATCH_TILE, D_MODEL), lambda i: (i, 0)),
        out_shape=jax.ShapeDtypeStruct((TOKENS, D_MODEL), jnp.float32),
        scratch_shapes=[pltpu.VMEM((2, TOP_K, DISPATCH_TILE * ROW_TILES, 128), jnp.float32),
                        pltpu.SemaphoreType.DMA((2,))],
        compiler_params=pltpu.CompilerParams(dimension_semantics=("arbitrary",),
                                             vmem_limit_bytes=VMEM_LIMIT_V7X),
        name="moe_combine",
    )(pos_flat, pos_flat, x, gates_tok, g2, ys)


def _moe_layer(x, g, sc, sh, g2, router_w, router_b, layer, w1, b1, w2, b2):
    h, idx, rank, gates, cnt = _router(x, g, sc, sh, router_w, router_b)
    counts = cnt[:, 0].astype(jnp.int32)
    padded = ((counts + MOE_BLOCK - 1) // MOE_BLOCK) * MOE_BLOCK
    pends = jnp.cumsum(padded)
    pstarts = pends - padded
    experts = jnp.arange(N_EXPERTS, dtype=jnp.int32)[:, None, None]
    pos = rank + jnp.sum(jnp.where(idx[None] == experts, pstarts[:, None, None], 0), axis=0)
    pos_flat = pos.T.reshape(-1)
    block_start = jnp.arange(MOE_NBLOCKS, dtype=jnp.int32) * MOE_BLOCK
    block_e = jnp.minimum(jnp.sum(block_start[:, None] >= pends[None, :], axis=1),
                          N_EXPERTS - 1).astype(jnp.int32)
    n_used = (pends[-1:] // MOE_BLOCK).astype(jnp.int32)
    zstart = jnp.where(padded > 0, pends - MOE_BLOCK, -1).astype(jnp.int32)
    ids = jnp.arange(N_EXPERTS, dtype=jnp.int32)
    later_used = jnp.logical_and(ids[None, :] > ids[:, None], (padded > 0)[None, :])
    next_used = jnp.min(jnp.where(later_used, ids[None, :], N_EXPERTS), axis=1)
    next_used = jnp.where(next_used < N_EXPERTS, next_used, -1)
    next_e = jnp.sum(jnp.where(block_e[:, None] == ids[None, :], next_used[None, :], 0),
                     axis=1).astype(jnp.int32)
    xs = _dispatch(h, pos_flat, zstart, n_used)
    ys = _experts(xs, block_e, n_used, next_e, layer, w1, b1, w2, b2)
    return _combine(x, pos_flat, gates.T, g2, ys)


def kernel(x, c, ada_w, ada_b, norm_mix_g, norm_ffn_g, ev_w_in, ev_w_out, sg_ln_g, sg_ln_b, sg_w, sg_b, od_w_in, od_w_out, od_q_g, od_k_g, od_rel_bias, moe_router_w, moe_router_b, moe_w1, moe_b1, moe_w2, moe_b2):
    mod = _ada_mod(c, ada_w, ada_b)
    xt = x.reshape(TOKENS, D_MODEL)
    for l in range(DEPTH):
        sh1, sc1, g1, sh2, sc2, g2 = [m.reshape(BATCH, 1, D_MODEL)
                                      for m in jnp.split(mod[l], 6, axis=-1)]
        i = l // 2
        if l % 2 == 0:
            proj = _norm_proj(xt, norm_mix_g[l], sc1, sh1, _bf16(ev_w_in[i]))
            a_out = _retention(proj)
            b_out = _spatial_gate(proj, sg_ln_g[i], sg_ln_b[i], sg_w[i], sg_b[i])
            xt = _out_proj([a_out, b_out], _bf16(ev_w_out[i]), xt, g1)
        else:
            qkv = _norm_proj(xt, norm_mix_g[l], sc1, sh1, _bf16(od_w_in[i]))
            att = _band_attention(qkv, od_q_g[i], od_k_g[i], od_rel_bias[i])
            xt = _out_proj([att], _bf16(od_w_out[i]), xt, g1)
        xt = _moe_layer(xt, norm_ffn_g[l], sc2, sh2, g2, moe_router_w[l], moe_router_b[l],
                        l, moe_w1, moe_b1, moe_w2, moe_b2)
    return xt.reshape(BATCH, SEQ, D_MODEL)
```

```python
import functools
import math

import numpy as np
import jax
import jax.numpy as jnp
from jax import lax
from jax.experimental import pallas as pl
from jax.experimental.pallas import tpu as pltpu

D_MODEL = 1024
BATCH = 8
SEQ = 2048
DEPTH = 4
TOKENS = BATCH * SEQ
CHUNK = 64
EPS = 1e-6

RET_HEADS = 4
RET_DK = 128
RET_DV = 256
ROPE_BASE = 10000.0
A_QK = RET_HEADS * RET_DK
A_V = RET_HEADS * RET_DV
SG_GROUPS = 4
SG_CH = 128
SG_WIDTH = SG_GROUPS * SG_CH
SG_LEN = 128
EVEN_IN = 2 * A_QK + 2 * A_V + 2 * SG_WIDTH
EVEN_MIX = A_V + SG_WIDTH

ATT_HEADS = 16
ATT_DH = 64
BAND_PAD = 8 * CHUNK
MAX_REL = 256
REL_SIZE = (CHUNK - 1) + MAX_REL + 1

N_EXPERTS = 32
TOP_K = 4
D_FF = D_MODEL
SWIGLU_ALPHA = 1.702
SWIGLU_LIMIT = 7.0

ROW_TILE = 512
COL_CHUNK = 512
RET_BLOCK = 256
ATT_QBLOCK = 128
ATT_BAND = ATT_QBLOCK + BAND_PAD
ATT_GROUP = 4
SG_TILE = 512
ROUTE_TILE = 512
MOE_BLOCK = 256
MOE_CAP = TOKENS * TOP_K + N_EXPERTS * MOE_BLOCK
MOE_NBLOCKS = MOE_CAP // MOE_BLOCK
DISPATCH_TILE = 256
ISSUE_UNROLL = 4
DEST_SLOTS = MOE_CAP + 2 * MOE_BLOCK
GATHERED_ROWS = TOKENS * TOP_K + 2 * MOE_BLOCK
assert MOE_NBLOCKS % 2 == 0
VMEM_LIMIT_V7X = 56 * 1024 * 1024
ROW_TILES = D_MODEL // 128

NEG_BIG = -1e30
LOG2_E = math.log2(math.e)


def _silu(x):
    return x * (1.0 / (1.0 + jnp.exp(-x)))


def _gelu_tanh(x):
    return 0.5 * x * (1.0 + jnp.tanh(math.sqrt(2.0 / math.pi) * (x + 0.044715 * (x * x * x))))


def _bf16(x):
    return x.astype(jnp.bfloat16)


def _dot(a, b):
    return jnp.dot(a, b, preferred_element_type=jnp.float32)


def _dot_nt(a, b):
    return lax.dot_general(a, b, (((1,), (1,)), ((), ())), preferred_element_type=jnp.float32)


def _dot_tn(a, b):
    return lax.dot_general(a, b, (((0,), (0,)), ((), ())), preferred_element_type=jnp.float32)


def _norm_mod(x, g, sc, sh):
    y = x * lax.rsqrt(jnp.mean(x * x, axis=-1, keepdims=True) + EPS)
    return (y * g) * (1.0 + sc) + sh


def _ada_kernel(c_ref, w_ref, b_ref, o_ref):
    c_act = _silu(c_ref[...])
    o_ref[0] = _dot(c_act, w_ref[0]) + b_ref[0]


def _ada_mod(c, ada_w, ada_b):
    n_col = 6
    return pl.pallas_call(
        _ada_kernel,
        grid=(DEPTH, n_col),
        in_specs=[
            pl.BlockSpec((BATCH, D_MODEL), lambda l, j: (0, 0)),
            pl.BlockSpec((1, D_MODEL, D_MODEL), lambda l, j: (l, 0, j)),
            pl.BlockSpec((1, 1, D_MODEL), lambda l, j: (l, 0, j)),
        ],
        out_specs=pl.BlockSpec((1, BATCH, D_MODEL), lambda l, j: (l, 0, j)),
        out_shape=jax.ShapeDtypeStruct((DEPTH, BATCH, 6 * D_MODEL), jnp.float32),
        name="ada_mod",
    )(c, ada_w, ada_b.reshape(DEPTH, 1, 6 * D_MODEL))


def _norm_proj_kernel(x_ref, g_ref, sc_ref, sh_ref, w_ref, o_ref):
    h = _bf16(_norm_mod(x_ref[...], g_ref[...], sc_ref[0], sh_ref[0]))
    n_out = o_ref.shape[1]
    for j in range(n_out // COL_CHUNK):
        cols = slice(j * COL_CHUNK, (j + 1) * COL_CHUNK)
        o_ref[:, cols] = _bf16(_dot(h, w_ref[:, cols]))


def _norm_proj(x, g, sc, sh, w_bf16):
    n_out = w_bf16.shape[1]
    tiles_per_batch = SEQ // ROW_TILE
    return pl.pallas_call(
        _norm_proj_kernel,
        grid=(TOKENS // ROW_TILE,),
        in_specs=[
            pl.BlockSpec((ROW_TILE, D_MODEL), lambda i: (i, 0)),
            pl.BlockSpec((1, D_MODEL), lambda i: (0, 0)),
            pl.BlockSpec((1, 1, D_MODEL), lambda i: (i // tiles_per_batch, 0, 0)),
            pl.BlockSpec((1, 1, D_MODEL), lambda i: (i // tiles_per_batch, 0, 0)),
            pl.BlockSpec((D_MODEL, n_out), lambda i: (0, 0)),
        ],
        out_specs=pl.BlockSpec((ROW_TILE, n_out), lambda i: (i, 0)),
        out_shape=jax.ShapeDtypeStruct((TOKENS, n_out), jnp.bfloat16),
        compiler_params=pltpu.CompilerParams(vmem_limit_bytes=VMEM_LIMIT_V7X),
        name="norm_proj",
    )(x, g.reshape(1, D_MODEL), sc, sh, w_bf16)


def _out_proj_kernel(*refs, widths):
    part_refs = refs[:len(widths)]
    w_ref, x_ref, g_ref, o_ref = refs[len(widths):]
    for j in range(D_MODEL // COL_CHUNK):
        cols = slice(j * COL_CHUNK, (j + 1) * COL_CHUNK)
        acc = None
        row0 = 0
        for p_ref, width in zip(part_refs, widths):
            term = _dot(p_ref[...], w_ref[row0:row0 + width, cols])
            acc = term if acc is None else acc + term
            row0 += width
        o_ref[:, cols] = x_ref[:, cols] + g_ref[0][:, cols] * acc


def _out_proj(parts, w_bf16, x, gate):
    widths = tuple(p.shape[1] for p in parts)
    tiles_per_batch = SEQ // ROW_TILE
    in_specs = [pl.BlockSpec((ROW_TILE, width), lambda i: (i, 0)) for width in widths]
    in_specs += [
        pl.BlockSpec((sum(widths), D_MODEL), lambda i: (0, 0)),
        pl.BlockSpec((ROW_TILE, D_MODEL), lambda i: (i, 0)),
        pl.BlockSpec((1, 1, D_MODEL), lambda i: (i // tiles_per_batch, 0, 0)),
    ]
    return pl.pallas_call(
        functools.partial(_out_proj_kernel, widths=widths),
        grid=(TOKENS // ROW_TILE,),
        in_specs=in_specs,
        out_specs=pl.BlockSpec((ROW_TILE, D_MODEL), lambda i: (i, 0)),
        out_shape=jax.ShapeDtypeStruct((TOKENS, D_MODEL), jnp.float32),
        compiler_params=pltpu.CompilerParams(vmem_limit_bytes=VMEM_LIMIT_V7X),
        name="out_proj",
    )(*parts, w_bf16, x, gate)


def _retention_tables():
    heads = np.arange(RET_HEADS, dtype=np.float64)
    log_g = np.log1p(-np.exp2(-5.0 - heads))
    idx = np.arange(RET_BLOCK, dtype=np.float64)
    diff = idx[:, None] - idx[None, :]
    ci, cj = (idx // CHUNK)[:, None], (idx // CHUNK)[None, :]
    expo = np.where(ci == cj, np.abs(diff), diff)
    decay = np.where(cj <= ci, np.exp(log_g[:, None, None] * expo[None]), 0.0)
    q_dec = np.exp(log_g[:, None] * (idx[None, :] + 1.0))
    k_dec = np.exp(log_g[:, None] * (RET_BLOCK - 1.0 - idx[None, :]))
    blk_dec = np.exp(log_g * RET_BLOCK)
    q_dec = np.broadcast_to(q_dec[:, :, None], (RET_HEADS, RET_BLOCK, RET_DK))
    k_dec = np.broadcast_to(k_dec[:, :, None], (RET_HEADS, RET_BLOCK, RET_DK))
    blk_dec = np.broadcast_to(blk_dec[:, None, None], (RET_HEADS, 1, RET_DV))
    half = RET_DK // 2
    inv = ROPE_BASE ** (-np.arange(half, dtype=np.float64) / half)
    ang = np.arange(SEQ, dtype=np.float64)[:, None] * inv[None, :]
    cos = np.concatenate([np.cos(ang), np.cos(ang)], axis=1)
    sin = np.concatenate([-np.sin(ang), np.sin(ang)], axis=1)
    f32 = lambda a: jnp.asarray(np.ascontiguousarray(a), jnp.float32)
    return f32(decay), f32(q_dec), f32(k_dec), f32(blk_dec), f32(cos), f32(sin)


def _retention_kernel(q_ref, k_ref, v_ref, gate_ref, cos_ref, sin_ref, dec_ref, qd_ref, kd_ref,
                      bd_ref, o_ref, state_ref):
    @pl.when(pl.program_id(1) == 0)
    def _():
        state_ref[...] = jnp.zeros_like(state_ref)

    cos, sin = cos_ref[...], sin_ref[...]
    for h in range(RET_HEADS):
        ks = slice(h * RET_DK, (h + 1) * RET_DK)
        vs = slice(h * RET_DV, (h + 1) * RET_DV)
        q = q_ref[:, ks].astype(jnp.float32)
        k = k_ref[:, ks].astype(jnp.float32)
        qr = q * cos + pltpu.roll(q, RET_DK // 2, axis=1) * sin
        kr = (k * cos + pltpu.roll(k, RET_DK // 2, axis=1) * sin) * (RET_DK ** -0.5)
        v = v_ref[:, vs]

        scores = _dot_nt(_bf16(qr), _bf16(kr)) * dec_ref[h]
        intra = _dot(_bf16(scores), v)
        state = state_ref[h]
        inter = _dot(_bf16(qr * qd_ref[h]), _bf16(state))
        state_ref[h] = state * bd_ref[h] + _dot_tn(_bf16(kr * kd_ref[h]), v)

        o = intra + inter
        mu = jnp.mean(o, axis=-1, keepdims=True)
        cen = o - mu
        var = jnp.mean(cen * cen, axis=-1, keepdims=True)
        normed = cen * lax.rsqrt(var + EPS)
        o_ref[:, vs] = _bf16(_silu(gate_ref[:, vs].astype(jnp.float32)) * normed)


def _retention(proj):
    decay, q_dec, k_dec, blk_dec, cos, sin = _retention_tables()
    nblk = SEQ // RET_BLOCK
    row = lambda b, n: b * nblk + n
    whole = lambda b, n: (0, 0, 0)
    return pl.pallas_call(
        _retention_kernel,
        grid=(BATCH, nblk),
        in_specs=[
            pl.BlockSpec((RET_BLOCK, A_QK), lambda b, n: (row(b, n), 0)),
            pl.BlockSpec((RET_BLOCK, A_QK), lambda b, n: (row(b, n), 1)),
            pl.BlockSpec((RET_BLOCK, A_V), lambda b, n: (row(b, n), 2 * A_QK // A_V)),
            pl.BlockSpec((RET_BLOCK, A_V), lambda b, n: (row(b, n), 2 * A_QK // A_V + 1)),
            pl.BlockSpec((RET_BLOCK, RET_DK), lambda b, n: (n, 0)),
            pl.BlockSpec((RET_BLOCK, RET_DK), lambda b, n: (n, 0)),
            pl.BlockSpec((RET_HEADS, RET_BLOCK, RET_BLOCK), whole),
            pl.BlockSpec((RET_HEADS, RET_BLOCK, RET_DK), whole),
            pl.BlockSpec((RET_HEADS, RET_BLOCK, RET_DK), whole),
            pl.BlockSpec((RET_HEADS, 1, RET_DV), whole),
        ],
        out_specs=pl.BlockSpec((RET_BLOCK, A_V), lambda b, n: (row(b, n), 0)),
        out_shape=jax.ShapeDtypeStruct((TOKENS, A_V), jnp.bfloat16),
        scratch_shapes=[pltpu.VMEM((RET_HEADS, RET_DK, RET_DV), jnp.float32)],
        compiler_params=pltpu.CompilerParams(dimension_semantics=("arbitrary", "arbitrary")),
        name="retention",
    )(proj, proj, proj, proj, cos, sin, decay, q_dec, k_dec, blk_dec)


def _spatial_gate_kernel(u_ref, z_ref, lng_ref, lnb_ref, w_ref, b_ref, o_ref):
    u = _gelu_tanh(u_ref[...].astype(jnp.float32))
    z = _gelu_tanh(z_ref[...].astype(jnp.float32))
    mu = jnp.mean(z, axis=-1, keepdims=True)
    cen = z - mu
    var = jnp.mean(cen * cen, axis=-1, keepdims=True)
    zn = _bf16(cen * lax.rsqrt(var + EPS) * lng_ref[...] + lnb_ref[...])
    rows = lax.broadcasted_iota(jnp.int32, (SG_LEN, SG_LEN), 0)
    cols = lax.broadcasted_iota(jnp.int32, (SG_LEN, SG_LEN), 1)
    keep = (rows // CHUNK) >= (cols // CHUNK)
    for g in range(SG_GROUPS):
        cs = slice(g * SG_CH, (g + 1) * SG_CH)
        w = _bf16(jnp.where(keep, w_ref[g], 0.0))
        for blk in range(SG_TILE // SG_LEN):
            rs = slice(blk * SG_LEN, (blk + 1) * SG_LEN)
            mixed = _dot(w, zn[rs, cs]) + b_ref[g]
            o_ref[rs, cs] = _bf16(u[rs, cs] * mixed)


def _spatial_gate(proj, ln_g, ln_b, w_s, b_s):
    u_col = (2 * A_QK + 2 * A_V) // SG_WIDTH
    b_full = jnp.broadcast_to(b_s[:, :, None], (SG_GROUPS, SG_LEN, SG_CH))
    return pl.pallas_call(
        _spatial_gate_kernel,
        grid=(TOKENS // SG_TILE,),
        in_specs=[
            pl.BlockSpec((SG_TILE, SG_WIDTH), lambda i: (i, u_col)),
            pl.BlockSpec((SG_TILE, SG_WIDTH), lambda i: (i, u_col + 1)),
            pl.BlockSpec((1, SG_WIDTH), lambda i: (0, 0)),
            pl.BlockSpec((1, SG_WIDTH), lambda i: (0, 0)),
            pl.BlockSpec((SG_GROUPS, SG_LEN, SG_LEN), lambda i: (0, 0, 0)),
            pl.BlockSpec((SG_GROUPS, SG_LEN, SG_CH), lambda i: (0, 0, 0)),
        ],
        out_specs=pl.BlockSpec((SG_TILE, SG_WIDTH), lambda i: (i, 0)),
        out_shape=jax.ShapeDtypeStruct((TOKENS, SG_WIDTH), jnp.bfloat16),
        name="spatial_gate",
    )(proj, proj, ln_g.reshape(1, SG_WIDTH), ln_b.reshape(1, SG_WIDTH), w_s, b_full)


def _band_bias(rel_bias):
    period = ATT_QBLOCK + ATT_BAND
    m = np.arange(period)
    shift = np.where(m < ATT_BAND, m, m - period)
    rel = BAND_PAD - shift
    table = rel_bias[:, np.clip(rel, -(CHUNK - 1), MAX_REL) + (CHUNK - 1)].astype(jnp.float32)
    flat = jnp.tile(table, (1, ATT_QBLOCK))[:, :ATT_QBLOCK * (period - 1)]
    bias = flat.reshape(ATT_HEADS, ATT_QBLOCK, period - 1)[:, :, :ATT_BAND]
    i = np.arange(ATT_QBLOCK)[:, None]
    jj = np.arange(ATT_BAND)[None, :]
    in_band = (jj // CHUNK >= i // CHUNK) & (jj // CHUNK <= i // CHUNK + BAND_PAD // CHUNK)
    return jnp.where(jnp.asarray(in_band)[None], bias * LOG2_E, NEG_BIG)


def _head_rms(x, gain):
    head_of_lane = lax.broadcasted_iota(jnp.int32, x.shape, 1) // ATT_DH
    sq = x * x
    ms = jnp.zeros_like(x)
    for h in range(ATT_GROUP):
        in_head = head_of_lane == h
        ms = jnp.where(in_head, jnp.sum(jnp.where(in_head, sq, 0.0), axis=-1, keepdims=True), ms)
    return x * lax.rsqrt(ms * (1.0 / ATT_DH) + EPS) * gain


def _band_attn_kernel(q_ref, k_ref, v_ref, bias_ref, qg_ref, kg_ref, o_ref, kpad_ref, vpad_ref,
                      s_ref, p_ref, den_ref):
    n = pl.program_id(2)
    pair = 2 * ATT_DH

    @pl.when(n == 0)
    def _():
        kn = _bf16(_head_rms(k_ref[...].astype(jnp.float32), kg_ref[...]))
        v = v_ref[...]
        low_half = lax.broadcasted_iota(jnp.int32, (SEQ, pair), 1) < ATT_DH
        zero = jnp.zeros((SEQ, pair), jnp.bfloat16)
        for h in range(ATT_GROUP):
            cols = slice((h // 2) * pair, (h // 2 + 1) * pair)
            own = low_half if h % 2 == 0 else jnp.logical_not(low_half)
            kpad_ref[h, 0:BAND_PAD, :] = jnp.zeros((BAND_PAD, pair), jnp.bfloat16)
            vpad_ref[h, 0:BAND_PAD, :] = jnp.zeros((BAND_PAD, pair), jnp.bfloat16)
            kpad_ref[h, BAND_PAD:, :] = jnp.where(own, kn[:, cols], zero)
            vpad_ref[h, BAND_PAD:, :] = jnp.where(own, v[:, cols], zero)

    q = _bf16(_head_rms(q_ref[...].astype(jnp.float32), qg_ref[...]) * (LOG2_E * ATT_DH ** -0.5))
    start = pl.multiple_of(n * ATT_QBLOCK, ATT_QBLOCK)
    for h in range(ATT_GROUP):
        q_pair = q[:, (h // 2) * pair:(h // 2 + 1) * pair]
        s_ref[h] = _dot_nt(q_pair, kpad_ref[h, pl.ds(start, ATT_BAND), :])

    key_pos = lax.broadcasted_iota(jnp.int32, (ATT_QBLOCK, ATT_BAND), 1) + start
    valid = key_pos >= BAND_PAD
    for h in range(ATT_GROUP):
        s = jnp.where(valid, s_ref[h] + bias_ref[h], NEG_BIG)
        m = jnp.max(s, axis=-1, keepdims=True)
        p = jnp.exp2(s - m)
        den_ref[h] = jnp.sum(p, axis=-1, keepdims=True)
        p_ref[h] = _bf16(p)
    outs = []
    for h0 in range(0, ATT_GROUP, 2):
        pv = [_dot(p_ref[h], vpad_ref[h, pl.ds(start, ATT_BAND), :]) * (1.0 / den_ref[h])
              for h in (h0, h0 + 1)]
        outs.append(pv[0] + pv[1])
    o_ref[...] = _bf16(jnp.concatenate(outs, axis=-1))


def _band_attention(qkv, q_g, k_g, rel_bias):
    bias = _band_bias(rel_bias)
    nq = SEQ // ATT_QBLOCK
    width = ATT_GROUP * ATT_DH
    n_groups = ATT_HEADS // ATT_GROUP
    qg = jnp.tile(q_g, ATT_GROUP).reshape(1, width)
    kg = jnp.tile(k_g, ATT_GROUP).reshape(1, width)
    return pl.pallas_call(
        _band_attn_kernel,
        grid=(BATCH, n_groups, nq),
        in_specs=[
            pl.BlockSpec((ATT_QBLOCK, width), lambda b, hg, n: (b * nq + n, hg)),
            pl.BlockSpec((SEQ, width), lambda b, hg, n: (b, n_groups + hg)),
            pl.BlockSpec((SEQ, width), lambda b, hg, n: (b, 2 * n_groups + hg)),
            pl.BlockSpec((ATT_GROUP, ATT_QBLOCK, ATT_BAND), lambda b, hg, n: (hg, 0, 0)),
            pl.BlockSpec((1, width), lambda b, hg, n: (0, 0)),
            pl.BlockSpec((1, width), lambda b, hg, n: (0, 0)),
        ],
        out_specs=pl.BlockSpec((ATT_QBLOCK, width), lambda b, hg, n: (b * nq + n, hg)),
        out_shape=jax.ShapeDtypeStruct((TOKENS, D_MODEL), jnp.bfloat16),
        scratch_shapes=[pltpu.VMEM((ATT_GROUP, SEQ + BAND_PAD, 2 * ATT_DH), jnp.bfloat16),
                        pltpu.VMEM((ATT_GROUP, SEQ + BAND_PAD, 2 * ATT_DH), jnp.bfloat16),
                        pltpu.VMEM((ATT_GROUP, ATT_QBLOCK, ATT_BAND), jnp.float32),
                        pltpu.VMEM((ATT_GROUP, ATT_QBLOCK, ATT_BAND), jnp.bfloat16),
                        pltpu.VMEM((ATT_GROUP, ATT_QBLOCK, 1), jnp.float32)],
        compiler_params=pltpu.CompilerParams(
            dimension_semantics=("arbitrary", "arbitrary", "arbitrary")),
        name="band_attention",
    )(qkv, qkv, qkv, bias, qg, kg)


def _store_token_major(ref, value, lead=()):
    rows = value.shape[0]
    for c in range(ROW_TILES):
        ref[(*lead, pl.ds(c, rows, stride=ROW_TILES), slice(None))] = value[:, c * 128:(c + 1) * 128]


def _load_token_major(ref, rows, c, lead=()):
    return ref[(*lead, pl.ds(c, rows, stride=ROW_TILES), slice(None))]


def _split_bf16(x):
    hi = _bf16(x)
    lo = _bf16(x - hi.astype(jnp.float32))
    return hi, lo


def _router_kernel(x_ref, g_ref, sc_ref, sh_ref, rw_ref, rb_ref,
                   h_ref, idx_ref, rank_ref, gate_ref, cnt_ref, carry_ref):
    @pl.when(pl.program_id(0) == 0)
    def _():
        carry_ref[...] = jnp.zeros_like(carry_ref)

    h = _norm_mod(x_ref[...], g_ref[...], sc_ref[0], sh_ref[0])
    _store_token_major(h_ref, h)
    h_hi, h_lo = _split_bf16(h)
    w_hi, w_lo = _split_bf16(rw_ref[...])
    logits = _dot_nt(w_hi, h_hi) + _dot_nt(w_hi, h_lo) + _dot_nt(w_lo, h_hi) + rb_ref[...]

    expert = lax.broadcasted_iota(jnp.int32, logits.shape, 0).astype(jnp.float32)
    work = logits
    vals, idxs, sels = [], [], []
    for _ in range(TOP_K):
        m = jnp.max(work, axis=0, keepdims=True)
        pick = jnp.min(jnp.where(work == m, expert, float(N_EXPERTS)), axis=0, keepdims=True)
        sel = expert == pick
        work = jnp.where(sel, -jnp.inf, work)
        vals.append(m)
        idxs.append(pick)
        sels.append(sel)
    exps = [jnp.exp(v - vals[0]) for v in vals]
    denom = exps[0] + exps[1] + exps[2] + exps[3]
    gate_ref[...] = jnp.concatenate([e / denom for e in exps], axis=0)
    idx_ref[...] = jnp.concatenate(idxs, axis=0).astype(jnp.int32)

    chosen = jnp.zeros(logits.shape, jnp.float32)
    for sel in sels:
        chosen = jnp.where(sel, 1.0, chosen)
    tile = logits.shape[1]
    earlier = (lax.broadcasted_iota(jnp.int32, (tile, tile), 0)
               < lax.broadcasted_iota(jnp.int32, (tile, tile), 1))
    before = _dot(_bf16(chosen), jnp.where(earlier, 1.0, 0.0).astype(jnp.bfloat16))
    rank_full = before + carry_ref[...]
    ranks = [jnp.sum(jnp.where(sel, rank_full, 0.0), axis=0, keepdims=True) for sel in sels]
    rank_ref[...] = jnp.concatenate(ranks, axis=0).astype(jnp.int32)
    carry = carry_ref[...] + jnp.sum(chosen, axis=1, keepdims=True)
    carry_ref[...] = carry
    cnt_ref[...] = jnp.broadcast_to(carry, cnt_ref.shape)


def _router(x, g, sc, sh, router_w, router_b):
    tiles_per_batch = SEQ // ROUTE_TILE
    lane_out = lambda dt: jax.ShapeDtypeStruct((TOP_K, TOKENS), dt)
    lane_spec = pl.BlockSpec((TOP_K, ROUTE_TILE), lambda i: (0, i))
    return pl.pallas_call(
        _router_kernel,
        grid=(TOKENS // ROUTE_TILE,),
        in_specs=[
            pl.BlockSpec((ROUTE_TILE, D_MODEL), lambda i: (i, 0)),
            pl.BlockSpec((1, D_MODEL), lambda i: (0, 0)),
            pl.BlockSpec((1, 1, D_MODEL), lambda i: (i // tiles_per_batch, 0, 0)),
            pl.BlockSpec((1, 1, D_MODEL), lambda i: (i // tiles_per_batch, 0, 0)),
            pl.BlockSpec((N_EXPERTS, D_MODEL), lambda i: (0, 0)),
            pl.BlockSpec((N_EXPERTS, 1), lambda i: (0, 0)),
        ],
        out_specs=[
            pl.BlockSpec((ROUTE_TILE * ROW_TILES, 128), lambda i: (i, 0)),
            lane_spec, lane_spec, lane_spec,
            pl.BlockSpec((N_EXPERTS, 128), lambda i: (0, 0)),
        ],
        out_shape=[
            jax.ShapeDtypeStruct((TOKENS * ROW_TILES, 128), jnp.float32),
            lane_out(jnp.int32), lane_out(jnp.int32), lane_out(jnp.float32),
            jax.ShapeDtypeStruct((N_EXPERTS, 128), jnp.float32),
        ],
        scratch_shapes=[pltpu.VMEM((N_EXPERTS, 1), jnp.float32)],
        compiler_params=pltpu.CompilerParams(dimension_semantics=("arbitrary",)),
        name="moe_router",
    )(x, g.reshape(1, D_MODEL), sc, sh, router_w.T, router_b.reshape(N_EXPERTS, 1))


def _row_copy(src_ref, src_row, dst_ref, dst_row, sem):
    src = pl.multiple_of(src_row * ROW_TILES, ROW_TILES)
    dst = pl.multiple_of(dst_row * ROW_TILES, ROW_TILES)
    return pltpu.make_async_copy(src_ref.at[pl.ds(src, ROW_TILES), :],
                                 dst_ref.at[pl.ds(dst, ROW_TILES), :], sem)


def _zero_block_copy(zero_ref, xs_ref, row, sem):
    row = pl.multiple_of(row * ROW_TILES, MOE_BLOCK * ROW_TILES)
    return pltpu.make_async_copy(zero_ref, xs_ref.at[pl.ds(row, MOE_BLOCK * ROW_TILES), :], sem)


def _dispatch_kernel(zstart_ref, nused_ref, pos_ref, h_ref, xs_ref, dest_ref, zero_ref, zsem, sem):
    step = pl.program_id(0)

    @pl.when(step == 0)
    def _():
        def spare(b, carry):
            parity = jnp.where(b < MOE_NBLOCKS, b, b + 1) % 2
            base = TOKENS * TOP_K + parity * MOE_BLOCK

            def rows(j, inner):
                for u in range(8):
                    r = j * 8 + u
                    dest_ref[b * MOE_BLOCK + r] = base + r
                return inner

            lax.fori_loop(0, MOE_BLOCK // 8, rows, 0)
            return carry

        lax.fori_loop(0, DEST_SLOTS // MOE_BLOCK, spare, 0)

    @pl.when(step == 0)
    def _():
        zero_ref[...] = jnp.zeros_like(zero_ref)

        def each_block(fn):
            def expert_last(e, carry):
                @pl.when(zstart_ref[e] >= 0)
                def _():
                    fn(_zero_block_copy(zero_ref, xs_ref, zstart_ref[e], zsem))
                return carry

            def unused(blk, carry):
                fn(_zero_block_copy(zero_ref, xs_ref, blk * MOE_BLOCK, zsem))
                return carry

            lax.fori_loop(0, N_EXPERTS, expert_last, 0)
            lax.fori_loop(nused_ref[0], MOE_NBLOCKS, unused, 0)

        each_block(lambda copy: copy.start())
        each_block(lambda copy: copy.wait())

    def issue(i, carry):
        for u in range(ISSUE_UNROLL):
            t = i * ISSUE_UNROLL + u
            for k in range(TOP_K):
                slot = pos_ref[t * TOP_K + k]
                _row_copy(h_ref, t, xs_ref, slot, sem).start(priority=k % 2)
                dest_ref[slot] = k * TOKENS + step * DISPATCH_TILE + t
        return carry

    lax.fori_loop(0, DISPATCH_TILE // ISSUE_UNROLL, issue, 0)

    for _ in range(TOP_K):
        pltpu.make_async_copy(h_ref, xs_ref.at[pl.ds(0, DISPATCH_TILE * ROW_TILES), :], sem).wait()


def _dispatch(h, pos_flat, zstart, n_used):
    grid_spec = pltpu.PrefetchScalarGridSpec(
        num_scalar_prefetch=2,
        grid=(TOKENS // DISPATCH_TILE,),
        in_specs=[
            pl.BlockSpec((DISPATCH_TILE * TOP_K,), lambda i, zs, nu: (i,),
                         memory_space=pltpu.SMEM),
            pl.BlockSpec((DISPATCH_TILE * ROW_TILES, 128), lambda i, zs, nu: (i, 0)),
        ],
        out_specs=[pl.BlockSpec(memory_space=pl.ANY), pl.BlockSpec(memory_space=pltpu.SMEM)],
        scratch_shapes=[pltpu.VMEM((MOE_BLOCK * ROW_TILES, 128), jnp.float32),
                        pltpu.SemaphoreType.DMA, pltpu.SemaphoreType.DMA],
    )
    return pl.pallas_call(
        _dispatch_kernel,
        grid_spec=grid_spec,
        out_shape=[jax.ShapeDtypeStruct((MOE_CAP * ROW_TILES, 128), jnp.float32),
                   jax.ShapeDtypeStruct((DEST_SLOTS,), jnp.int32)],
        compiler_params=pltpu.CompilerParams(dimension_semantics=("arbitrary",)),
        name="moe_dispatch",
    )(zstart, n_used, pos_flat, h)


def _expert_kernel(be_ref, nused_ref, next_ref, dest_ref, warm_ref, xs_ref, b1_ref, b2_ref,
                   w1_hbm, w2_hbm, out_ref, w1f_ref, w2f_ref, w1b_ref, w2b_ref, stage_ref,
                   wsems, ssems, *, layer):
    i = pl.program_id(0)
    n_used = nused_ref[0]
    expert = be_ref[jnp.minimum(i, MOE_NBLOCKS - 1)]
    used = i < n_used
    new_expert = jnp.logical_or(i == 0, expert != be_ref[jnp.maximum(i - 1, 0)])
    stage_rows = MOE_BLOCK * ROW_TILES
    now = i % 2
    prev = 1 - now

    def weight_copies(e):
        return (pltpu.make_async_copy(w1_hbm.at[layer, e], w1f_ref, wsems.at[0]),
                pltpu.make_async_copy(w2_hbm.at[layer, e], w2f_ref, wsems.at[1]))

    def start_scatter(stage, rows_ref):
        for r in range(MOE_BLOCK):
            dst = pl.multiple_of(rows_ref[r] * ROW_TILES, ROW_TILES)
            pltpu.make_async_copy(stage_ref.at[stage, pl.ds(r * ROW_TILES, ROW_TILES), :],
                                  out_ref.at[pl.ds(dst, ROW_TILES), :],
                                  ssems.at[stage]).start(priority=r % 2)

    def wait_scatter(stage):
        pltpu.make_async_copy(stage_ref.at[stage], out_ref.at[pl.ds(0, stage_rows), :],
                              ssems.at[stage]).wait()

    @pl.when(i == 0)
    def _():
        stage_ref[...] = jnp.zeros_like(stage_ref)
        start_scatter(0, warm_ref)

    @pl.when(jnp.logical_and(used, new_expert))
    def _():
        @pl.when(i == 0)
        def _():
            for copy in weight_copies(expert):
                copy.start()

        for copy in weight_copies(expert):
            copy.wait()
        for r in range(D_MODEL // 256):
            rows = slice(r * 256, (r + 1) * 256)
            w1b_ref[rows, :] = _bf16(w1f_ref[rows, :])
            w2b_ref[rows, :] = _bf16(w2f_ref[rows, :])

        @pl.when(next_ref[i] >= 0)
        def _():
            for copy in weight_copies(next_ref[i]):
                copy.start()

    @pl.when(used)
    def _():
        wait_scatter(now)
        start_scatter(prev, dest_ref)
        x = _bf16(jnp.concatenate([_load_token_major(xs_ref, MOE_BLOCK, c)
                                   for c in range(ROW_TILES)], axis=-1))
        glu = _dot(x, w1b_ref[:, :D_FF]) + b1_ref[0][:, :D_FF]
        lin = _dot(x, w1b_ref[:, D_FF:]) + b1_ref[0][:, D_FF:]
        glu = jnp.minimum(glu, SWIGLU_LIMIT)
        lin = jnp.clip(lin, -SWIGLU_LIMIT, SWIGLU_LIMIT)
        act = glu * (1.0 / (1.0 + jnp.exp(-SWIGLU_ALPHA * glu))) * (lin + 1.0)
        _store_token_major(stage_ref, _dot(_bf16(act), w2b_ref[...]) + b2_ref[0], (now,))

    @pl.when(i == n_used)
    def _():
        start_scatter(prev, dest_ref)
        wait_scatter(now)
        wait_scatter(prev)


def _experts(xs, dest, block_e, n_used, next_e, layer, w1, b1, w2, b2):
    last = MOE_NBLOCKS - 1
    block = lambda i, be, nu, nx: jnp.minimum(i, last)
    grid_spec = pltpu.PrefetchScalarGridSpec(
        num_scalar_prefetch=3,
        grid=(MOE_NBLOCKS + 1,),
        in_specs=[
            pl.BlockSpec((MOE_BLOCK,),
                         lambda i, be, nu, nx: (jnp.where(i == 0, MOE_NBLOCKS, i - 1),),
                         memory_space=pltpu.SMEM),
            pl.BlockSpec((MOE_BLOCK,), lambda i, be, nu, nx: (MOE_NBLOCKS + 1,),
                         memory_space=pltpu.SMEM),
            pl.BlockSpec((MOE_BLOCK * ROW_TILES, 128),
                         lambda i, be, nu, nx: (jnp.minimum(i, nu[0] - 1), 0)),
            pl.BlockSpec((None, 1, 1, 2 * D_FF),
                         lambda i, be, nu, nx: (layer, be[block(i, be, nu, nx)], 0, 0)),
            pl.BlockSpec((None, 1, 1, D_MODEL),
                         lambda i, be, nu, nx: (layer, be[block(i, be, nu, nx)], 0, 0)),
            pl.BlockSpec(memory_space=pl.ANY),
            pl.BlockSpec(memory_space=pl.ANY),
        ],
        out_specs=pl.BlockSpec(memory_space=pl.ANY),
        scratch_shapes=[pltpu.VMEM((D_MODEL, 2 * D_FF), jnp.float32),
                        pltpu.VMEM((D_FF, D_MODEL), jnp.float32),
                        pltpu.VMEM((D_MODEL, 2 * D_FF), jnp.bfloat16),
                        pltpu.VMEM((D_FF, D_MODEL), jnp.bfloat16),
                        pltpu.VMEM((2, MOE_BLOCK * ROW_TILES, 128), jnp.float32),
                        pltpu.SemaphoreType.DMA((2,)),
                        pltpu.SemaphoreType.DMA((2,))],
    )
    return pl.pallas_call(
        functools.partial(_expert_kernel, layer=layer),
        grid_spec=grid_spec,
        out_shape=jax.ShapeDtypeStruct((GATHERED_ROWS * ROW_TILES, 128), jnp.float32),
        compiler_params=pltpu.CompilerParams(dimension_semantics=("arbitrary",),
                                             vmem_limit_bytes=VMEM_LIMIT_V7X),
        name="moe_experts",
    )(block_e, n_used, next_e, dest, dest, xs, b1.reshape(DEPTH, N_EXPERTS, 1, 2 * D_FF),
      b2.reshape(DEPTH, N_EXPERTS, 1, D_MODEL), w1, w2)


def _combine_kernel(x_ref, gates_ref, g2_ref, y0_ref, y1_ref, y2_ref, y3_ref, o_ref):
    gates = gates_ref[...]
    gate_cols = [jnp.broadcast_to(gates[:, k:k + 1], (DISPATCH_TILE, 128)) for k in range(TOP_K)]
    for c in range(ROW_TILES):
        cols = slice(c * 128, (c + 1) * 128)
        y = gate_cols[0] * _load_token_major(y0_ref, DISPATCH_TILE, c)
        for k, y_ref in ((1, y1_ref), (2, y2_ref), (3, y3_ref)):
            y = y + gate_cols[k] * _load_token_major(y_ref, DISPATCH_TILE, c)
        o_ref[:, cols] = x_ref[:, cols] + g2_ref[0][:, cols] * y


def _combine(x, gates_tok, g2, ys):
    tiles_per_batch = SEQ // DISPATCH_TILE
    n_steps = TOKENS // DISPATCH_TILE
    plane = lambda k: pl.BlockSpec((DISPATCH_TILE * ROW_TILES, 128),
                                   lambda i: (k * n_steps + i, 0))
    return pl.pallas_call(
        _combine_kernel,
        grid=(n_steps,),
        in_specs=[
            pl.BlockSpec((DISPATCH_TILE, D_MODEL), lambda i: (i, 0)),
            pl.BlockSpec((DISPATCH_TILE, TOP_K), lambda i: (i, 0)),
            pl.BlockSpec((1, 1, D_MODEL), lambda i: (i // tiles_per_batch, 0, 0)),
            plane(0), plane(1), plane(2), plane(3),
        ],
        out_specs=pl.BlockSpec((DISPATCH_TILE, D_MODEL), lambda i: (i, 0)),
        out_shape=jax.ShapeDtypeStruct((TOKENS, D_MODEL), jnp.float32),
        name="moe_combine",
    )(x, gates_tok, g2, ys, ys, ys, ys)


def _moe_layer(x, g, sc, sh, g2, router_w, router_b, layer, w1, b1, w2, b2):
    h, idx, rank, gates, cnt = _router(x, g, sc, sh, router_w, router_b)
    counts = cnt[:, 0].astype(jnp.int32)
    padded = ((counts + MOE_BLOCK - 1) // MOE_BLOCK) * MOE_BLOCK
    pends = jnp.cumsum(padded)
    pstarts = pends - padded
    experts = jnp.arange(N_EXPERTS, dtype=jnp.int32)[:, None, None]
    pos = rank + jnp.sum(jnp.where(idx[None] == experts, pstarts[:, None, None], 0), axis=0)
    pos_flat = pos.T.reshape(-1)
    block_start = jnp.arange(MOE_NBLOCKS, dtype=jnp.int32) * MOE_BLOCK
    block_e = jnp.minimum(jnp.sum(block_start[:, None] >= pends[None, :], axis=1),
                          N_EXPERTS - 1).astype(jnp.int32)
    n_used = (pends[-1:] // MOE_BLOCK).astype(jnp.int32)
    zstart = jnp.where(padded > 0, pends - MOE_BLOCK, -1).astype(jnp.int32)
    ids = jnp.arange(N_EXPERTS, dtype=jnp.int32)
    later_used = jnp.logical_and(ids[None, :] > ids[:, None], (padded > 0)[None, :])
    next_used = jnp.min(jnp.where(later_used, ids[None, :], N_EXPERTS), axis=1)
    next_used = jnp.where(next_used < N_EXPERTS, next_used, -1)
    next_e = jnp.sum(jnp.where(block_e[:, None] == ids[None, :], next_used[None, :], 0),
                     axis=1).astype(jnp.int32)
    xs, dest = _dispatch(h, pos_flat, zstart, n_used)
    ys = _experts(xs, dest, block_e, n_used, next_e, layer, w1, b1, w2, b2)
    return _combine(x, gates.T, g2, ys)


def kernel(x, c, ada_w, ada_b, norm_mix_g, norm_ffn_g, ev_w_in, ev_w_out, sg_ln_g, sg_ln_b, sg_w, sg_b, od_w_in, od_w_out, od_q_g, od_k_g, od_rel_bias, moe_router_w, moe_router_b, moe_w1, moe_b1, moe_w2, moe_b2):
    mod = _ada_mod(c, ada_w, ada_b)
    xt = x.reshape(TOKENS, D_MODEL)
    for l in range(DEPTH):
        sh1, sc1, g1, sh2, sc2, g2 = [m.reshape(BATCH, 1, D_MODEL)
                                      for m in jnp.split(mod[l], 6, axis=-1)]
        i = l // 2
        if l % 2 == 0:
            proj = _norm_proj(xt, norm_mix_g[l], sc1, sh1, _bf16(ev_w_in[i]))
            a_out = _retention(proj)
            b_out = _spatial_gate(proj, sg_ln_g[i], sg_ln_b[i], sg_w[i], sg_b[i])
            xt = _out_proj([a_out, b_out], _bf16(ev_w_out[i]), xt, g1)
        else:
            qkv = _norm_proj(xt, norm_mix_g[l], sc1, sh1, _bf16(od_w_in[i]))
            att = _band_attention(qkv, od_q_g[i], od_k_g[i], od_rel_bias[i])
            xt = _out_proj([att], _bf16(od_w_out[i]), xt, g1)
        xt = _moe_layer(xt, norm_ffn_g[l], sc2, sh2, g2, moe_router_w[l], moe_router_b[l],
                        l, moe_w1, moe_b1, moe_w2, moe_b2)
    return xt.reshape(BATCH, SEQ, D_MODEL)
```

```python
import functools
import math

import numpy as np
import jax
import jax.numpy as jnp
from jax import lax
from jax.experimental import pallas as pl
from jax.experimental.pallas import tpu as pltpu

D_MODEL = 1024
BATCH = 8
SEQ = 2048
DEPTH = 4
TOKENS = BATCH * SEQ
CHUNK = 64
EPS = 1e-6

RET_HEADS = 4
RET_DK = 128
RET_DV = 256
ROPE_BASE = 10000.0
A_QK = RET_HEADS * RET_DK
A_V = RET_HEADS * RET_DV
SG_GROUPS = 4
SG_CH = 128
SG_WIDTH = SG_GROUPS * SG_CH
SG_LEN = 128
EVEN_IN = 2 * A_QK + 2 * A_V + 2 * SG_WIDTH
EVEN_MIX = A_V + SG_WIDTH

ATT_HEADS = 16
ATT_DH = 64
BAND_PAD = 8 * CHUNK
MAX_REL = 256
REL_SIZE = (CHUNK - 1) + MAX_REL + 1

N_EXPERTS = 32
TOP_K = 4
D_FF = D_MODEL
SWIGLU_ALPHA = 1.702
SWIGLU_LIMIT = 7.0

ROW_TILE = 512
COL_CHUNK = 512
RET_BLOCK = 256
ATT_QBLOCK = 128
ATT_BAND = ATT_QBLOCK + BAND_PAD
ATT_GROUP = 4
ATT_LEAD = BAND_PAD // ATT_QBLOCK
SG_TILE = 512
ROUTE_TILE = 512
MOE_BLOCK = 256
MOE_CAP = TOKENS * TOP_K + N_EXPERTS * MOE_BLOCK
MOE_NBLOCKS = MOE_CAP // MOE_BLOCK
DISPATCH_TILE = 256
ISSUE_UNROLL = 4
VMEM_LIMIT_V7X = 56 * 1024 * 1024
ROW_TILES = D_MODEL // 128

NEG_BIG = -1e30
LOG2_E = math.log2(math.e)


def _silu(x):
    return x * (1.0 / (1.0 + jnp.exp(-x)))


def _gelu_tanh(x):
    return 0.5 * x * (1.0 + jnp.tanh(math.sqrt(2.0 / math.pi) * (x + 0.044715 * (x * x * x))))


def _bf16(x):
    return x.astype(jnp.bfloat16)


def _dot(a, b):
    return jnp.dot(a, b, preferred_element_type=jnp.float32)


def _dot_nt(a, b):
    return lax.dot_general(a, b, (((1,), (1,)), ((), ())), preferred_element_type=jnp.float32)


def _dot_tn(a, b):
    return lax.dot_general(a, b, (((0,), (0,)), ((), ())), preferred_element_type=jnp.float32)


def _norm_mod(x, g, sc, sh):
    y = x * lax.rsqrt(jnp.mean(x * x, axis=-1, keepdims=True) + EPS)
    return (y * g) * (1.0 + sc) + sh


def _ada_kernel(c_ref, w_ref, b_ref, o_ref):
    c_act = _silu(c_ref[...])
    o_ref[0] = _dot(c_act, w_ref[0]) + b_ref[0]


def _ada_mod(c, ada_w, ada_b):
    n_col = 6
    return pl.pallas_call(
        _ada_kernel,
        grid=(DEPTH, n_col),
        in_specs=[
            pl.BlockSpec((BATCH, D_MODEL), lambda l, j: (0, 0)),
            pl.BlockSpec((1, D_MODEL, D_MODEL), lambda l, j: (l, 0, j)),
            pl.BlockSpec((1, 1, D_MODEL), lambda l, j: (l, 0, j)),
        ],
        out_specs=pl.BlockSpec((1, BATCH, D_MODEL), lambda l, j: (l, 0, j)),
        out_shape=jax.ShapeDtypeStruct((DEPTH, BATCH, 6 * D_MODEL), jnp.float32),
        name="ada_mod",
    )(c, ada_w, ada_b.reshape(DEPTH, 1, 6 * D_MODEL))


def _norm_proj_kernel(x_ref, g_ref, sc_ref, sh_ref, w_ref, o_ref):
    h = _bf16(_norm_mod(x_ref[...], g_ref[...], sc_ref[0], sh_ref[0]))
    n_out = o_ref.shape[1]
    for j in range(n_out // COL_CHUNK):
        cols = slice(j * COL_CHUNK, (j + 1) * COL_CHUNK)
        o_ref[:, cols] = _bf16(_dot(h, w_ref[:, cols]))


def _norm_proj(x, g, sc, sh, w_bf16):
    n_out = w_bf16.shape[1]
    tiles_per_batch = SEQ // ROW_TILE
    return pl.pallas_call(
        _norm_proj_kernel,
        grid=(TOKENS // ROW_TILE,),
        in_specs=[
            pl.BlockSpec((ROW_TILE, D_MODEL), lambda i: (i, 0)),
            pl.BlockSpec((1, D_MODEL), lambda i: (0, 0)),
            pl.BlockSpec((1, 1, D_MODEL), lambda i: (i // tiles_per_batch, 0, 0)),
            pl.BlockSpec((1, 1, D_MODEL), lambda i: (i // tiles_per_batch, 0, 0)),
            pl.BlockSpec((D_MODEL, n_out), lambda i: (0, 0)),
        ],
        out_specs=pl.BlockSpec((ROW_TILE, n_out), lambda i: (i, 0)),
        out_shape=jax.ShapeDtypeStruct((TOKENS, n_out), jnp.bfloat16),
        compiler_params=pltpu.CompilerParams(vmem_limit_bytes=VMEM_LIMIT_V7X),
        name="norm_proj",
    )(x, g.reshape(1, D_MODEL), sc, sh, w_bf16)


def _out_proj_kernel(*refs, widths):
    part_refs = refs[:len(widths)]
    w_ref, x_ref, g_ref, o_ref = refs[len(widths):]
    for j in range(D_MODEL // COL_CHUNK):
        cols = slice(j * COL_CHUNK, (j + 1) * COL_CHUNK)
        acc = None
        row0 = 0
        for p_ref, width in zip(part_refs, widths):
            term = _dot(p_ref[...], w_ref[row0:row0 + width, cols])
            acc = term if acc is None else acc + term
            row0 += width
        o_ref[:, cols] = x_ref[:, cols] + g_ref[0][:, cols] * acc


def _out_proj(parts, w_bf16, x, gate):
    widths = tuple(p.shape[1] for p in parts)
    tiles_per_batch = SEQ // ROW_TILE
    in_specs = [pl.BlockSpec((ROW_TILE, width), lambda i: (i, 0)) for width in widths]
    in_specs += [
        pl.BlockSpec((sum(widths), D_MODEL), lambda i: (0, 0)),
        pl.BlockSpec((ROW_TILE, D_MODEL), lambda i: (i, 0)),
        pl.BlockSpec((1, 1, D_MODEL), lambda i: (i // tiles_per_batch, 0, 0)),
    ]
    return pl.pallas_call(
        functools.partial(_out_proj_kernel, widths=widths),
        grid=(TOKENS // ROW_TILE,),
        in_specs=in_specs,
        out_specs=pl.BlockSpec((ROW_TILE, D_MODEL), lambda i: (i, 0)),
        out_shape=jax.ShapeDtypeStruct((TOKENS, D_MODEL), jnp.float32),
        compiler_params=pltpu.CompilerParams(vmem_limit_bytes=VMEM_LIMIT_V7X),
        name="out_proj",
    )(*parts, w_bf16, x, gate)


def _retention_tables():
    heads = np.arange(RET_HEADS, dtype=np.float64)
    log_g = np.log1p(-np.exp2(-5.0 - heads))
    idx = np.arange(RET_BLOCK, dtype=np.float64)
    diff = idx[:, None] - idx[None, :]
    ci, cj = (idx // CHUNK)[:, None], (idx // CHUNK)[None, :]
    expo = np.where(ci == cj, np.abs(diff), diff)
    decay = np.where(cj <= ci, np.exp(log_g[:, None, None] * expo[None]), 0.0)
    q_dec = np.exp(log_g[:, None] * (idx[None, :] + 1.0))
    k_dec = np.exp(log_g[:, None] * (RET_BLOCK - 1.0 - idx[None, :]))
    blk_dec = np.exp(log_g * RET_BLOCK)
    q_dec = np.broadcast_to(q_dec[:, :, None], (RET_HEADS, RET_BLOCK, RET_DK))
    k_dec = np.broadcast_to(k_dec[:, :, None], (RET_HEADS, RET_BLOCK, RET_DK))
    blk_dec = np.broadcast_to(blk_dec[:, None, None], (RET_HEADS, 1, RET_DV))
    half = RET_DK // 2
    inv = ROPE_BASE ** (-np.arange(half, dtype=np.float64) / half)
    ang = np.arange(SEQ, dtype=np.float64)[:, None] * inv[None, :]
    cos = np.concatenate([np.cos(ang), np.cos(ang)], axis=1)
    sin = np.concatenate([-np.sin(ang), np.sin(ang)], axis=1)
    f32 = lambda a: jnp.asarray(np.ascontiguousarray(a), jnp.float32)
    return f32(decay), f32(q_dec), f32(k_dec), f32(blk_dec), f32(cos), f32(sin)


def _retention_kernel(q_ref, k_ref, v_ref, gate_ref, cos_ref, sin_ref, dec_ref, qd_ref, kd_ref,
                      bd_ref, o_ref, state_ref):
    @pl.when(pl.program_id(1) == 0)
    def _():
        state_ref[...] = jnp.zeros_like(state_ref)

    cos, sin = cos_ref[...], sin_ref[...]
    for h in range(RET_HEADS):
        ks = slice(h * RET_DK, (h + 1) * RET_DK)
        vs = slice(h * RET_DV, (h + 1) * RET_DV)
        q = q_ref[:, ks].astype(jnp.float32)
        k = k_ref[:, ks].astype(jnp.float32)
        qr = q * cos + pltpu.roll(q, RET_DK // 2, axis=1) * sin
        kr = (k * cos + pltpu.roll(k, RET_DK // 2, axis=1) * sin) * (RET_DK ** -0.5)
        v = v_ref[:, vs]

        scores = _dot_nt(_bf16(qr), _bf16(kr)) * dec_ref[h]
        intra = _dot(_bf16(scores), v)
        state = state_ref[h]
        inter = _dot(_bf16(qr * qd_ref[h]), _bf16(state))
        state_ref[h] = state * bd_ref[h] + _dot_tn(_bf16(kr * kd_ref[h]), v)

        o = intra + inter
        mu = jnp.mean(o, axis=-1, keepdims=True)
        cen = o - mu
        var = jnp.mean(cen * cen, axis=-1, keepdims=True)
        normed = cen * lax.rsqrt(var + EPS)
        o_ref[:, vs] = _bf16(_silu(gate_ref[:, vs].astype(jnp.float32)) * normed)


def _retention(proj):
    decay, q_dec, k_dec, blk_dec, cos, sin = _retention_tables()
    nblk = SEQ // RET_BLOCK
    row = lambda b, n: b * nblk + n
    whole = lambda b, n: (0, 0, 0)
    return pl.pallas_call(
        _retention_kernel,
        grid=(BATCH, nblk),
        in_specs=[
            pl.BlockSpec((RET_BLOCK, A_QK), lambda b, n: (row(b, n), 0)),
            pl.BlockSpec((RET_BLOCK, A_QK), lambda b, n: (row(b, n), 1)),
            pl.BlockSpec((RET_BLOCK, A_V), lambda b, n: (row(b, n), 2 * A_QK // A_V)),
            pl.BlockSpec((RET_BLOCK, A_V), lambda b, n: (row(b, n), 2 * A_QK // A_V + 1)),
            pl.BlockSpec((RET_BLOCK, RET_DK), lambda b, n: (n, 0)),
            pl.BlockSpec((RET_BLOCK, RET_DK), lambda b, n: (n, 0)),
            pl.BlockSpec((RET_HEADS, RET_BLOCK, RET_BLOCK), whole),
            pl.BlockSpec((RET_HEADS, RET_BLOCK, RET_DK), whole),
            pl.BlockSpec((RET_HEADS, RET_BLOCK, RET_DK), whole),
            pl.BlockSpec((RET_HEADS, 1, RET_DV), whole),
        ],
        out_specs=pl.BlockSpec((RET_BLOCK, A_V), lambda b, n: (row(b, n), 0)),
        out_shape=jax.ShapeDtypeStruct((TOKENS, A_V), jnp.bfloat16),
        scratch_shapes=[pltpu.VMEM((RET_HEADS, RET_DK, RET_DV), jnp.float32)],
        compiler_params=pltpu.CompilerParams(dimension_semantics=("arbitrary", "arbitrary")),
        name="retention",
    )(proj, proj, proj, proj, cos, sin, decay, q_dec, k_dec, blk_dec)


def _spatial_gate_kernel(u_ref, z_ref, lng_ref, lnb_ref, w_ref, b_ref, o_ref):
    u = _gelu_tanh(u_ref[...].astype(jnp.float32))
    z = _gelu_tanh(z_ref[...].astype(jnp.float32))
    mu = jnp.mean(z, axis=-1, keepdims=True)
    cen = z - mu
    var = jnp.mean(cen * cen, axis=-1, keepdims=True)
    zn = _bf16(cen * lax.rsqrt(var + EPS) * lng_ref[...] + lnb_ref[...])
    rows = lax.broadcasted_iota(jnp.int32, (SG_LEN, SG_LEN), 0)
    cols = lax.broadcasted_iota(jnp.int32, (SG_LEN, SG_LEN), 1)
    keep = (rows // CHUNK) >= (cols // CHUNK)
    for g in range(SG_GROUPS):
        cs = slice(g * SG_CH, (g + 1) * SG_CH)
        w = _bf16(jnp.where(keep, w_ref[g], 0.0))
        for blk in range(SG_TILE // SG_LEN):
            rs = slice(blk * SG_LEN, (blk + 1) * SG_LEN)
            mixed = _dot(w, zn[rs, cs]) + b_ref[g]
            o_ref[rs, cs] = _bf16(u[rs, cs] * mixed)


def _spatial_gate(proj, ln_g, ln_b, w_s, b_s):
    u_col = (2 * A_QK + 2 * A_V) // SG_WIDTH
    b_full = jnp.broadcast_to(b_s[:, :, None], (SG_GROUPS, SG_LEN, SG_CH))
    return pl.pallas_call(
        _spatial_gate_kernel,
        grid=(TOKENS // SG_TILE,),
        in_specs=[
            pl.BlockSpec((SG_TILE, SG_WIDTH), lambda i: (i, u_col)),
            pl.BlockSpec((SG_TILE, SG_WIDTH), lambda i: (i, u_col + 1)),
            pl.BlockSpec((1, SG_WIDTH), lambda i: (0, 0)),
            pl.BlockSpec((1, SG_WIDTH), lambda i: (0, 0)),
            pl.BlockSpec((SG_GROUPS, SG_LEN, SG_LEN), lambda i: (0, 0, 0)),
            pl.BlockSpec((SG_GROUPS, SG_LEN, SG_CH), lambda i: (0, 0, 0)),
        ],
        out_specs=pl.BlockSpec((SG_TILE, SG_WIDTH), lambda i: (i, 0)),
        out_shape=jax.ShapeDtypeStruct((TOKENS, SG_WIDTH), jnp.bfloat16),
        name="spatial_gate",
    )(proj, proj, ln_g.reshape(1, SG_WIDTH), ln_b.reshape(1, SG_WIDTH), w_s, b_full)


def _band_bias(rel_bias):
    period = ATT_QBLOCK + ATT_BAND
    m = np.arange(period)
    shift = np.where(m < ATT_BAND, m, m - period)
    rel = BAND_PAD - shift
    table = rel_bias[:, np.clip(rel, -(CHUNK - 1), MAX_REL) + (CHUNK - 1)].astype(jnp.float32)
    flat = jnp.tile(table, (1, ATT_QBLOCK))[:, :ATT_QBLOCK * (period - 1)]
    bias = flat.reshape(ATT_HEADS, ATT_QBLOCK, period - 1)[:, :, :ATT_BAND]
    i = np.arange(ATT_QBLOCK)[:, None]
    jj = np.arange(ATT_BAND)[None, :]
    in_band = (jj // CHUNK >= i // CHUNK) & (jj // CHUNK <= i // CHUNK + BAND_PAD // CHUNK)
    block = np.arange(ATT_LEAD + 1)[:, None, None]
    in_seq = (jj[None] + block * ATT_QBLOCK >= BAND_PAD) | (block == ATT_LEAD)
    keep = in_band[None] & in_seq
    return jnp.where(jnp.asarray(keep)[:, None], (bias * LOG2_E)[None], NEG_BIG)


def _head_rms(x, gain):
    head_of_lane = lax.broadcasted_iota(jnp.int32, x.shape, 1) // ATT_DH
    sq = x * x
    ms = jnp.zeros_like(x)
    for h in range(ATT_GROUP):
        in_head = head_of_lane == h
        ms = jnp.where(in_head, jnp.sum(jnp.where(in_head, sq, 0.0), axis=-1, keepdims=True), ms)
    return x * lax.rsqrt(ms * (1.0 / ATT_DH) + EPS) * gain


def _head_rms_mxu(x, gain, same_head):
    sq = x * x
    hi = _bf16(sq)
    lo = _bf16(sq - hi.astype(jnp.float32))
    ss = _dot(hi, same_head) + _dot(lo, same_head)
    return x * lax.rsqrt(ss * (1.0 / ATT_DH) + EPS) * gain


def _band_attn_kernel(q_ref, k_ref, v_ref, bias_ref, qg_ref, kg_ref, seg_ref, o_ref,
                      kpad_ref, vpad_ref, s_ref, p_ref, den_ref):
    n = pl.program_id(2)
    pair = 2 * ATT_DH

    @pl.when(n == 0)
    def _():
        kn = _bf16(_head_rms_mxu(k_ref[...].astype(jnp.float32), kg_ref[...], seg_ref[...]))
        v = v_ref[...]
        low_half = lax.broadcasted_iota(jnp.int32, (SEQ, pair), 1) < ATT_DH
        zero = jnp.zeros((SEQ, pair), jnp.bfloat16)
        for h in range(ATT_GROUP):
            cols = slice((h // 2) * pair, (h // 2 + 1) * pair)
            own = low_half if h % 2 == 0 else jnp.logical_not(low_half)
            kpad_ref[h, 0:BAND_PAD, :] = jnp.zeros((BAND_PAD, pair), jnp.bfloat16)
            vpad_ref[h, 0:BAND_PAD, :] = jnp.zeros((BAND_PAD, pair), jnp.bfloat16)
            kpad_ref[h, BAND_PAD:, :] = jnp.where(own, kn[:, cols], zero)
            vpad_ref[h, BAND_PAD:, :] = jnp.where(own, v[:, cols], zero)

    q = _bf16(_head_rms(q_ref[...].astype(jnp.float32), qg_ref[...]) * (LOG2_E * ATT_DH ** -0.5))
    start = pl.multiple_of(n * ATT_QBLOCK, ATT_QBLOCK)
    for h in range(ATT_GROUP):
        q_pair = q[:, (h // 2) * pair:(h // 2 + 1) * pair]
        s_ref[h] = _dot_nt(q_pair, kpad_ref[h, pl.ds(start, ATT_BAND), :])

    for h in range(ATT_GROUP):
        s = s_ref[h] + bias_ref[h]
        m = jnp.max(s, axis=-1, keepdims=True)
        p = jnp.exp2(s - m)
        den_ref[h] = jnp.sum(p, axis=-1, keepdims=True)
        p_ref[h] = _bf16(p)
    outs = []
    for h0 in range(0, ATT_GROUP, 2):
        pv = [_dot(p_ref[h], vpad_ref[h, pl.ds(start, ATT_BAND), :]) * (1.0 / den_ref[h])
              for h in (h0, h0 + 1)]
        outs.append(pv[0] + pv[1])
    o_ref[...] = _bf16(jnp.concatenate(outs, axis=-1))


def _band_attention(qkv, q_g, k_g, rel_bias):
    bias = _band_bias(rel_bias)
    nq = SEQ // ATT_QBLOCK
    width = ATT_GROUP * ATT_DH
    n_groups = ATT_HEADS // ATT_GROUP
    qg = jnp.tile(q_g, ATT_GROUP).reshape(1, width)
    kg = jnp.tile(k_g, ATT_GROUP).reshape(1, width)
    lane_head = np.arange(width) // ATT_DH
    same_head = jnp.asarray(lane_head[:, None] == lane_head[None, :], jnp.bfloat16)
    return pl.pallas_call(
        _band_attn_kernel,
        grid=(BATCH, n_groups, nq),
        in_specs=[
            pl.BlockSpec((ATT_QBLOCK, width), lambda b, hg, n: (b * nq + n, hg)),
            pl.BlockSpec((SEQ, width), lambda b, hg, n: (b, n_groups + hg)),
            pl.BlockSpec((SEQ, width), lambda b, hg, n: (b, 2 * n_groups + hg)),
            pl.BlockSpec((None, ATT_GROUP, ATT_QBLOCK, ATT_BAND),
                         lambda b, hg, n: (jnp.minimum(n, ATT_LEAD), hg, 0, 0)),
            pl.BlockSpec((1, width), lambda b, hg, n: (0, 0)),
            pl.BlockSpec((1, width), lambda b, hg, n: (0, 0)),
            pl.BlockSpec((width, width), lambda b, hg, n: (0, 0)),
        ],
        out_specs=pl.BlockSpec((ATT_QBLOCK, width), lambda b, hg, n: (b * nq + n, hg)),
        out_shape=jax.ShapeDtypeStruct((TOKENS, D_MODEL), jnp.bfloat16),
        scratch_shapes=[pltpu.VMEM((ATT_GROUP, SEQ + BAND_PAD, 2 * ATT_DH), jnp.bfloat16),
                        pltpu.VMEM((ATT_GROUP, SEQ + BAND_PAD, 2 * ATT_DH), jnp.bfloat16),
                        pltpu.VMEM((ATT_GROUP, ATT_QBLOCK, ATT_BAND), jnp.float32),
                        pltpu.VMEM((ATT_GROUP, ATT_QBLOCK, ATT_BAND), jnp.bfloat16),
                        pltpu.VMEM((ATT_GROUP, ATT_QBLOCK, 1), jnp.float32)],
        compiler_params=pltpu.CompilerParams(
            dimension_semantics=("arbitrary", "arbitrary", "arbitrary")),
        name="band_attention",
    )(qkv, qkv, qkv, bias, qg, kg, same_head)


def _store_token_major(ref, value, lead=()):
    rows = value.shape[0]
    for c in range(ROW_TILES):
        ref[(*lead, pl.ds(c, rows, stride=ROW_TILES), slice(None))] = value[:, c * 128:(c + 1) * 128]


def _load_token_major(ref, rows, c, lead=()):
    return ref[(*lead, pl.ds(c, rows, stride=ROW_TILES), slice(None))]


def _split_bf16(x):
    hi = _bf16(x)
    lo = _bf16(x - hi.astype(jnp.float32))
    return hi, lo


def _router_kernel(x_ref, g_ref, sc_ref, sh_ref, rw_ref, rb_ref,
                   h_ref, idx_ref, rank_ref, gate_ref, cnt_ref, carry_ref):
    @pl.when(pl.program_id(0) == 0)
    def _():
        carry_ref[...] = jnp.zeros_like(carry_ref)

    h = _norm_mod(x_ref[...], g_ref[...], sc_ref[0], sh_ref[0])
    _store_token_major(h_ref, h)
    h_hi, h_lo = _split_bf16(h)
    w_hi, w_lo = _split_bf16(rw_ref[...])
    logits = _dot_nt(w_hi, h_hi) + _dot_nt(w_hi, h_lo) + _dot_nt(w_lo, h_hi) + rb_ref[...]

    expert = lax.broadcasted_iota(jnp.int32, logits.shape, 0).astype(jnp.float32)
    work = logits
    vals, idxs, sels = [], [], []
    for _ in range(TOP_K):
        m = jnp.max(work, axis=0, keepdims=True)
        pick = jnp.min(jnp.where(work == m, expert, float(N_EXPERTS)), axis=0, keepdims=True)
        sel = expert == pick
        work = jnp.where(sel, -jnp.inf, work)
        vals.append(m)
        idxs.append(pick)
        sels.append(sel)
    exps = [jnp.exp(v - vals[0]) for v in vals]
    denom = exps[0] + exps[1] + exps[2] + exps[3]
    gate_ref[...] = jnp.concatenate([e / denom for e in exps], axis=0)
    idx_ref[...] = jnp.concatenate(idxs, axis=0).astype(jnp.int32)

    chosen = jnp.zeros(logits.shape, jnp.float32)
    for sel in sels:
        chosen = jnp.where(sel, 1.0, chosen)
    tile = logits.shape[1]
    earlier = (lax.broadcasted_iota(jnp.int32, (tile, tile), 0)
               < lax.broadcasted_iota(jnp.int32, (tile, tile), 1))
    before = _dot(_bf16(chosen), jnp.where(earlier, 1.0, 0.0).astype(jnp.bfloat16))
    rank_full = before + carry_ref[...]
    ranks = [jnp.sum(jnp.where(sel, rank_full, 0.0), axis=0, keepdims=True) for sel in sels]
    rank_ref[...] = jnp.concatenate(ranks, axis=0).astype(jnp.int32)
    carry = carry_ref[...] + jnp.sum(chosen, axis=1, keepdims=True)
    carry_ref[...] = carry
    cnt_ref[...] = jnp.broadcast_to(carry, cnt_ref.shape)


def _router(x, g, sc, sh, router_w, router_b):
    tiles_per_batch = SEQ // ROUTE_TILE
    lane_out = lambda dt: jax.ShapeDtypeStruct((TOP_K, TOKENS), dt)
    lane_spec = pl.BlockSpec((TOP_K, ROUTE_TILE), lambda i: (0, i))
    return pl.pallas_call(
        _router_kernel,
        grid=(TOKENS // ROUTE_TILE,),
        in_specs=[
            pl.BlockSpec((ROUTE_TILE, D_MODEL), lambda i: (i, 0)),
            pl.BlockSpec((1, D_MODEL), lambda i: (0, 0)),
            pl.BlockSpec((1, 1, D_MODEL), lambda i: (i // tiles_per_batch, 0, 0)),
            pl.BlockSpec((1, 1, D_MODEL), lambda i: (i // tiles_per_batch, 0, 0)),
            pl.BlockSpec((N_EXPERTS, D_MODEL), lambda i: (0, 0)),
            pl.BlockSpec((N_EXPERTS, 1), lambda i: (0, 0)),
        ],
        out_specs=[
            pl.BlockSpec((ROUTE_TILE * ROW_TILES, 128), lambda i: (i, 0)),
            lane_spec, lane_spec, lane_spec,
            pl.BlockSpec((N_EXPERTS, 128), lambda i: (0, 0)),
        ],
        out_shape=[
            jax.ShapeDtypeStruct((TOKENS * ROW_TILES, 128), jnp.float32),
            lane_out(jnp.int32), lane_out(jnp.int32), lane_out(jnp.float32),
            jax.ShapeDtypeStruct((N_EXPERTS, 128), jnp.float32),
        ],
        scratch_shapes=[pltpu.VMEM((N_EXPERTS, 1), jnp.float32)],
        compiler_params=pltpu.CompilerParams(dimension_semantics=("arbitrary",)),
        name="moe_router",
    )(x, g.reshape(1, D_MODEL), sc, sh, router_w.T, router_b.reshape(N_EXPERTS, 1))


def _row_copy(src_ref, src_row, dst_ref, dst_row, sem):
    src = pl.multiple_of(src_row * ROW_TILES, ROW_TILES)
    dst = pl.multiple_of(dst_row * ROW_TILES, ROW_TILES)
    return pltpu.make_async_copy(src_ref.at[pl.ds(src, ROW_TILES), :],
                                 dst_ref.at[pl.ds(dst, ROW_TILES), :], sem)


def _zero_block_copy(zero_ref, xs_ref, row, sem):
    row = pl.multiple_of(row * ROW_TILES, MOE_BLOCK * ROW_TILES)
    return pltpu.make_async_copy(zero_ref, xs_ref.at[pl.ds(row, MOE_BLOCK * ROW_TILES), :], sem)


def _dispatch_kernel(zstart_ref, nused_ref, pos_ref, h_ref, xs_ref, zero_ref, zsem, sem):
    step = pl.program_id(0)

    @pl.when(step == 0)
    def _():
        zero_ref[...] = jnp.zeros_like(zero_ref)

        def each_block(fn):
            def expert_last(e, carry):
                @pl.when(zstart_ref[e] >= 0)
                def _():
                    fn(_zero_block_copy(zero_ref, xs_ref, zstart_ref[e], zsem))
                return carry

            def unused(blk, carry):
                fn(_zero_block_copy(zero_ref, xs_ref, blk * MOE_BLOCK, zsem))
                return carry

            lax.fori_loop(0, N_EXPERTS, expert_last, 0)
            lax.fori_loop(nused_ref[0], MOE_NBLOCKS, unused, 0)

        each_block(lambda copy: copy.start())
        each_block(lambda copy: copy.wait())

    def issue(i, carry):
        for u in range(ISSUE_UNROLL):
            t = i * ISSUE_UNROLL + u
            for k in range(TOP_K):
                _row_copy(h_ref, t, xs_ref, pos_ref[t * TOP_K + k], sem).start(priority=k % 2)
        return carry

    lax.fori_loop(0, DISPATCH_TILE // ISSUE_UNROLL, issue, 0)

    for _ in range(TOP_K):
        pltpu.make_async_copy(h_ref, xs_ref.at[pl.ds(0, DISPATCH_TILE * ROW_TILES), :], sem).wait()


def _dispatch(h, pos_flat, zstart, n_used):
    grid_spec = pltpu.PrefetchScalarGridSpec(
        num_scalar_prefetch=2,
        grid=(TOKENS // DISPATCH_TILE,),
        in_specs=[
            pl.BlockSpec((DISPATCH_TILE * TOP_K,), lambda i, zs, nu: (i,),
                         memory_space=pltpu.SMEM),
            pl.BlockSpec((DISPATCH_TILE * ROW_TILES, 128), lambda i, zs, nu: (i, 0)),
        ],
        out_specs=pl.BlockSpec(memory_space=pl.ANY),
        scratch_shapes=[pltpu.VMEM((MOE_BLOCK * ROW_TILES, 128), jnp.float32),
                        pltpu.SemaphoreType.DMA, pltpu.SemaphoreType.DMA],
    )
    return pl.pallas_call(
        _dispatch_kernel,
        grid_spec=grid_spec,
        out_shape=jax.ShapeDtypeStruct((MOE_CAP * ROW_TILES, 128), jnp.float32),
        compiler_params=pltpu.CompilerParams(dimension_semantics=("arbitrary",)),
        name="moe_dispatch",
    )(zstart, n_used, pos_flat, h)


def _expert_kernel(be_ref, nused_ref, next_ref, xs_ref, b1_ref, b2_ref, w1_hbm, w2_hbm, ys_ref,
                   w1f_ref, w2f_ref, w1b_ref, w2b_ref, sems, *, layer):
    i = pl.program_id(0)
    expert = be_ref[i]
    used = i < nused_ref[0]
    new_expert = jnp.logical_or(i == 0, expert != be_ref[jnp.maximum(i - 1, 0)])

    def weight_copies(e):
        return (pltpu.make_async_copy(w1_hbm.at[layer, e], w1f_ref, sems.at[0]),
                pltpu.make_async_copy(w2_hbm.at[layer, e], w2f_ref, sems.at[1]))

    @pl.when(jnp.logical_and(used, new_expert))
    def _():
        @pl.when(i == 0)
        def _():
            for copy in weight_copies(expert):
                copy.start()

        for copy in weight_copies(expert):
            copy.wait()
        for r in range(D_MODEL // 256):
            rows = slice(r * 256, (r + 1) * 256)
            w1b_ref[rows, :] = _bf16(w1f_ref[rows, :])
            w2b_ref[rows, :] = _bf16(w2f_ref[rows, :])

        @pl.when(next_ref[i] >= 0)
        def _():
            for copy in weight_copies(next_ref[i]):
                copy.start()

    @pl.when(used)
    def _():
        x = _bf16(jnp.concatenate([_load_token_major(xs_ref, MOE_BLOCK, c)
                                   for c in range(ROW_TILES)], axis=-1))
        glu = _dot(x, w1b_ref[:, :D_FF]) + b1_ref[0][:, :D_FF]
        lin = _dot(x, w1b_ref[:, D_FF:]) + b1_ref[0][:, D_FF:]
        glu = jnp.minimum(glu, SWIGLU_LIMIT)
        lin = jnp.clip(lin, -SWIGLU_LIMIT, SWIGLU_LIMIT)
        act = glu * (1.0 / (1.0 + jnp.exp(-SWIGLU_ALPHA * glu))) * (lin + 1.0)
        _store_token_major(ys_ref, _dot(_bf16(act), w2b_ref[...]) + b2_ref[0])

    @pl.when(jnp.logical_not(used))
    def _():
        ys_ref[...] = jnp.zeros_like(ys_ref)


def _experts(xs, block_e, n_used, next_e, layer, w1, b1, w2, b2):
    grid_spec = pltpu.PrefetchScalarGridSpec(
        num_scalar_prefetch=3,
        grid=(MOE_NBLOCKS,),
        in_specs=[
            pl.BlockSpec((MOE_BLOCK * ROW_TILES, 128),
                         lambda i, be, nu, nx: (jnp.minimum(i, nu[0] - 1), 0)),
            pl.BlockSpec((None, 1, 1, 2 * D_FF), lambda i, be, nu, nx: (layer, be[i], 0, 0)),
            pl.BlockSpec((None, 1, 1, D_MODEL), lambda i, be, nu, nx: (layer, be[i], 0, 0)),
            pl.BlockSpec(memory_space=pl.ANY),
            pl.BlockSpec(memory_space=pl.ANY),
        ],
        out_specs=pl.BlockSpec((MOE_BLOCK * ROW_TILES, 128), lambda i, be, nu, nx: (i, 0)),
        scratch_shapes=[pltpu.VMEM((D_MODEL, 2 * D_FF), jnp.float32),
                        pltpu.VMEM((D_FF, D_MODEL), jnp.float32),
                        pltpu.VMEM((D_MODEL, 2 * D_FF), jnp.bfloat16),
                        pltpu.VMEM((D_FF, D_MODEL), jnp.bfloat16),
                        pltpu.SemaphoreType.DMA((2,))],
    )
    return pl.pallas_call(
        functools.partial(_expert_kernel, layer=layer),
        grid_spec=grid_spec,
        out_shape=jax.ShapeDtypeStruct((MOE_CAP * ROW_TILES, 128), jnp.float32),
        compiler_params=pltpu.CompilerParams(dimension_semantics=("arbitrary",),
                                             vmem_limit_bytes=VMEM_LIMIT_V7X),
        name="moe_experts",
    )(block_e, n_used, next_e, xs, b1.reshape(DEPTH, N_EXPERTS, 1, 2 * D_FF),
      b2.reshape(DEPTH, N_EXPERTS, 1, D_MODEL), w1, w2)


def _combine_kernel(pos_ref, pos_next_ref, x_ref, gates_ref, g2_ref, ys_ref, o_ref, buf_ref, sems):
    step = pl.program_id(0)
    n_steps = pl.num_programs(0)
    slot = step % 2

    def gather(p_ref, dst_slot):
        def issue(i, carry):
            for u in range(ISSUE_UNROLL):
                t = i * ISSUE_UNROLL + u
                for k in range(TOP_K):
                    _row_copy(ys_ref, p_ref[t * TOP_K + k], buf_ref.at[dst_slot, k], t,
                              sems.at[dst_slot]).start(priority=k % 2)
            return carry

        lax.fori_loop(0, DISPATCH_TILE // ISSUE_UNROLL, issue, 0)

    @pl.when(step == 0)
    def _():
        gather(pos_ref, 0)

    @pl.when(step + 1 < n_steps)
    def _():
        gather(pos_next_ref, 1 - slot)

    for k in range(TOP_K):
        pltpu.make_async_copy(ys_ref.at[pl.ds(0, DISPATCH_TILE * ROW_TILES), :],
                              buf_ref.at[slot, k], sems.at[slot]).wait()

    gates = gates_ref[...]
    gate_cols = [jnp.broadcast_to(gates[:, k:k + 1], (DISPATCH_TILE, 128)) for k in range(TOP_K)]
    for c in range(ROW_TILES):
        cols = slice(c * 128, (c + 1) * 128)
        y = gate_cols[0] * _load_token_major(buf_ref, DISPATCH_TILE, c, (slot, 0))
        for k in range(1, TOP_K):
            y = y + gate_cols[k] * _load_token_major(buf_ref, DISPATCH_TILE, c, (slot, k))
        o_ref[:, cols] = x_ref[:, cols] + g2_ref[0][:, cols] * y


def _combine(x, pos_flat, gates_tok, g2, ys):
    tiles_per_batch = SEQ // DISPATCH_TILE
    n_steps = TOKENS // DISPATCH_TILE
    return pl.pallas_call(
        _combine_kernel,
        grid=(n_steps,),
        in_specs=[
            pl.BlockSpec((DISPATCH_TILE * TOP_K,), lambda i: (i,), memory_space=pltpu.SMEM),
            pl.BlockSpec((DISPATCH_TILE * TOP_K,), lambda i: (jnp.minimum(i + 1, n_steps - 1),),
                         memory_space=pltpu.SMEM),
            pl.BlockSpec((DISPATCH_TILE, D_MODEL), lambda i: (i, 0)),
            pl.BlockSpec((DISPATCH_TILE, TOP_K), lambda i: (i, 0)),
            pl.BlockSpec((1, 1, D_MODEL), lambda i: (i // tiles_per_batch, 0, 0)),
            pl.BlockSpec(memory_space=pl.ANY),
        ],
        out_specs=pl.BlockSpec((DISPATCH_TILE, D_MODEL), lambda i: (i, 0)),
        out_shape=jax.ShapeDtypeStruct((TOKENS, D_MODEL), jnp.float32),
        scratch_shapes=[pltpu.VMEM((2, TOP_K, DISPATCH_TILE * ROW_TILES, 128), jnp.float32),
                        pltpu.SemaphoreType.DMA((2,))],
        compiler_params=pltpu.CompilerParams(dimension_semantics=("arbitrary",),
                                             vmem_limit_bytes=VMEM_LIMIT_V7X),
        name="moe_combine",
    )(pos_flat, pos_flat, x, gates_tok, g2, ys)


def _moe_layer(x, g, sc, sh, g2, router_w, router_b, layer, w1, b1, w2, b2):
    h, idx, rank, gates, cnt = _router(x, g, sc, sh, router_w, router_b)
    counts = cnt[:, 0].astype(jnp.int32)
    padded = ((counts + MOE_BLOCK - 1) // MOE_BLOCK) * MOE_BLOCK
    pends = jnp.cumsum(padded)
    pstarts = pends - padded
    experts = jnp.arange(N_EXPERTS, dtype=jnp.int32)[:, None, None]
    pos = rank + jnp.sum(jnp.where(idx[None] == experts, pstarts[:, None, None], 0), axis=0)
    pos_flat = pos.T.reshape(-1)
    block_start = jnp.arange(MOE_NBLOCKS, dtype=jnp.int32) * MOE_BLOCK
    block_e = jnp.minimum(jnp.sum(block_start[:, None] >= pends[None, :], axis=1),
                          N_EXPERTS - 1).astype(jnp.int32)
    n_used = (pends[-1:] // MOE_BLOCK).astype(jnp.int32)
    zstart = jnp.where(padded > 0, pends - MOE_BLOCK, -1).astype(jnp.int32)
    ids = jnp.arange(N_EXPERTS, dtype=jnp.int32)
    later_used = jnp.logical_and(ids[None, :] > ids[:, None], (padded > 0)[None, :])
    next_used = jnp.min(jnp.where(later_used, ids[None, :], N_EXPERTS), axis=1)
    next_used = jnp.where(next_used < N_EXPERTS, next_used, -1)
    next_e = jnp.sum(jnp.where(block_e[:, None] == ids[None, :], next_used[None, :], 0),
                     axis=1).astype(jnp.int32)
    xs = _dispatch(h, pos_flat, zstart, n_used)
    ys = _experts(xs, block_e, n_used, next_e, layer, w1, b1, w2, b2)
    return _combine(x, pos_flat, gates.T, g2, ys)


def kernel(x, c, ada_w, ada_b, norm_mix_g, norm_ffn_g, ev_w_in, ev_w_out, sg_ln_g, sg_ln_b, sg_w, sg_b, od_w_in, od_w_out, od_q_g, od_k_g, od_rel_bias, moe_router_w, moe_router_b, moe_w1, moe_b1, moe_w2, moe_b2):
    mod = _ada_mod(c, ada_w, ada_b)
    xt = x.reshape(TOKENS, D_MODEL)
    for l in range(DEPTH):
        sh1, sc1, g1, sh2, sc2, g2 = [m.reshape(BATCH, 1, D_MODEL)
                                      for m in jnp.split(mod[l], 6, axis=-1)]
        i = l // 2
        if l % 2 == 0:
            proj = _norm_proj(xt, norm_mix_g[l], sc1, sh1, _bf16(ev_w_in[i]))
            a_out = _retention(proj)
            b_out = _spatial_gate(proj, sg_ln_g[i], sg_ln_b[i], sg_w[i], sg_b[i])
            xt = _out_proj([a_out, b_out], _bf16(ev_w_out[i]), xt, g1)
        else:
            qkv = _norm_proj(xt, norm_mix_g[l], sc1, sh1, _bf16(od_w_in[i]))
            att = _band_attention(qkv, od_q_g[i], od_k_g[i], od_rel_bias[i])
            xt = _out_proj([att], _bf16(od_w_out[i]), xt, g1)
        xt = _moe_layer(xt, norm_ffn_g[l], sc2, sh2, g2, moe_router_w[l], moe_router_b[l],
                        l, moe_w1, moe_b1, moe_w2, moe_b2)
    return xt.reshape(BATCH, SEQ, D_MODEL)
```

```python
import functools
import math

import numpy as np
import jax
import jax.numpy as jnp
from jax import lax
from jax.experimental import pallas as pl
from jax.experimental.pallas import tpu as pltpu

D_MODEL = 1024
BATCH = 8
SEQ = 2048
DEPTH = 4
TOKENS = BATCH * SEQ
CHUNK = 64
EPS = 1e-6

RET_HEADS = 4
RET_DK = 128
RET_DV = 256
ROPE_BASE = 10000.0
A_QK = RET_HEADS * RET_DK
A_V = RET_HEADS * RET_DV
SG_GROUPS = 4
SG_CH = 128
SG_WIDTH = SG_GROUPS * SG_CH
SG_LEN = 128
EVEN_IN = 2 * A_QK + 2 * A_V + 2 * SG_WIDTH
EVEN_MIX = A_V + SG_WIDTH

ATT_HEADS = 16
ATT_DH = 64
BAND_PAD = 8 * CHUNK
MAX_REL = 256
REL_SIZE = (CHUNK - 1) + MAX_REL + 1

N_EXPERTS = 32
TOP_K = 4
D_FF = D_MODEL
SWIGLU_ALPHA = 1.702
SWIGLU_LIMIT = 7.0

ROW_TILE = 512
COL_CHUNK = 512
RET_BLOCK = 256
ATT_QBLOCK = 128
ATT_BAND = ATT_QBLOCK + BAND_PAD
ATT_GROUP = 8
ATT_LEAD = BAND_PAD // ATT_QBLOCK
KV_PREP_ROWS = 256
SEG_LANES = 256
SG_TILE = 512
ROUTE_TILE = 512
MOE_BLOCK = 256
MOE_CAP = TOKENS * TOP_K + N_EXPERTS * MOE_BLOCK
MOE_NBLOCKS = MOE_CAP // MOE_BLOCK
DISPATCH_TILE = 512
ISSUE_UNROLL = 4
VMEM_LIMIT_V7X = 56 * 1024 * 1024
ROW_TILES = D_MODEL // 128

NEG_BIG = -1e30
LOG2_E = math.log2(math.e)


def _silu(x):
    return x * (1.0 / (1.0 + jnp.exp(-x)))


def _gelu_tanh(x):
    return 0.5 * x * (1.0 + jnp.tanh(math.sqrt(2.0 / math.pi) * (x + 0.044715 * (x * x * x))))


def _bf16(x):
    return x.astype(jnp.bfloat16)


def _dot(a, b):
    return jnp.dot(a, b, preferred_element_type=jnp.float32)


def _dot_nt(a, b):
    return lax.dot_general(a, b, (((1,), (1,)), ((), ())), preferred_element_type=jnp.float32)


def _dot_tn(a, b):
    return lax.dot_general(a, b, (((0,), (0,)), ((), ())), preferred_element_type=jnp.float32)


def _norm_mod(x, g, sc, sh):
    y = x * lax.rsqrt(jnp.mean(x * x, axis=-1, keepdims=True) + EPS)
    return (y * g) * (1.0 + sc) + sh


def _ada_kernel(c_ref, w_ref, b_ref, o_ref):
    c_act = _silu(c_ref[...])
    o_ref[0] = _dot(c_act, w_ref[0]) + b_ref[0]


def _ada_mod(c, ada_w, ada_b):
    n_col = 6
    return pl.pallas_call(
        _ada_kernel,
        grid=(DEPTH, n_col),
        in_specs=[
            pl.BlockSpec((BATCH, D_MODEL), lambda l, j: (0, 0)),
            pl.BlockSpec((1, D_MODEL, D_MODEL), lambda l, j: (l, 0, j)),
            pl.BlockSpec((1, 1, D_MODEL), lambda l, j: (l, 0, j)),
        ],
        out_specs=pl.BlockSpec((1, BATCH, D_MODEL), lambda l, j: (l, 0, j)),
        out_shape=jax.ShapeDtypeStruct((DEPTH, BATCH, 6 * D_MODEL), jnp.float32),
        name="ada_mod",
    )(c, ada_w, ada_b.reshape(DEPTH, 1, 6 * D_MODEL))


def _norm_proj_kernel(x_ref, g_ref, sc_ref, sh_ref, w_ref, o_ref):
    h = _bf16(_norm_mod(x_ref[...], g_ref[...], sc_ref[0], sh_ref[0]))
    n_out = o_ref.shape[1]
    for j in range(n_out // COL_CHUNK):
        cols = slice(j * COL_CHUNK, (j + 1) * COL_CHUNK)
        o_ref[:, cols] = _bf16(_dot(h, w_ref[:, cols]))


def _norm_proj(x, g, sc, sh, w_bf16):
    n_out = w_bf16.shape[1]
    tiles_per_batch = SEQ // ROW_TILE
    return pl.pallas_call(
        _norm_proj_kernel,
        grid=(TOKENS // ROW_TILE,),
        in_specs=[
            pl.BlockSpec((ROW_TILE, D_MODEL), lambda i: (i, 0)),
            pl.BlockSpec((1, D_MODEL), lambda i: (0, 0)),
            pl.BlockSpec((1, 1, D_MODEL), lambda i: (i // tiles_per_batch, 0, 0)),
            pl.BlockSpec((1, 1, D_MODEL), lambda i: (i // tiles_per_batch, 0, 0)),
            pl.BlockSpec((D_MODEL, n_out), lambda i: (0, 0)),
        ],
        out_specs=pl.BlockSpec((ROW_TILE, n_out), lambda i: (i, 0)),
        out_shape=jax.ShapeDtypeStruct((TOKENS, n_out), jnp.bfloat16),
        compiler_params=pltpu.CompilerParams(vmem_limit_bytes=VMEM_LIMIT_V7X),
        name="norm_proj",
    )(x, g.reshape(1, D_MODEL), sc, sh, w_bf16)


def _out_proj_kernel(*refs, widths):
    part_refs = refs[:len(widths)]
    w_ref, x_ref, g_ref, o_ref = refs[len(widths):]
    for j in range(D_MODEL // COL_CHUNK):
        cols = slice(j * COL_CHUNK, (j + 1) * COL_CHUNK)
        acc = None
        row0 = 0
        for p_ref, width in zip(part_refs, widths):
            term = _dot(p_ref[...], w_ref[row0:row0 + width, cols])
            acc = term if acc is None else acc + term
            row0 += width
        o_ref[:, cols] = x_ref[:, cols] + g_ref[0][:, cols] * acc


def _out_proj(parts, w_bf16, x, gate):
    widths = tuple(p.shape[1] for p in parts)
    tiles_per_batch = SEQ // ROW_TILE
    in_specs = [pl.BlockSpec((ROW_TILE, width), lambda i: (i, 0)) for width in widths]
    in_specs += [
        pl.BlockSpec((sum(widths), D_MODEL), lambda i: (0, 0)),
        pl.BlockSpec((ROW_TILE, D_MODEL), lambda i: (i, 0)),
        pl.BlockSpec((1, 1, D_MODEL), lambda i: (i // tiles_per_batch, 0, 0)),
    ]
    return pl.pallas_call(
        functools.partial(_out_proj_kernel, widths=widths),
        grid=(TOKENS // ROW_TILE,),
        in_specs=in_specs,
        out_specs=pl.BlockSpec((ROW_TILE, D_MODEL), lambda i: (i, 0)),
        out_shape=jax.ShapeDtypeStruct((TOKENS, D_MODEL), jnp.float32),
        compiler_params=pltpu.CompilerParams(vmem_limit_bytes=VMEM_LIMIT_V7X),
        name="out_proj",
    )(*parts, w_bf16, x, gate)


def _retention_tables():
    heads = np.arange(RET_HEADS, dtype=np.float64)
    log_g = np.log1p(-np.exp2(-5.0 - heads))
    idx = np.arange(RET_BLOCK, dtype=np.float64)
    diff = idx[:, None] - idx[None, :]
    ci, cj = (idx // CHUNK)[:, None], (idx // CHUNK)[None, :]
    expo = np.where(ci == cj, np.abs(diff), diff)
    decay = np.where(cj <= ci, np.exp(log_g[:, None, None] * expo[None]), 0.0)
    q_dec = np.exp(log_g[:, None] * (idx[None, :] + 1.0))
    k_dec = np.exp(log_g[:, None] * (RET_BLOCK - 1.0 - idx[None, :]))
    blk_dec = np.exp(log_g * RET_BLOCK)
    q_dec = np.broadcast_to(q_dec[:, :, None], (RET_HEADS, RET_BLOCK, RET_DK))
    k_dec = np.broadcast_to(k_dec[:, :, None], (RET_HEADS, RET_BLOCK, RET_DK))
    blk_dec = np.broadcast_to(blk_dec[:, None, None], (RET_HEADS, 1, RET_DV))
    half = RET_DK // 2
    inv = ROPE_BASE ** (-np.arange(half, dtype=np.float64) / half)
    ang = np.arange(SEQ, dtype=np.float64)[:, None] * inv[None, :]
    cos = np.concatenate([np.cos(ang), np.cos(ang)], axis=1)
    sin = np.concatenate([-np.sin(ang), np.sin(ang)], axis=1)
    f32 = lambda a: jnp.asarray(np.ascontiguousarray(a), jnp.float32)
    return f32(decay), f32(q_dec), f32(k_dec), f32(blk_dec), f32(cos), f32(sin)


def _retention_kernel(q_ref, k_ref, v_ref, gate_ref, cos_ref, sin_ref, dec_ref, qd_ref, kd_ref,
                      bd_ref, o_ref, state_ref):
    @pl.when(pl.program_id(1) == 0)
    def _():
        state_ref[...] = jnp.zeros_like(state_ref)

    cos, sin = cos_ref[...], sin_ref[...]
    for h in range(RET_HEADS):
        ks = slice(h * RET_DK, (h + 1) * RET_DK)
        vs = slice(h * RET_DV, (h + 1) * RET_DV)
        q = q_ref[:, ks].astype(jnp.float32)
        k = k_ref[:, ks].astype(jnp.float32)
        qr = q * cos + pltpu.roll(q, RET_DK // 2, axis=1) * sin
        kr = (k * cos + pltpu.roll(k, RET_DK // 2, axis=1) * sin) * (RET_DK ** -0.5)
        v = v_ref[:, vs]

        scores = _dot_nt(_bf16(qr), _bf16(kr)) * dec_ref[h]
        intra = _dot(_bf16(scores), v)
        state = state_ref[h]
        inter = _dot(_bf16(qr * qd_ref[h]), _bf16(state))
        state_ref[h] = state * bd_ref[h] + _dot_tn(_bf16(kr * kd_ref[h]), v)

        o = intra + inter
        mu = jnp.mean(o, axis=-1, keepdims=True)
        cen = o - mu
        var = jnp.mean(cen * cen, axis=-1, keepdims=True)
        normed = cen * lax.rsqrt(var + EPS)
        o_ref[:, vs] = _bf16(_silu(gate_ref[:, vs].astype(jnp.float32)) * normed)


def _retention(proj):
    decay, q_dec, k_dec, blk_dec, cos, sin = _retention_tables()
    nblk = SEQ // RET_BLOCK
    row = lambda b, n: b * nblk + n
    whole = lambda b, n: (0, 0, 0)
    return pl.pallas_call(
        _retention_kernel,
        grid=(BATCH, nblk),
        in_specs=[
            pl.BlockSpec((RET_BLOCK, A_QK), lambda b, n: (row(b, n), 0)),
            pl.BlockSpec((RET_BLOCK, A_QK), lambda b, n: (row(b, n), 1)),
            pl.BlockSpec((RET_BLOCK, A_V), lambda b, n: (row(b, n), 2 * A_QK // A_V)),
            pl.BlockSpec((RET_BLOCK, A_V), lambda b, n: (row(b, n), 2 * A_QK // A_V + 1)),
            pl.BlockSpec((RET_BLOCK, RET_DK), lambda b, n: (n, 0)),
            pl.BlockSpec((RET_BLOCK, RET_DK), lambda b, n: (n, 0)),
            pl.BlockSpec((RET_HEADS, RET_BLOCK, RET_BLOCK), whole),
            pl.BlockSpec((RET_HEADS, RET_BLOCK, RET_DK), whole),
            pl.BlockSpec((RET_HEADS, RET_BLOCK, RET_DK), whole),
            pl.BlockSpec((RET_HEADS, 1, RET_DV), whole),
        ],
        out_specs=pl.BlockSpec((RET_BLOCK, A_V), lambda b, n: (row(b, n), 0)),
        out_shape=jax.ShapeDtypeStruct((TOKENS, A_V), jnp.bfloat16),
        scratch_shapes=[pltpu.VMEM((RET_HEADS, RET_DK, RET_DV), jnp.float32)],
        compiler_params=pltpu.CompilerParams(dimension_semantics=("arbitrary", "arbitrary")),
        name="retention",
    )(proj, proj, proj, proj, cos, sin, decay, q_dec, k_dec, blk_dec)


def _spatial_gate_kernel(u_ref, z_ref, lng_ref, lnb_ref, w_ref, b_ref, o_ref):
    u = _gelu_tanh(u_ref[...].astype(jnp.float32))
    z = _gelu_tanh(z_ref[...].astype(jnp.float32))
    mu = jnp.mean(z, axis=-1, keepdims=True)
    cen = z - mu
    var = jnp.mean(cen * cen, axis=-1, keepdims=True)
    zn = _bf16(cen * lax.rsqrt(var + EPS) * lng_ref[...] + lnb_ref[...])
    rows = lax.broadcasted_iota(jnp.int32, (SG_LEN, SG_LEN), 0)
    cols = lax.broadcasted_iota(jnp.int32, (SG_LEN, SG_LEN), 1)
    keep = (rows // CHUNK) >= (cols // CHUNK)
    for g in range(SG_GROUPS):
        cs = slice(g * SG_CH, (g + 1) * SG_CH)
        w = _bf16(jnp.where(keep, w_ref[g], 0.0))
        for blk in range(SG_TILE // SG_LEN):
            rs = slice(blk * SG_LEN, (blk + 1) * SG_LEN)
            mixed = _dot(w, zn[rs, cs]) + b_ref[g]
            o_ref[rs, cs] = _bf16(u[rs, cs] * mixed)


def _spatial_gate(proj, ln_g, ln_b, w_s, b_s):
    u_col = (2 * A_QK + 2 * A_V) // SG_WIDTH
    b_full = jnp.broadcast_to(b_s[:, :, None], (SG_GROUPS, SG_LEN, SG_CH))
    return pl.pallas_call(
        _spatial_gate_kernel,
        grid=(TOKENS // SG_TILE,),
        in_specs=[
            pl.BlockSpec((SG_TILE, SG_WIDTH), lambda i: (i, u_col)),
            pl.BlockSpec((SG_TILE, SG_WIDTH), lambda i: (i, u_col + 1)),
            pl.BlockSpec((1, SG_WIDTH), lambda i: (0, 0)),
            pl.BlockSpec((1, SG_WIDTH), lambda i: (0, 0)),
            pl.BlockSpec((SG_GROUPS, SG_LEN, SG_LEN), lambda i: (0, 0, 0)),
            pl.BlockSpec((SG_GROUPS, SG_LEN, SG_CH), lambda i: (0, 0, 0)),
        ],
        out_specs=pl.BlockSpec((SG_TILE, SG_WIDTH), lambda i: (i, 0)),
        out_shape=jax.ShapeDtypeStruct((TOKENS, SG_WIDTH), jnp.bfloat16),
        name="spatial_gate",
    )(proj, proj, ln_g.reshape(1, SG_WIDTH), ln_b.reshape(1, SG_WIDTH), w_s, b_full)


def _band_bias(rel_bias):
    period = ATT_QBLOCK + ATT_BAND
    m = np.arange(period)
    shift = np.where(m < ATT_BAND, m, m - period)
    rel = BAND_PAD - shift
    table = rel_bias[:, np.clip(rel, -(CHUNK - 1), MAX_REL) + (CHUNK - 1)].astype(jnp.float32)
    flat = jnp.tile(table, (1, ATT_QBLOCK))[:, :ATT_QBLOCK * (period - 1)]
    bias = flat.reshape(ATT_HEADS, ATT_QBLOCK, period - 1)[:, :, :ATT_BAND]
    i = np.arange(ATT_QBLOCK)[:, None]
    jj = np.arange(ATT_BAND)[None, :]
    in_band = (jj // CHUNK >= i // CHUNK) & (jj // CHUNK <= i // CHUNK + BAND_PAD // CHUNK)
    block = np.arange(ATT_LEAD + 1)[:, None, None]
    in_seq = (jj[None] + block * ATT_QBLOCK >= BAND_PAD) | (block == ATT_LEAD)
    keep = in_band[None] & in_seq
    return jnp.where(jnp.asarray(keep)[:, None], (bias * LOG2_E)[None], NEG_BIG)


def _head_rms(x, gain):
    head_of_lane = lax.broadcasted_iota(jnp.int32, x.shape, 1) // ATT_DH
    sq = x * x
    ms = jnp.zeros_like(x)
    for h in range(ATT_GROUP):
        in_head = head_of_lane == h
        ms = jnp.where(in_head, jnp.sum(jnp.where(in_head, sq, 0.0), axis=-1, keepdims=True), ms)
    return x * lax.rsqrt(ms * (1.0 / ATT_DH) + EPS) * gain


def _head_rms_mxu(x, gain, same_head):
    sq = x * x
    hi = _bf16(sq)
    lo = _bf16(sq - hi.astype(jnp.float32))
    groups = []
    for g in range(x.shape[1] // SEG_LANES):
        ls = slice(g * SEG_LANES, (g + 1) * SEG_LANES)
        groups.append(_dot(hi[:, ls], same_head) + _dot(lo[:, ls], same_head))
    ss = jnp.concatenate(groups, axis=-1)
    return x * lax.rsqrt(ss * (1.0 / ATT_DH) + EPS) * gain


def _band_attn_kernel(q_ref, k_ref, v_ref, bias_ref, qg_ref, kg_ref, seg_ref, o_ref,
                      kpad_ref, vpad_ref, s_ref, p_ref, den_ref):
    n = pl.program_id(2)
    pair = 2 * ATT_DH

    @pl.when(n == 0)
    def _():
        low_half = lax.broadcasted_iota(jnp.int32, (KV_PREP_ROWS, pair), 1) < ATT_DH
        zero = jnp.zeros((KV_PREP_ROWS, pair), jnp.bfloat16)
        for h in range(ATT_GROUP):
            kpad_ref[h, 0:BAND_PAD, :] = jnp.zeros((BAND_PAD, pair), jnp.bfloat16)
            vpad_ref[h, 0:BAND_PAD, :] = jnp.zeros((BAND_PAD, pair), jnp.bfloat16)
        for r in range(SEQ // KV_PREP_ROWS):
            rows = slice(r * KV_PREP_ROWS, (r + 1) * KV_PREP_ROWS)
            dst = slice(BAND_PAD + r * KV_PREP_ROWS, BAND_PAD + (r + 1) * KV_PREP_ROWS)
            kn = _bf16(_head_rms_mxu(k_ref[rows, :].astype(jnp.float32), kg_ref[...], seg_ref[...]))
            v = v_ref[rows, :]
            for h in range(ATT_GROUP):
                cols = slice((h // 2) * pair, (h // 2 + 1) * pair)
                own = low_half if h % 2 == 0 else jnp.logical_not(low_half)
                kpad_ref[h, dst, :] = jnp.where(own, kn[:, cols], zero)
                vpad_ref[h, dst, :] = jnp.where(own, v[:, cols], zero)

    q = _bf16(_head_rms(q_ref[...].astype(jnp.float32), qg_ref[...]) * (LOG2_E * ATT_DH ** -0.5))
    start = pl.multiple_of(n * ATT_QBLOCK, ATT_QBLOCK)
    for h in range(ATT_GROUP):
        q_pair = q[:, (h // 2) * pair:(h // 2 + 1) * pair]
        s_ref[h] = _dot_nt(q_pair, kpad_ref[h, pl.ds(start, ATT_BAND), :])

    for h in range(ATT_GROUP):
        s = s_ref[h] + bias_ref[h]
        m = jnp.max(s, axis=-1, keepdims=True)
        p = jnp.exp2(s - m)
        den_ref[h] = jnp.sum(p, axis=-1, keepdims=True)
        p_ref[h] = _bf16(p)
    outs = []
    for h0 in range(0, ATT_GROUP, 2):
        pv = [_dot(p_ref[h], vpad_ref[h, pl.ds(start, ATT_BAND), :]) * (1.0 / den_ref[h])
              for h in (h0, h0 + 1)]
        outs.append(pv[0] + pv[1])
    o_ref[...] = _bf16(jnp.concatenate(outs, axis=-1))


def _band_attention(qkv, q_g, k_g, rel_bias):
    bias = _band_bias(rel_bias)
    nq = SEQ // ATT_QBLOCK
    width = ATT_GROUP * ATT_DH
    n_groups = ATT_HEADS // ATT_GROUP
    qg = jnp.tile(q_g, ATT_GROUP).reshape(1, width)
    kg = jnp.tile(k_g, ATT_GROUP).reshape(1, width)
    lane_head = np.arange(SEG_LANES) // ATT_DH
    same_head = jnp.asarray(lane_head[:, None] == lane_head[None, :], jnp.bfloat16)
    return pl.pallas_call(
        _band_attn_kernel,
        grid=(BATCH, n_groups, nq),
        in_specs=[
            pl.BlockSpec((ATT_QBLOCK, width), lambda b, hg, n: (b * nq + n, hg)),
            pl.BlockSpec((SEQ, width), lambda b, hg, n: (b, n_groups + hg)),
            pl.BlockSpec((SEQ, width), lambda b, hg, n: (b, 2 * n_groups + hg)),
            pl.BlockSpec((None, ATT_GROUP, ATT_QBLOCK, ATT_BAND),
                         lambda b, hg, n: (jnp.minimum(n, ATT_LEAD), hg, 0, 0)),
            pl.BlockSpec((1, width), lambda b, hg, n: (0, 0)),
            pl.BlockSpec((1, width), lambda b, hg, n: (0, 0)),
            pl.BlockSpec((SEG_LANES, SEG_LANES), lambda b, hg, n: (0, 0)),
        ],
        out_specs=pl.BlockSpec((ATT_QBLOCK, width), lambda b, hg, n: (b * nq + n, hg)),
        out_shape=jax.ShapeDtypeStruct((TOKENS, D_MODEL), jnp.bfloat16),
        scratch_shapes=[pltpu.VMEM((ATT_GROUP, SEQ + BAND_PAD, 2 * ATT_DH), jnp.bfloat16),
                        pltpu.VMEM((ATT_GROUP, SEQ + BAND_PAD, 2 * ATT_DH), jnp.bfloat16),
                        pltpu.VMEM((ATT_GROUP, ATT_QBLOCK, ATT_BAND), jnp.float32),
                        pltpu.VMEM((ATT_GROUP, ATT_QBLOCK, ATT_BAND), jnp.bfloat16),
                        pltpu.VMEM((ATT_GROUP, ATT_QBLOCK, 1), jnp.float32)],
        compiler_params=pltpu.CompilerParams(
            dimension_semantics=("arbitrary", "arbitrary", "arbitrary")),
        name="band_attention",
    )(qkv, qkv, qkv, bias, qg, kg, same_head)


def _store_token_major(ref, value, lead=()):
    rows = value.shape[0]
    for c in range(ROW_TILES):
        ref[(*lead, pl.ds(c, rows, stride=ROW_TILES), slice(None))] = value[:, c * 128:(c + 1) * 128]


def _load_token_major(ref, rows, c, lead=()):
    return ref[(*lead, pl.ds(c, rows, stride=ROW_TILES), slice(None))]


def _split_bf16(x):
    hi = _bf16(x)
    lo = _bf16(x - hi.astype(jnp.float32))
    return hi, lo


def _router_kernel(x_ref, g_ref, sc_ref, sh_ref, rw_ref, rb_ref,
                   h_ref, idx_ref, rank_ref, gate_ref, cnt_ref, carry_ref):
    @pl.when(pl.program_id(0) == 0)
    def _():
        carry_ref[...] = jnp.zeros_like(carry_ref)

    h = _norm_mod(x_ref[...], g_ref[...], sc_ref[0], sh_ref[0])
    _store_token_major(h_ref, h)
    h_hi, h_lo = _split_bf16(h)
    w_hi, w_lo = _split_bf16(rw_ref[...])
    logits = _dot_nt(w_hi, h_hi) + _dot_nt(w_hi, h_lo) + _dot_nt(w_lo, h_hi) + rb_ref[...]

    expert = lax.broadcasted_iota(jnp.int32, logits.shape, 0).astype(jnp.float32)
    work = logits
    vals, idxs, sels = [], [], []
    for _ in range(TOP_K):
        m = jnp.max(work, axis=0, keepdims=True)
        pick = jnp.min(jnp.where(work == m, expert, float(N_EXPERTS)), axis=0, keepdims=True)
        sel = expert == pick
        work = jnp.where(sel, -jnp.inf, work)
        vals.append(m)
        idxs.append(pick)
        sels.append(sel)
    exps = [jnp.exp(v - vals[0]) for v in vals]
    denom = exps[0] + exps[1] + exps[2] + exps[3]
    gate_ref[...] = jnp.concatenate([e / denom for e in exps], axis=0)
    idx_ref[...] = jnp.concatenate(idxs, axis=0).astype(jnp.int32)

    chosen = jnp.zeros(logits.shape, jnp.float32)
    for sel in sels:
        chosen = jnp.where(sel, 1.0, chosen)
    tile = logits.shape[1]
    earlier = (lax.broadcasted_iota(jnp.int32, (tile, tile), 0)
               < lax.broadcasted_iota(jnp.int32, (tile, tile), 1))
    before = _dot(_bf16(chosen), jnp.where(earlier, 1.0, 0.0).astype(jnp.bfloat16))
    rank_full = before + carry_ref[...]
    ranks = [jnp.sum(jnp.where(sel, rank_full, 0.0), axis=0, keepdims=True) for sel in sels]
    rank_ref[...] = jnp.concatenate(ranks, axis=0).astype(jnp.int32)
    carry = carry_ref[...] + jnp.sum(chosen, axis=1, keepdims=True)
    carry_ref[...] = carry
    cnt_ref[...] = jnp.broadcast_to(carry, cnt_ref.shape)


def _router(x, g, sc, sh, router_w, router_b):
    tiles_per_batch = SEQ // ROUTE_TILE
    lane_out = lambda dt: jax.ShapeDtypeStruct((TOP_K, TOKENS), dt)
    lane_spec = pl.BlockSpec((TOP_K, ROUTE_TILE), lambda i: (0, i))
    return pl.pallas_call(
        _router_kernel,
        grid=(TOKENS // ROUTE_TILE,),
        in_specs=[
            pl.BlockSpec((ROUTE_TILE, D_MODEL), lambda i: (i, 0)),
            pl.BlockSpec((1, D_MODEL), lambda i: (0, 0)),
            pl.BlockSpec((1, 1, D_MODEL), lambda i: (i // tiles_per_batch, 0, 0)),
            pl.BlockSpec((1, 1, D_MODEL), lambda i: (i // tiles_per_batch, 0, 0)),
            pl.BlockSpec((N_EXPERTS, D_MODEL), lambda i: (0, 0)),
            pl.BlockSpec((N_EXPERTS, 1), lambda i: (0, 0)),
        ],
        out_specs=[
            pl.BlockSpec((ROUTE_TILE * ROW_TILES, 128), lambda i: (i, 0)),
            lane_spec, lane_spec, lane_spec,
            pl.BlockSpec((N_EXPERTS, 128), lambda i: (0, 0)),
        ],
        out_shape=[
            jax.ShapeDtypeStruct((TOKENS * ROW_TILES, 128), jnp.float32),
            lane_out(jnp.int32), lane_out(jnp.int32), lane_out(jnp.float32),
            jax.ShapeDtypeStruct((N_EXPERTS, 128), jnp.float32),
        ],
        scratch_shapes=[pltpu.VMEM((N_EXPERTS, 1), jnp.float32)],
        compiler_params=pltpu.CompilerParams(dimension_semantics=("arbitrary",)),
        name="moe_router",
    )(x, g.reshape(1, D_MODEL), sc, sh, router_w.T, router_b.reshape(N_EXPERTS, 1))


def _row_copy(src_ref, src_row, dst_ref, dst_row, sem):
    src = pl.multiple_of(src_row * ROW_TILES, ROW_TILES)
    dst = pl.multiple_of(dst_row * ROW_TILES, ROW_TILES)
    return pltpu.make_async_copy(src_ref.at[pl.ds(src, ROW_TILES), :],
                                 dst_ref.at[pl.ds(dst, ROW_TILES), :], sem)


def _zero_block_copy(zero_ref, xs_ref, row, sem):
    row = pl.multiple_of(row * ROW_TILES, MOE_BLOCK * ROW_TILES)
    return pltpu.make_async_copy(zero_ref, xs_ref.at[pl.ds(row, MOE_BLOCK * ROW_TILES), :], sem)


def _dispatch_kernel(zstart_ref, nused_ref, pos_ref, h_ref, xs_ref, zero_ref, zsem, sem):
    step = pl.program_id(0)

    @pl.when(step == 0)
    def _():
        zero_ref[...] = jnp.zeros_like(zero_ref)

        def each_block(fn):
            def expert_last(e, carry):
                @pl.when(zstart_ref[e] >= 0)
                def _():
                    fn(_zero_block_copy(zero_ref, xs_ref, zstart_ref[e], zsem))
                return carry

            def unused(blk, carry):
                fn(_zero_block_copy(zero_ref, xs_ref, blk * MOE_BLOCK, zsem))
                return carry

            lax.fori_loop(0, N_EXPERTS, expert_last, 0)
            lax.fori_loop(nused_ref[0], MOE_NBLOCKS, unused, 0)

        each_block(lambda copy: copy.start())
        each_block(lambda copy: copy.wait())

    def issue(i, carry):
        for u in range(ISSUE_UNROLL):
            t = i * ISSUE_UNROLL + u
            for k in range(TOP_K):
                _row_copy(h_ref, t, xs_ref, pos_ref[t * TOP_K + k], sem).start(priority=k % 2)
        return carry

    lax.fori_loop(0, DISPATCH_TILE // ISSUE_UNROLL, issue, 0)

    for _ in range(TOP_K):
        pltpu.make_async_copy(h_ref, xs_ref.at[pl.ds(0, DISPATCH_TILE * ROW_TILES), :], sem).wait()


def _dispatch(h, pos_flat, zstart, n_used):
    grid_spec = pltpu.PrefetchScalarGridSpec(
        num_scalar_prefetch=2,
        grid=(TOKENS // DISPATCH_TILE,),
        in_specs=[
            pl.BlockSpec((DISPATCH_TILE * TOP_K,), lambda i, zs, nu: (i,),
                         memory_space=pltpu.SMEM),
            pl.BlockSpec((DISPATCH_TILE * ROW_TILES, 128), lambda i, zs, nu: (i, 0)),
        ],
        out_specs=pl.BlockSpec(memory_space=pl.ANY),
        scratch_shapes=[pltpu.VMEM((MOE_BLOCK * ROW_TILES, 128), jnp.float32),
                        pltpu.SemaphoreType.DMA, pltpu.SemaphoreType.DMA],
    )
    return pl.pallas_call(
        _dispatch_kernel,
        grid_spec=grid_spec,
        out_shape=jax.ShapeDtypeStruct((MOE_CAP * ROW_TILES, 128), jnp.float32),
        compiler_params=pltpu.CompilerParams(dimension_semantics=("arbitrary",)),
        name="moe_dispatch",
    )(zstart, n_used, pos_flat, h)


def _expert_kernel(be_ref, nused_ref, next_ref, xs_ref, b1_ref, b2_ref, w1_hbm, w2_hbm, ys_ref,
                   w1f_ref, w2f_ref, w1b_ref, w2b_ref, sems, *, layer):
    i = pl.program_id(0)
    expert = be_ref[i]
    used = i < nused_ref[0]
    new_expert = jnp.logical_or(i == 0, expert != be_ref[jnp.maximum(i - 1, 0)])

    def weight_copies(e):
        return (pltpu.make_async_copy(w1_hbm.at[layer, e], w1f_ref, sems.at[0]),
                pltpu.make_async_copy(w2_hbm.at[layer, e], w2f_ref, sems.at[1]))

    @pl.when(jnp.logical_and(used, new_expert))
    def _():
        @pl.when(i == 0)
        def _():
            for copy in weight_copies(expert):
                copy.start()

        for copy in weight_copies(expert):
            copy.wait()
        for r in range(D_MODEL // 256):
            rows = slice(r * 256, (r + 1) * 256)
            w1b_ref[rows, :] = _bf16(w1f_ref[rows, :])
            w2b_ref[rows, :] = _bf16(w2f_ref[rows, :])

        @pl.when(next_ref[i] >= 0)
        def _():
            for copy in weight_copies(next_ref[i]):
                copy.start()

    @pl.when(used)
    def _():
        x = _bf16(jnp.concatenate([_load_token_major(xs_ref, MOE_BLOCK, c)
                                   for c in range(ROW_TILES)], axis=-1))
        glu = _dot(x, w1b_ref[:, :D_FF]) + b1_ref[0][:, :D_FF]
        lin = _dot(x, w1b_ref[:, D_FF:]) + b1_ref[0][:, D_FF:]
        glu = jnp.minimum(glu, SWIGLU_LIMIT)
        lin = jnp.clip(lin, -SWIGLU_LIMIT, SWIGLU_LIMIT)
        act = glu * (1.0 / (1.0 + jnp.exp(-SWIGLU_ALPHA * glu))) * (lin + 1.0)
        _store_token_major(ys_ref, _dot(_bf16(act), w2b_ref[...]) + b2_ref[0])

    @pl.when(jnp.logical_not(used))
    def _():
        ys_ref[...] = jnp.zeros_like(ys_ref)


def _experts(xs, block_e, n_used, next_e, layer, w1, b1, w2, b2):
    grid_spec = pltpu.PrefetchScalarGridSpec(
        num_scalar_prefetch=3,
        grid=(MOE_NBLOCKS,),
        in_specs=[
            pl.BlockSpec((MOE_BLOCK * ROW_TILES, 128),
                         lambda i, be, nu, nx: (jnp.minimum(i, nu[0] - 1), 0)),
            pl.BlockSpec((None, 1, 1, 2 * D_FF), lambda i, be, nu, nx: (layer, be[i], 0, 0)),
            pl.BlockSpec((None, 1, 1, D_MODEL), lambda i, be, nu, nx: (layer, be[i], 0, 0)),
            pl.BlockSpec(memory_space=pl.ANY),
            pl.BlockSpec(memory_space=pl.ANY),
        ],
        out_specs=pl.BlockSpec((MOE_BLOCK * ROW_TILES, 128), lambda i, be, nu, nx: (i, 0)),
        scratch_shapes=[pltpu.VMEM((D_MODEL, 2 * D_FF), jnp.float32),
                        pltpu.VMEM((D_FF, D_MODEL), jnp.float32),
                        pltpu.VMEM((D_MODEL, 2 * D_FF), jnp.bfloat16),
                        pltpu.VMEM((D_FF, D_MODEL), jnp.bfloat16),
                        pltpu.SemaphoreType.DMA((2,))],
    )
    return pl.pallas_call(
        functools.partial(_expert_kernel, layer=layer),
        grid_spec=grid_spec,
        out_shape=jax.ShapeDtypeStruct((MOE_CAP * ROW_TILES, 128), jnp.float32),
        compiler_params=pltpu.CompilerParams(dimension_semantics=("arbitrary",),
                                             vmem_limit_bytes=VMEM_LIMIT_V7X),
        name="moe_experts",
    )(block_e, n_used, next_e, xs, b1.reshape(DEPTH, N_EXPERTS, 1, 2 * D_FF),
      b2.reshape(DEPTH, N_EXPERTS, 1, D_MODEL), w1, w2)


def _combine_kernel(pos_ref, pos_next_ref, x_ref, gates_ref, g2_ref, ys_ref, o_ref, buf_ref, sems):
    step = pl.program_id(0)
    n_steps = pl.num_programs(0)
    slot = step % 2

    def gather(p_ref, dst_slot):
        def issue(i, carry):
            for u in range(ISSUE_UNROLL):
                t = i * ISSUE_UNROLL + u
                for k in range(TOP_K):
                    _row_copy(ys_ref, p_ref[t * TOP_K + k], buf_ref.at[dst_slot, k], t,
                              sems.at[dst_slot]).start(priority=k % 2)
            return carry

        lax.fori_loop(0, DISPATCH_TILE // ISSUE_UNROLL, issue, 0)

    @pl.when(step == 0)
    def _():
        gather(pos_ref, 0)

    @pl.when(step + 1 < n_steps)
    def _():
        gather(pos_next_ref, 1 - slot)

    for k in range(TOP_K):
        pltpu.make_async_copy(ys_ref.at[pl.ds(0, DISPATCH_TILE * ROW_TILES), :],
                              buf_ref.at[slot, k], sems.at[slot]).wait()

    gates = gates_ref[...]
    gate_cols = [jnp.broadcast_to(gates[:, k:k + 1], (DISPATCH_TILE, 128)) for k in range(TOP_K)]
    for c in range(ROW_TILES):
        cols = slice(c * 128, (c + 1) * 128)
        y = gate_cols[0] * _load_token_major(buf_ref, DISPATCH_TILE, c, (slot, 0))
        for k in range(1, TOP_K):
            y = y + gate_cols[k] * _load_token_major(buf_ref, DISPATCH_TILE, c, (slot, k))
        o_ref[:, cols] = x_ref[:, cols] + g2_ref[0][:, cols] * y


def _combine(x, pos_flat, gates_tok, g2, ys):
    tiles_per_batch = SEQ // DISPATCH_TILE
    n_steps = TOKENS // DISPATCH_TILE
    return pl.pallas_call(
        _combine_kernel,
        grid=(n_steps,),
        in_specs=[
            pl.BlockSpec((DISPATCH_TILE * TOP_K,), lambda i: (i,), memory_space=pltpu.SMEM),
            pl.BlockSpec((DISPATCH_TILE * TOP_K,), lambda i: (jnp.minimum(i + 1, n_steps - 1),),
                         memory_space=pltpu.SMEM),
            pl.BlockSpec((DISPATCH_TILE, D_MODEL), lambda i: (i, 0)),
            pl.BlockSpec((DISPATCH_TILE, TOP_K), lambda i: (i, 0)),
            pl.BlockSpec((1, 1, D_MODEL), lambda i: (i // tiles_per_batch, 0, 0)),
            pl.BlockSpec(memory_space=pl.ANY),
        ],
        out_specs=pl.BlockSpec((DISPATCH_TILE, D_MODEL), lambda i: (i, 0)),
        out_shape=jax.ShapeDtypeStruct((TOKENS, D_MODEL), jnp.float32),
        scratch_shapes=[pltpu.VMEM((2, TOP_K, DISPATCH_TILE * ROW_TILES, 128), jnp.float32),
                        pltpu.SemaphoreType.DMA((2,))],
        compiler_params=pltpu.CompilerParams(dimension_semantics=("arbitrary",),
                                             vmem_limit_bytes=VMEM_LIMIT_V7X),
        name="moe_combine",
    )(pos_flat, pos_flat, x, gates_tok, g2, ys)


def _moe_layer(x, g, sc, sh, g2, router_w, router_b, layer, w1, b1, w2, b2):
    h, idx, rank, gates, cnt = _router(x, g, sc, sh, router_w, router_b)
    counts = cnt[:, 0].astype(jnp.int32)
    padded = ((counts + MOE_BLOCK - 1) // MOE_BLOCK) * MOE_BLOCK
    pends = jnp.cumsum(padded)
    pstarts = pends - padded
    experts = jnp.arange(N_EXPERTS, dtype=jnp.int32)[:, None, None]
    pos = rank + jnp.sum(jnp.where(idx[None] == experts, pstarts[:, None, None], 0), axis=0)
    pos_flat = pos.T.reshape(-1)
    block_start = jnp.arange(MOE_NBLOCKS, dtype=jnp.int32) * MOE_BLOCK
    block_e = jnp.minimum(jnp.sum(block_start[:, None] >= pends[None, :], axis=1),
                          N_EXPERTS - 1).astype(jnp.int32)
    n_used = (pends[-1:] // MOE_BLOCK).astype(jnp.int32)
    zstart = jnp.where(padded > 0, pends - MOE_BLOCK, -1).astype(jnp.int32)
    ids = jnp.arange(N_EXPERTS, dtype=jnp.int32)
    later_used = jnp.logical_and(ids[None, :] > ids[:, None], (padded > 0)[None, :])
    next_used = jnp.min(jnp.where(later_used, ids[None, :], N_EXPERTS), axis=1)
    next_used = jnp.where(next_used < N_EXPERTS, next_used, -1)
    next_e = jnp.sum(jnp.where(block_e[:, None] == ids[None, :], next_used[None, :], 0),
                     axis=1).astype(jnp.int32)
    xs = _dispatch(h, pos_flat, zstart, n_used)
    ys = _experts(xs, block_e, n_used, next_e, layer, w1, b1, w2, b2)
    return _combine(x, pos_flat, gates.T, g2, ys)


def kernel(x, c, ada_w, ada_b, norm_mix_g, norm_ffn_g, ev_w_in, ev_w_out, sg_ln_g, sg_ln_b, sg_w, sg_b, od_w_in, od_w_out, od_q_g, od_k_g, od_rel_bias, moe_router_w, moe_router_b, moe_w1, moe_b1, moe_w2, moe_b2):
    mod = _ada_mod(c, ada_w, ada_b)
    xt = x.reshape(TOKENS, D_MODEL)
    for l in range(DEPTH):
        sh1, sc1, g1, sh2, sc2, g2 = [m.reshape(BATCH, 1, D_MODEL)
                                      for m in jnp.split(mod[l], 6, axis=-1)]
        i = l // 2
        if l % 2 == 0:
            proj = _norm_proj(xt, norm_mix_g[l], sc1, sh1, _bf16(ev_w_in[i]))
            a_out = _retention(proj)
            b_out = _spatial_gate(proj, sg_ln_g[i], sg_ln_b[i], sg_w[i], sg_b[i])
            xt = _out_proj([a_out, b_out], _bf16(ev_w_out[i]), xt, g1)
        else:
            qkv = _norm_proj(xt, norm_mix_g[l], sc1, sh1, _bf16(od_w_in[i]))
            att = _band_attention(qkv, od_q_g[i], od_k_g[i], od_rel_bias[i])
            xt = _out_proj([att], _bf16(od_w_out[i]), xt, g1)
        xt = _moe_layer(xt, norm_ffn_g[l], sc2, sh2, g2, moe_router_w[l], moe_router_b[l],
                        l, moe_w1, moe_b1, moe_w2, moe_b2)
    return xt.reshape(BATCH, SEQ, D_MODEL)
```

```python
import functools
import math

import numpy as np
import jax
import jax.numpy as jnp
from jax import lax
from jax.experimental import pallas as pl
from jax.experimental.pallas import tpu as pltpu

D_MODEL = 1024
BATCH = 8
SEQ = 2048
DEPTH = 4
TOKENS = BATCH * SEQ
CHUNK = 64
EPS = 1e-6

RET_HEADS = 4
RET_DK = 128
RET_DV = 256
ROPE_BASE = 10000.0
A_QK = RET_HEADS * RET_DK
A_V = RET_HEADS * RET_DV
SG_GROUPS = 4
SG_CH = 128
SG_WIDTH = SG_GROUPS * SG_CH
SG_LEN = 128
EVEN_IN = 2 * A_QK + 2 * A_V + 2 * SG_WIDTH
EVEN_MIX = A_V + SG_WIDTH

ATT_HEADS = 16
ATT_DH = 64
BAND_PAD = 8 * CHUNK
MAX_REL = 256
REL_SIZE = (CHUNK - 1) + MAX_REL + 1

N_EXPERTS = 32
TOP_K = 4
D_FF = D_MODEL
SWIGLU_ALPHA = 1.702
SWIGLU_LIMIT = 7.0

ROW_TILE = 1024
COL_CHUNK = 512
RET_BLOCK = 256
ATT_QBLOCK = 128
ATT_BAND = ATT_QBLOCK + BAND_PAD
ATT_GROUP = 8
ATT_LEAD = BAND_PAD // ATT_QBLOCK
KV_PREP_ROWS = 256
SEG_LANES = 256
SG_TILE = 512
ROUTE_TILE = 512
MOE_BLOCK = 256
MOE_CAP = TOKENS * TOP_K + N_EXPERTS * MOE_BLOCK
MOE_NBLOCKS = MOE_CAP // MOE_BLOCK
DISPATCH_TILE = 512
COMBINE_TILE = 256
ISSUE_UNROLL = 4
VMEM_LIMIT_V7X = 56 * 1024 * 1024
ROW_TILES = D_MODEL // 128

NEG_BIG = -1e30
LOG2_E = math.log2(math.e)


def _silu(x):
    return x * (1.0 / (1.0 + jnp.exp(-x)))


def _gelu_tanh(x):
    return 0.5 * x * (1.0 + jnp.tanh(math.sqrt(2.0 / math.pi) * (x + 0.044715 * (x * x * x))))


def _bf16(x):
    return x.astype(jnp.bfloat16)


def _dot(a, b):
    return jnp.dot(a, b, preferred_element_type=jnp.float32)


def _dot_nt(a, b):
    return lax.dot_general(a, b, (((1,), (1,)), ((), ())), preferred_element_type=jnp.float32)


def _dot_tn(a, b):
    return lax.dot_general(a, b, (((0,), (0,)), ((), ())), preferred_element_type=jnp.float32)


def _norm_mod(x, g, sc, sh):
    y = x * lax.rsqrt(jnp.mean(x * x, axis=-1, keepdims=True) + EPS)
    return (y * g) * (1.0 + sc) + sh


def _ada_kernel(c_ref, w_ref, b_ref, o_ref):
    c_act = _silu(c_ref[...])
    o_ref[0] = _dot(c_act, w_ref[0]) + b_ref[0]


def _ada_mod(c, ada_w, ada_b):
    n_col = 6
    return pl.pallas_call(
        _ada_kernel,
        grid=(DEPTH, n_col),
        in_specs=[
            pl.BlockSpec((BATCH, D_MODEL), lambda l, j: (0, 0)),
            pl.BlockSpec((1, D_MODEL, D_MODEL), lambda l, j: (l, 0, j)),
            pl.BlockSpec((1, 1, D_MODEL), lambda l, j: (l, 0, j)),
        ],
        out_specs=pl.BlockSpec((1, BATCH, D_MODEL), lambda l, j: (l, 0, j)),
        out_shape=jax.ShapeDtypeStruct((DEPTH, BATCH, 6 * D_MODEL), jnp.float32),
        name="ada_mod",
    )(c, ada_w, ada_b.reshape(DEPTH, 1, 6 * D_MODEL))


def _norm_proj_kernel(x_ref, g_ref, sc_ref, sh_ref, w_ref, o_ref):
    h = _bf16(_norm_mod(x_ref[...], g_ref[...], sc_ref[0], sh_ref[0]))
    n_out = o_ref.shape[1]
    for j in range(n_out // COL_CHUNK):
        cols = slice(j * COL_CHUNK, (j + 1) * COL_CHUNK)
        o_ref[:, cols] = _bf16(_dot(h, w_ref[:, cols]))


def _norm_proj(x, g, sc, sh, w_bf16):
    n_out = w_bf16.shape[1]
    tiles_per_batch = SEQ // ROW_TILE
    return pl.pallas_call(
        _norm_proj_kernel,
        grid=(TOKENS // ROW_TILE,),
        in_specs=[
            pl.BlockSpec((ROW_TILE, D_MODEL), lambda i: (i, 0)),
            pl.BlockSpec((1, D_MODEL), lambda i: (0, 0)),
            pl.BlockSpec((1, 1, D_MODEL), lambda i: (i // tiles_per_batch, 0, 0)),
            pl.BlockSpec((1, 1, D_MODEL), lambda i: (i // tiles_per_batch, 0, 0)),
            pl.BlockSpec((D_MODEL, n_out), lambda i: (0, 0)),
        ],
        out_specs=pl.BlockSpec((ROW_TILE, n_out), lambda i: (i, 0)),
        out_shape=jax.ShapeDtypeStruct((TOKENS, n_out), jnp.bfloat16),
        compiler_params=pltpu.CompilerParams(vmem_limit_bytes=VMEM_LIMIT_V7X),
        name="norm_proj",
    )(x, g.reshape(1, D_MODEL), sc, sh, w_bf16)


def _out_proj_kernel(*refs, widths):
    part_refs = refs[:len(widths)]
    w_ref, x_ref, g_ref, o_ref = refs[len(widths):]
    for j in range(D_MODEL // COL_CHUNK):
        cols = slice(j * COL_CHUNK, (j + 1) * COL_CHUNK)
        acc = None
        row0 = 0
        for p_ref, width in zip(part_refs, widths):
            term = _dot(p_ref[...], w_ref[row0:row0 + width, cols])
            acc = term if acc is None else acc + term
            row0 += width
        o_ref[:, cols] = x_ref[:, cols] + g_ref[0][:, cols] * acc


def _out_proj(parts, w_bf16, x, gate):
    widths = tuple(p.shape[1] for p in parts)
    tiles_per_batch = SEQ // ROW_TILE
    in_specs = [pl.BlockSpec((ROW_TILE, width), lambda i: (i, 0)) for width in widths]
    in_specs += [
        pl.BlockSpec((sum(widths), D_MODEL), lambda i: (0, 0)),
        pl.BlockSpec((ROW_TILE, D_MODEL), lambda i: (i, 0)),
        pl.BlockSpec((1, 1, D_MODEL), lambda i: (i // tiles_per_batch, 0, 0)),
    ]
    return pl.pallas_call(
        functools.partial(_out_proj_kernel, widths=widths),
        grid=(TOKENS // ROW_TILE,),
        in_specs=in_specs,
        out_specs=pl.BlockSpec((ROW_TILE, D_MODEL), lambda i: (i, 0)),
        out_shape=jax.ShapeDtypeStruct((TOKENS, D_MODEL), jnp.float32),
        compiler_params=pltpu.CompilerParams(vmem_limit_bytes=VMEM_LIMIT_V7X),
        name="out_proj",
    )(*parts, w_bf16, x, gate)


def _retention_tables():
    heads = np.arange(RET_HEADS, dtype=np.float64)
    log_g = np.log1p(-np.exp2(-5.0 - heads))
    idx = np.arange(RET_BLOCK, dtype=np.float64)
    diff = idx[:, None] - idx[None, :]
    ci, cj = (idx // CHUNK)[:, None], (idx // CHUNK)[None, :]
    expo = np.where(ci == cj, np.abs(diff), diff)
    decay = np.where(cj <= ci, np.exp(log_g[:, None, None] * expo[None]), 0.0)
    q_dec = np.exp(log_g[:, None] * (idx[None, :] + 1.0))
    k_dec = np.exp(log_g[:, None] * (RET_BLOCK - 1.0 - idx[None, :]))
    blk_dec = np.exp(log_g * RET_BLOCK)
    q_dec = np.broadcast_to(q_dec[:, :, None], (RET_HEADS, RET_BLOCK, RET_DK))
    k_dec = np.broadcast_to(k_dec[:, :, None], (RET_HEADS, RET_BLOCK, RET_DK))
    blk_dec = np.broadcast_to(blk_dec[:, None, None], (RET_HEADS, 1, RET_DV))
    half = RET_DK // 2
    inv = ROPE_BASE ** (-np.arange(half, dtype=np.float64) / half)
    ang = np.arange(SEQ, dtype=np.float64)[:, None] * inv[None, :]
    cos = np.concatenate([np.cos(ang), np.cos(ang)], axis=1)
    sin = np.concatenate([-np.sin(ang), np.sin(ang)], axis=1)
    f32 = lambda a: jnp.asarray(np.ascontiguousarray(a), jnp.float32)
    return f32(decay), f32(q_dec), f32(k_dec), f32(blk_dec), f32(cos), f32(sin)


def _retention_kernel(q_ref, k_ref, v_ref, gate_ref, cos_ref, sin_ref, dec_ref, qd_ref, kd_ref,
                      bd_ref, o_ref, state_ref):
    @pl.when(pl.program_id(1) == 0)
    def _():
        state_ref[...] = jnp.zeros_like(state_ref)

    cos, sin = cos_ref[...], sin_ref[...]
    for h in range(RET_HEADS):
        ks = slice(h * RET_DK, (h + 1) * RET_DK)
        vs = slice(h * RET_DV, (h + 1) * RET_DV)
        q = q_ref[:, ks].astype(jnp.float32)
        k = k_ref[:, ks].astype(jnp.float32)
        qr = q * cos + pltpu.roll(q, RET_DK // 2, axis=1) * sin
        kr = (k * cos + pltpu.roll(k, RET_DK // 2, axis=1) * sin) * (RET_DK ** -0.5)
        v = v_ref[:, vs]

        scores = _dot_nt(_bf16(qr), _bf16(kr)) * dec_ref[h]
        intra = _dot(_bf16(scores), v)
        state = state_ref[h]
        inter = _dot(_bf16(qr * qd_ref[h]), _bf16(state))
        state_ref[h] = state * bd_ref[h] + _dot_tn(_bf16(kr * kd_ref[h]), v)

        o = intra + inter
        mu = jnp.mean(o, axis=-1, keepdims=True)
        cen = o - mu
        var = jnp.mean(cen * cen, axis=-1, keepdims=True)
        normed = cen * lax.rsqrt(var + EPS)
        o_ref[:, vs] = _bf16(_silu(gate_ref[:, vs].astype(jnp.float32)) * normed)


def _retention(proj):
    decay, q_dec, k_dec, blk_dec, cos, sin = _retention_tables()
    nblk = SEQ // RET_BLOCK
    row = lambda b, n: b * nblk + n
    whole = lambda b, n: (0, 0, 0)
    return pl.pallas_call(
        _retention_kernel,
        grid=(BATCH, nblk),
        in_specs=[
            pl.BlockSpec((RET_BLOCK, A_QK), lambda b, n: (row(b, n), 0)),
            pl.BlockSpec((RET_BLOCK, A_QK), lambda b, n: (row(b, n), 1)),
            pl.BlockSpec((RET_BLOCK, A_V), lambda b, n: (row(b, n), 2 * A_QK // A_V)),
            pl.BlockSpec((RET_BLOCK, A_V), lambda b, n: (row(b, n), 2 * A_QK // A_V + 1)),
            pl.BlockSpec((RET_BLOCK, RET_DK), lambda b, n: (n, 0)),
            pl.BlockSpec((RET_BLOCK, RET_DK), lambda b, n: (n, 0)),
            pl.BlockSpec((RET_HEADS, RET_BLOCK, RET_BLOCK), whole),
            pl.BlockSpec((RET_HEADS, RET_BLOCK, RET_DK), whole),
            pl.BlockSpec((RET_HEADS, RET_BLOCK, RET_DK), whole),
            pl.BlockSpec((RET_HEADS, 1, RET_DV), whole),
        ],
        out_specs=pl.BlockSpec((RET_BLOCK, A_V), lambda b, n: (row(b, n), 0)),
        out_shape=jax.ShapeDtypeStruct((TOKENS, A_V), jnp.bfloat16),
        scratch_shapes=[pltpu.VMEM((RET_HEADS, RET_DK, RET_DV), jnp.float32)],
        compiler_params=pltpu.CompilerParams(dimension_semantics=("arbitrary", "arbitrary")),
        name="retention",
    )(proj, proj, proj, proj, cos, sin, decay, q_dec, k_dec, blk_dec)


def _spatial_gate_kernel(u_ref, z_ref, lng_ref, lnb_ref, w_ref, b_ref, o_ref):
    u = _gelu_tanh(u_ref[...].astype(jnp.float32))
    z = _gelu_tanh(z_ref[...].astype(jnp.float32))
    mu = jnp.mean(z, axis=-1, keepdims=True)
    cen = z - mu
    var = jnp.mean(cen * cen, axis=-1, keepdims=True)
    zn = _bf16(cen * lax.rsqrt(var + EPS) * lng_ref[...] + lnb_ref[...])
    rows = lax.broadcasted_iota(jnp.int32, (SG_LEN, SG_LEN), 0)
    cols = lax.broadcasted_iota(jnp.int32, (SG_LEN, SG_LEN), 1)
    keep = (rows // CHUNK) >= (cols // CHUNK)
    for g in range(SG_GROUPS):
        cs = slice(g * SG_CH, (g + 1) * SG_CH)
        w = _bf16(jnp.where(keep, w_ref[g], 0.0))
        for blk in range(SG_TILE // SG_LEN):
            rs = slice(blk * SG_LEN, (blk + 1) * SG_LEN)
            mixed = _dot(w, zn[rs, cs]) + b_ref[g]
            o_ref[rs, cs] = _bf16(u[rs, cs] * mixed)


def _spatial_gate(proj, ln_g, ln_b, w_s, b_s):
    u_col = (2 * A_QK + 2 * A_V) // SG_WIDTH
    b_full = jnp.broadcast_to(b_s[:, :, None], (SG_GROUPS, SG_LEN, SG_CH))
    return pl.pallas_call(
        _spatial_gate_kernel,
        grid=(TOKENS // SG_TILE,),
        in_specs=[
            pl.BlockSpec((SG_TILE, SG_WIDTH), lambda i: (i, u_col)),
            pl.BlockSpec((SG_TILE, SG_WIDTH), lambda i: (i, u_col + 1)),
            pl.BlockSpec((1, SG_WIDTH), lambda i: (0, 0)),
            pl.BlockSpec((1, SG_WIDTH), lambda i: (0, 0)),
            pl.BlockSpec((SG_GROUPS, SG_LEN, SG_LEN), lambda i: (0, 0, 0)),
            pl.BlockSpec((SG_GROUPS, SG_LEN, SG_CH), lambda i: (0, 0, 0)),
        ],
        out_specs=pl.BlockSpec((SG_TILE, SG_WIDTH), lambda i: (i, 0)),
        out_shape=jax.ShapeDtypeStruct((TOKENS, SG_WIDTH), jnp.bfloat16),
        name="spatial_gate",
    )(proj, proj, ln_g.reshape(1, SG_WIDTH), ln_b.reshape(1, SG_WIDTH), w_s, b_full)


def _band_bias(rel_bias):
    period = ATT_QBLOCK + ATT_BAND
    m = np.arange(period)
    shift = np.where(m < ATT_BAND, m, m - period)
    rel = BAND_PAD - shift
    table = rel_bias[:, np.clip(rel, -(CHUNK - 1), MAX_REL) + (CHUNK - 1)].astype(jnp.float32)
    flat = jnp.tile(table, (1, ATT_QBLOCK))[:, :ATT_QBLOCK * (period - 1)]
    bias = flat.reshape(ATT_HEADS, ATT_QBLOCK, period - 1)[:, :, :ATT_BAND]
    i = np.arange(ATT_QBLOCK)[:, None]
    jj = np.arange(ATT_BAND)[None, :]
    in_band = (jj // CHUNK >= i // CHUNK) & (jj // CHUNK <= i // CHUNK + BAND_PAD // CHUNK)
    block = np.arange(ATT_LEAD + 1)[:, None, None]
    in_seq = (jj[None] + block * ATT_QBLOCK >= BAND_PAD) | (block == ATT_LEAD)
    keep = in_band[None] & in_seq
    return jnp.where(jnp.asarray(keep)[:, None], (bias * LOG2_E)[None], NEG_BIG)


def _head_rms(x, gain):
    head_of_lane = lax.broadcasted_iota(jnp.int32, x.shape, 1) // ATT_DH
    sq = x * x
    ms = jnp.zeros_like(x)
    for h in range(ATT_GROUP):
        in_head = head_of_lane == h
        ms = jnp.where(in_head, jnp.sum(jnp.where(in_head, sq, 0.0), axis=-1, keepdims=True), ms)
    return x * lax.rsqrt(ms * (1.0 / ATT_DH) + EPS) * gain


def _head_rms_mxu(x, gain, same_head):
    sq = x * x
    hi = _bf16(sq)
    lo = _bf16(sq - hi.astype(jnp.float32))
    groups = []
    for g in range(x.shape[1] // SEG_LANES):
        ls = slice(g * SEG_LANES, (g + 1) * SEG_LANES)
        groups.append(_dot(hi[:, ls], same_head) + _dot(lo[:, ls], same_head))
    ss = jnp.concatenate(groups, axis=-1)
    return x * lax.rsqrt(ss * (1.0 / ATT_DH) + EPS) * gain


def _band_attn_kernel(q_ref, k_ref, v_ref, bias_ref, qg_ref, kg_ref, seg_ref, o_ref,
                      kpad_ref, vpad_ref, s_ref, p_ref, den_ref):
    n = pl.program_id(2)
    pair = 2 * ATT_DH

    @pl.when(n == 0)
    def _():
        low_half = lax.broadcasted_iota(jnp.int32, (KV_PREP_ROWS, pair), 1) < ATT_DH
        zero = jnp.zeros((KV_PREP_ROWS, pair), jnp.bfloat16)
        for h in range(ATT_GROUP):
            kpad_ref[h, 0:BAND_PAD, :] = jnp.zeros((BAND_PAD, pair), jnp.bfloat16)
            vpad_ref[h, 0:BAND_PAD, :] = jnp.zeros((BAND_PAD, pair), jnp.bfloat16)
        for r in range(SEQ // KV_PREP_ROWS):
            rows = slice(r * KV_PREP_ROWS, (r + 1) * KV_PREP_ROWS)
            dst = slice(BAND_PAD + r * KV_PREP_ROWS, BAND_PAD + (r + 1) * KV_PREP_ROWS)
            kn = _bf16(_head_rms_mxu(k_ref[rows, :].astype(jnp.float32), kg_ref[...], seg_ref[...]))
            v = v_ref[rows, :]
            for h in range(ATT_GROUP):
                cols = slice((h // 2) * pair, (h // 2 + 1) * pair)
                own = low_half if h % 2 == 0 else jnp.logical_not(low_half)
                kpad_ref[h, dst, :] = jnp.where(own, kn[:, cols], zero)
                vpad_ref[h, dst, :] = jnp.where(own, v[:, cols], zero)

    q = _bf16(_head_rms(q_ref[...].astype(jnp.float32), qg_ref[...]) * (LOG2_E * ATT_DH ** -0.5))
    start = pl.multiple_of(n * ATT_QBLOCK, ATT_QBLOCK)
    for h in range(ATT_GROUP):
        q_pair = q[:, (h // 2) * pair:(h // 2 + 1) * pair]
        s_ref[h] = _dot_nt(q_pair, kpad_ref[h, pl.ds(start, ATT_BAND), :])

    for h in range(ATT_GROUP):
        s = s_ref[h] + bias_ref[h]
        m = jnp.max(s, axis=-1, keepdims=True)
        p = jnp.exp2(s - m)
        den_ref[h] = jnp.sum(p, axis=-1, keepdims=True)
        p_ref[h] = _bf16(p)
    outs = []
    for h0 in range(0, ATT_GROUP, 2):
        pv = [_dot(p_ref[h], vpad_ref[h, pl.ds(start, ATT_BAND), :]) * (1.0 / den_ref[h])
              for h in (h0, h0 + 1)]
        outs.append(pv[0] + pv[1])
    o_ref[...] = _bf16(jnp.concatenate(outs, axis=-1))


def _band_attention(qkv, q_g, k_g, rel_bias):
    bias = _band_bias(rel_bias)
    nq = SEQ // ATT_QBLOCK
    width = ATT_GROUP * ATT_DH
    n_groups = ATT_HEADS // ATT_GROUP
    qg = jnp.tile(q_g, ATT_GROUP).reshape(1, width)
    kg = jnp.tile(k_g, ATT_GROUP).reshape(1, width)
    lane_head = np.arange(SEG_LANES) // ATT_DH
    same_head = jnp.asarray(lane_head[:, None] == lane_head[None, :], jnp.bfloat16)
    return pl.pallas_call(
        _band_attn_kernel,
        grid=(BATCH, n_groups, nq),
        in_specs=[
            pl.BlockSpec((ATT_QBLOCK, width), lambda b, hg, n: (b * nq + n, hg)),
            pl.BlockSpec((SEQ, width), lambda b, hg, n: (b, n_groups + hg)),
            pl.BlockSpec((SEQ, width), lambda b, hg, n: (b, 2 * n_groups + hg)),
            pl.BlockSpec((None, ATT_GROUP, ATT_QBLOCK, ATT_BAND),
                         lambda b, hg, n: (jnp.minimum(n, ATT_LEAD), hg, 0, 0)),
            pl.BlockSpec((1, width), lambda b, hg, n: (0, 0)),
            pl.BlockSpec((1, width), lambda b, hg, n: (0, 0)),
            pl.BlockSpec((SEG_LANES, SEG_LANES), lambda b, hg, n: (0, 0)),
        ],
        out_specs=pl.BlockSpec((ATT_QBLOCK, width), lambda b, hg, n: (b * nq + n, hg)),
        out_shape=jax.ShapeDtypeStruct((TOKENS, D_MODEL), jnp.bfloat16),
        scratch_shapes=[pltpu.VMEM((ATT_GROUP, SEQ + BAND_PAD, 2 * ATT_DH), jnp.bfloat16),
                        pltpu.VMEM((ATT_GROUP, SEQ + BAND_PAD, 2 * ATT_DH), jnp.bfloat16),
                        pltpu.VMEM((ATT_GROUP, ATT_QBLOCK, ATT_BAND), jnp.float32),
                        pltpu.VMEM((ATT_GROUP, ATT_QBLOCK, ATT_BAND), jnp.bfloat16),
                        pltpu.VMEM((ATT_GROUP, ATT_QBLOCK, 1), jnp.float32)],
        compiler_params=pltpu.CompilerParams(
            dimension_semantics=("arbitrary", "arbitrary", "arbitrary")),
        name="band_attention",
    )(qkv, qkv, qkv, bias, qg, kg, same_head)


def _store_token_major(ref, value, lead=()):
    rows = value.shape[0]
    for c in range(ROW_TILES):
        ref[(*lead, pl.ds(c, rows, stride=ROW_TILES), slice(None))] = value[:, c * 128:(c + 1) * 128]


def _load_token_major(ref, rows, c, lead=()):
    return ref[(*lead, pl.ds(c, rows, stride=ROW_TILES), slice(None))]


def _split_bf16(x):
    hi = _bf16(x)
    lo = _bf16(x - hi.astype(jnp.float32))
    return hi, lo


def _router_kernel(x_ref, g_ref, sc_ref, sh_ref, rw_ref, rb_ref,
                   h_ref, idx_ref, rank_ref, gate_ref, cnt_ref, carry_ref):
    @pl.when(pl.program_id(0) == 0)
    def _():
        carry_ref[...] = jnp.zeros_like(carry_ref)

    h = _norm_mod(x_ref[...], g_ref[...], sc_ref[0], sh_ref[0])
    _store_token_major(h_ref, h)
    h_hi, h_lo = _split_bf16(h)
    w_hi, w_lo = _split_bf16(rw_ref[...])
    logits = _dot_nt(w_hi, h_hi) + _dot_nt(w_hi, h_lo) + _dot_nt(w_lo, h_hi) + rb_ref[...]

    expert = lax.broadcasted_iota(jnp.int32, logits.shape, 0).astype(jnp.float32)
    work = logits
    vals, idxs, sels = [], [], []
    for _ in range(TOP_K):
        m = jnp.max(work, axis=0, keepdims=True)
        pick = jnp.min(jnp.where(work == m, expert, float(N_EXPERTS)), axis=0, keepdims=True)
        sel = expert == pick
        work = jnp.where(sel, -jnp.inf, work)
        vals.append(m)
        idxs.append(pick)
        sels.append(sel)
    exps = [jnp.exp(v - vals[0]) for v in vals]
    denom = exps[0] + exps[1] + exps[2] + exps[3]
    gate_ref[...] = jnp.concatenate([e / denom for e in exps], axis=0)
    idx_ref[...] = jnp.concatenate(idxs, axis=0).astype(jnp.int32)

    chosen = jnp.zeros(logits.shape, jnp.float32)
    for sel in sels:
        chosen = jnp.where(sel, 1.0, chosen)
    tile = logits.shape[1]
    earlier = (lax.broadcasted_iota(jnp.int32, (tile, tile), 0)
               < lax.broadcasted_iota(jnp.int32, (tile, tile), 1))
    before = _dot(_bf16(chosen), jnp.where(earlier, 1.0, 0.0).astype(jnp.bfloat16))
    rank_full = before + carry_ref[...]
    ranks = [jnp.sum(jnp.where(sel, rank_full, 0.0), axis=0, keepdims=True) for sel in sels]
    rank_ref[...] = jnp.concatenate(ranks, axis=0).astype(jnp.int32)
    carry = carry_ref[...] + jnp.sum(chosen, axis=1, keepdims=True)
    carry_ref[...] = carry
    cnt_ref[...] = jnp.broadcast_to(carry, cnt_ref.shape)


def _router(x, g, sc, sh, router_w, router_b):
    tiles_per_batch = SEQ // ROUTE_TILE
    lane_out = lambda dt: jax.ShapeDtypeStruct((TOP_K, TOKENS), dt)
    lane_spec = pl.BlockSpec((TOP_K, ROUTE_TILE), lambda i: (0, i))
    return pl.pallas_call(
        _router_kernel,
        grid=(TOKENS // ROUTE_TILE,),
        in_specs=[
            pl.BlockSpec((ROUTE_TILE, D_MODEL), lambda i: (i, 0)),
            pl.BlockSpec((1, D_MODEL), lambda i: (0, 0)),
            pl.BlockSpec((1, 1, D_MODEL), lambda i: (i // tiles_per_batch, 0, 0)),
            pl.BlockSpec((1, 1, D_MODEL), lambda i: (i // tiles_per_batch, 0, 0)),
            pl.BlockSpec((N_EXPERTS, D_MODEL), lambda i: (0, 0)),
            pl.BlockSpec((N_EXPERTS, 1), lambda i: (0, 0)),
        ],
        out_specs=[
            pl.BlockSpec((ROUTE_TILE * ROW_TILES, 128), lambda i: (i, 0)),
            lane_spec, lane_spec, lane_spec,
            pl.BlockSpec((N_EXPERTS, 128), lambda i: (0, 0)),
        ],
        out_shape=[
            jax.ShapeDtypeStruct((TOKENS * ROW_TILES, 128), jnp.float32),
            lane_out(jnp.int32), lane_out(jnp.int32), lane_out(jnp.float32),
            jax.ShapeDtypeStruct((N_EXPERTS, 128), jnp.float32),
        ],
        scratch_shapes=[pltpu.VMEM((N_EXPERTS, 1), jnp.float32)],
        compiler_params=pltpu.CompilerParams(dimension_semantics=("arbitrary",)),
        name="moe_router",
    )(x, g.reshape(1, D_MODEL), sc, sh, router_w.T, router_b.reshape(N_EXPERTS, 1))


def _row_copy(src_ref, src_row, dst_ref, dst_row, sem):
    src = pl.multiple_of(src_row * ROW_TILES, ROW_TILES)
    dst = pl.multiple_of(dst_row * ROW_TILES, ROW_TILES)
    return pltpu.make_async_copy(src_ref.at[pl.ds(src, ROW_TILES), :],
                                 dst_ref.at[pl.ds(dst, ROW_TILES), :], sem)


def _zero_block_copy(zero_ref, xs_ref, row, sem):
    row = pl.multiple_of(row * ROW_TILES, MOE_BLOCK * ROW_TILES)
    return pltpu.make_async_copy(zero_ref, xs_ref.at[pl.ds(row, MOE_BLOCK * ROW_TILES), :], sem)


def _dispatch_kernel(zstart_ref, nused_ref, pos_ref, h_ref, xs_ref, zero_ref, zsem, sem):
    step = pl.program_id(0)

    @pl.when(step == 0)
    def _():
        zero_ref[...] = jnp.zeros_like(zero_ref)

        def each_block(fn):
            def expert_last(e, carry):
                @pl.when(zstart_ref[e] >= 0)
                def _():
                    fn(_zero_block_copy(zero_ref, xs_ref, zstart_ref[e], zsem))
                return carry

            def unused(blk, carry):
                fn(_zero_block_copy(zero_ref, xs_ref, blk * MOE_BLOCK, zsem))
                return carry

            lax.fori_loop(0, N_EXPERTS, expert_last, 0)
            lax.fori_loop(nused_ref[0], MOE_NBLOCKS, unused, 0)

        each_block(lambda copy: copy.start())
        each_block(lambda copy: copy.wait())

    def issue(i, carry):
        for u in range(ISSUE_UNROLL):
            t = i * ISSUE_UNROLL + u
            for k in range(TOP_K):
                _row_copy(h_ref, t, xs_ref, pos_ref[t * TOP_K + k], sem).start(priority=k % 2)
        return carry

    lax.fori_loop(0, DISPATCH_TILE // ISSUE_UNROLL, issue, 0)

    for _ in range(TOP_K):
        pltpu.make_async_copy(h_ref, xs_ref.at[pl.ds(0, DISPATCH_TILE * ROW_TILES), :], sem).wait()


def _dispatch(h, pos_flat, zstart, n_used):
    grid_spec = pltpu.PrefetchScalarGridSpec(
        num_scalar_prefetch=2,
        grid=(TOKENS // DISPATCH_TILE,),
        in_specs=[
            pl.BlockSpec((DISPATCH_TILE * TOP_K,), lambda i, zs, nu: (i,),
                         memory_space=pltpu.SMEM),
            pl.BlockSpec((DISPATCH_TILE * ROW_TILES, 128), lambda i, zs, nu: (i, 0)),
        ],
        out_specs=pl.BlockSpec(memory_space=pl.ANY),
        scratch_shapes=[pltpu.VMEM((MOE_BLOCK * ROW_TILES, 128), jnp.float32),
                        pltpu.SemaphoreType.DMA, pltpu.SemaphoreType.DMA],
    )
    return pl.pallas_call(
        _dispatch_kernel,
        grid_spec=grid_spec,
        out_shape=jax.ShapeDtypeStruct((MOE_CAP * ROW_TILES, 128), jnp.float32),
        compiler_params=pltpu.CompilerParams(dimension_semantics=("arbitrary",)),
        name="moe_dispatch",
    )(zstart, n_used, pos_flat, h)


def _expert_kernel(be_ref, nused_ref, next_ref, xs_ref, b1_ref, b2_ref, w1_hbm, w2_hbm, ys_ref,
                   w1f_ref, w2f_ref, w1b_ref, w2b_ref, sems, *, layer):
    i = pl.program_id(0)
    expert = be_ref[i]
    used = i < nused_ref[0]
    new_expert = jnp.logical_or(i == 0, expert != be_ref[jnp.maximum(i - 1, 0)])

    def weight_copies(e):
        return (pltpu.make_async_copy(w1_hbm.at[layer, e], w1f_ref, sems.at[0]),
                pltpu.make_async_copy(w2_hbm.at[layer, e], w2f_ref, sems.at[1]))

    @pl.when(jnp.logical_and(used, new_expert))
    def _():
        @pl.when(i == 0)
        def _():
            for copy in weight_copies(expert):
                copy.start()

        for copy in weight_copies(expert):
            copy.wait()
        for r in range(D_MODEL // 256):
            rows = slice(r * 256, (r + 1) * 256)
            w1b_ref[rows, :] = _bf16(w1f_ref[rows, :])
            w2b_ref[rows, :] = _bf16(w2f_ref[rows, :])

        @pl.when(next_ref[i] >= 0)
        def _():
            for copy in weight_copies(next_ref[i]):
                copy.start()

    @pl.when(used)
    def _():
        x = _bf16(jnp.concatenate([_load_token_major(xs_ref, MOE_BLOCK, c)
                                   for c in range(ROW_TILES)], axis=-1))
        glu = _dot(x, w1b_ref[:, :D_FF]) + b1_ref[0][:, :D_FF]
        lin = _dot(x, w1b_ref[:, D_FF:]) + b1_ref[0][:, D_FF:]
        glu = jnp.minimum(glu, SWIGLU_LIMIT)
        lin = jnp.clip(lin, -SWIGLU_LIMIT, SWIGLU_LIMIT)
        act = glu * (1.0 / (1.0 + jnp.exp(-SWIGLU_ALPHA * glu))) * (lin + 1.0)
        _store_token_major(ys_ref, _dot(_bf16(act), w2b_ref[...]) + b2_ref[0])

    @pl.when(jnp.logical_not(used))
    def _():
        ys_ref[...] = jnp.zeros_like(ys_ref)


def _experts(xs, block_e, n_used, next_e, layer, w1, b1, w2, b2):
    grid_spec = pltpu.PrefetchScalarGridSpec(
        num_scalar_prefetch=3,
        grid=(MOE_NBLOCKS,),
        in_specs=[
            pl.BlockSpec((MOE_BLOCK * ROW_TILES, 128),
                         lambda i, be, nu, nx: (jnp.minimum(i, nu[0] - 1), 0)),
            pl.BlockSpec((None, 1, 1, 2 * D_FF), lambda i, be, nu, nx: (layer, be[i], 0, 0)),
            pl.BlockSpec((None, 1, 1, D_MODEL), lambda i, be, nu, nx: (layer, be[i], 0, 0)),
            pl.BlockSpec(memory_space=pl.ANY),
            pl.BlockSpec(memory_space=pl.ANY),
        ],
        out_specs=pl.BlockSpec((MOE_BLOCK * ROW_TILES, 128), lambda i, be, nu, nx: (i, 0)),
        scratch_shapes=[pltpu.VMEM((D_MODEL, 2 * D_FF), jnp.float32),
                        pltpu.VMEM((D_FF, D_MODEL), jnp.float32),
                        pltpu.VMEM((D_MODEL, 2 * D_FF), jnp.bfloat16),
                        pltpu.VMEM((D_FF, D_MODEL), jnp.bfloat16),
                        pltpu.SemaphoreType.DMA((2,))],
    )
    return pl.pallas_call(
        functools.partial(_expert_kernel, layer=layer),
        grid_spec=grid_spec,
        out_shape=jax.ShapeDtypeStruct((MOE_CAP * ROW_TILES, 128), jnp.float32),
        compiler_params=pltpu.CompilerParams(dimension_semantics=("arbitrary",),
                                             vmem_limit_bytes=VMEM_LIMIT_V7X),
        name="moe_experts",
    )(block_e, n_used, next_e, xs, b1.reshape(DEPTH, N_EXPERTS, 1, 2 * D_FF),
      b2.reshape(DEPTH, N_EXPERTS, 1, D_MODEL), w1, w2)


def _combine_kernel(pos_ref, pos_next_ref, x_ref, gates_ref, g2_ref, ys_ref, o_ref, buf_ref, sems):
    step = pl.program_id(0)
    n_steps = pl.num_programs(0)
    slot = step % 2

    def gather(p_ref, dst_slot):
        def issue(i, carry):
            for u in range(ISSUE_UNROLL):
                t = i * ISSUE_UNROLL + u
                for k in range(TOP_K):
                    _row_copy(ys_ref, p_ref[t * TOP_K + k], buf_ref.at[dst_slot, k], t,
                              sems.at[dst_slot]).start(priority=k % 2)
            return carry

        lax.fori_loop(0, COMBINE_TILE // ISSUE_UNROLL, issue, 0)

    @pl.when(step == 0)
    def _():
        gather(pos_ref, 0)

    @pl.when(step + 1 < n_steps)
    def _():
        gather(pos_next_ref, 1 - slot)

    for k in range(TOP_K):
        pltpu.make_async_copy(ys_ref.at[pl.ds(0, COMBINE_TILE * ROW_TILES), :],
                              buf_ref.at[slot, k], sems.at[slot]).wait()

    gates = gates_ref[...]
    gate_cols = [jnp.broadcast_to(gates[:, k:k + 1], (COMBINE_TILE, 128)) for k in range(TOP_K)]
    for c in range(ROW_TILES):
        cols = slice(c * 128, (c + 1) * 128)
        y = gate_cols[0] * _load_token_major(buf_ref, COMBINE_TILE, c, (slot, 0))
        for k in range(1, TOP_K):
            y = y + gate_cols[k] * _load_token_major(buf_ref, COMBINE_TILE, c, (slot, k))
        o_ref[:, cols] = x_ref[:, cols] + g2_ref[0][:, cols] * y


def _combine(x, pos_flat, gates_tok, g2, ys):
    tiles_per_batch = SEQ // COMBINE_TILE
    n_steps = TOKENS // COMBINE_TILE
    return pl.pallas_call(
        _combine_kernel,
        grid=(n_steps,),
        in_specs=[
            pl.BlockSpec((COMBINE_TILE * TOP_K,), lambda i: (i,), memory_space=pltpu.SMEM),
            pl.BlockSpec((COMBINE_TILE * TOP_K,), lambda i: (jnp.minimum(i + 1, n_steps - 1),),
                         memory_space=pltpu.SMEM),
            pl.BlockSpec((COMBINE_TILE, D_MODEL), lambda i: (i, 0)),
            pl.BlockSpec((COMBINE_TILE, TOP_K), lambda i: (i, 0)),
            pl.BlockSpec((1, 1, D_MODEL), lambda i: (i // tiles_per_batch, 0, 0)),
            pl.BlockSpec(memory_space=pl.ANY),
        ],
        out_specs=pl.BlockSpec((COMBINE_TILE, D_MODEL), lambda i: (i, 0)),
        out_shape=jax.ShapeDtypeStruct((TOKENS, D_MODEL), jnp.float32),
        scratch_shapes=[pltpu.VMEM((2, TOP_K, COMBINE_TILE * ROW_TILES, 128), jnp.float32),
                        pltpu.SemaphoreType.DMA((2,))],
        compiler_params=pltpu.CompilerParams(dimension_semantics=("arbitrary",),
                                             vmem_limit_bytes=VMEM_LIMIT_V7X),
        name="moe_combine",
    )(pos_flat, pos_flat, x, gates_tok, g2, ys)


def _moe_layer(x, g, sc, sh, g2, router_w, router_b, layer, w1, b1, w2, b2):
    h, idx, rank, gates, cnt = _router(x, g, sc, sh, router_w, router_b)
    counts = cnt[:, 0].astype(jnp.int32)
    padded = ((counts + MOE_BLOCK - 1) // MOE_BLOCK) * MOE_BLOCK
    pends = jnp.cumsum(padded)
    pstarts = pends - padded
    experts = jnp.arange(N_EXPERTS, dtype=jnp.int32)[:, None, None]
    pos = rank + jnp.sum(jnp.where(idx[None] == experts, pstarts[:, None, None], 0), axis=0)
    pos_flat = pos.T.reshape(-1)
    block_start = jnp.arange(MOE_NBLOCKS, dtype=jnp.int32) * MOE_BLOCK
    block_e = jnp.minimum(jnp.sum(block_start[:, None] >= pends[None, :], axis=1),
                          N_EXPERTS - 1).astype(jnp.int32)
    n_used = (pends[-1:] // MOE_BLOCK).astype(jnp.int32)
    zstart = jnp.where(padded > 0, pends - MOE_BLOCK, -1).astype(jnp.int32)
    ids = jnp.arange(N_EXPERTS, dtype=jnp.int32)
    later_used = jnp.logical_and(ids[None, :] > ids[:, None], (padded > 0)[None, :])
    next_used = jnp.min(jnp.where(later_used, ids[None, :], N_EXPERTS), axis=1)
    next_used = jnp.where(next_used < N_EXPERTS, next_used, -1)
    next_e = jnp.sum(jnp.where(block_e[:, None] == ids[None, :], next_used[None, :], 0),
                     axis=1).astype(jnp.int32)
    xs = _dispatch(h, pos_flat, zstart, n_used)
    ys = _experts(xs, block_e, n_used, next_e, layer, w1, b1, w2, b2)
    return _combine(x, pos_flat, gates.T, g2, ys)


def kernel(x, c, ada_w, ada_b, norm_mix_g, norm_ffn_g, ev_w_in, ev_w_out, sg_ln_g, sg_ln_b, sg_w, sg_b, od_w_in, od_w_out, od_q_g, od_k_g, od_rel_bias, moe_router_w, moe_router_b, moe_w1, moe_b1, moe_w2, moe_b2):
    mod = _ada_mod(c, ada_w, ada_b)
    xt = x.reshape(TOKENS, D_MODEL)
    for l in range(DEPTH):
        sh1, sc1, g1, sh2, sc2, g2 = [m.reshape(BATCH, 1, D_MODEL)
                                      for m in jnp.split(mod[l], 6, axis=-1)]
        i = l // 2
        if l % 2 == 0:
            proj = _norm_proj(xt, norm_mix_g[l], sc1, sh1, _bf16(ev_w_in[i]))
            a_out = _retention(proj)
            b_out = _spatial_gate(proj, sg_ln_g[i], sg_ln_b[i], sg_w[i], sg_b[i])
            xt = _out_proj([a_out, b_out], _bf16(ev_w_out[i]), xt, g1)
        else:
            qkv = _norm_proj(xt, norm_mix_g[l], sc1, sh1, _bf16(od_w_in[i]))
            att = _band_attention(qkv, od_q_g[i], od_k_g[i], od_rel_bias[i])
            xt = _out_proj([att], _bf16(od_w_out[i]), xt, g1)
        xt = _moe_layer(xt, norm_ffn_g[l], sc2, sh2, g2, moe_router_w[l], moe_router_b[l],
                        l, moe_w1, moe_b1, moe_w2, moe_b2)
    return xt.reshape(BATCH, SEQ, D_MODEL)
```

```python
import functools
import math

import numpy as np
import jax
import jax.numpy as jnp
from jax import lax
from jax.experimental import pallas as pl
from jax.experimental.pallas import tpu as pltpu

D_MODEL = 1024
BATCH = 8
SEQ = 2048
DEPTH = 4
TOKENS = BATCH * SEQ
CHUNK = 64
EPS = 1e-6

RET_HEADS = 4
RET_DK = 128
RET_DV = 256
ROPE_BASE = 10000.0
A_QK = RET_HEADS * RET_DK
A_V = RET_HEADS * RET_DV
SG_GROUPS = 4
SG_CH = 128
SG_WIDTH = SG_GROUPS * SG_CH
SG_LEN = 128
EVEN_IN = 2 * A_QK + 2 * A_V + 2 * SG_WIDTH
EVEN_MIX = A_V + SG_WIDTH

ATT_HEADS = 16
ATT_DH = 64
BAND_PAD = 8 * CHUNK
MAX_REL = 256
REL_SIZE = (CHUNK - 1) + MAX_REL + 1

N_EXPERTS = 32
TOP_K = 4
D_FF = D_MODEL
SWIGLU_ALPHA = 1.702
SWIGLU_LIMIT = 7.0

ROW_TILE = 1024
COL_CHUNK = 512
RET_BLOCK = 256
ATT_QBLOCK = 128
ATT_BAND = ATT_QBLOCK + BAND_PAD
ATT_GROUP = 8
ATT_LEAD = BAND_PAD // ATT_QBLOCK
KV_PREP_ROWS = 256
SEG_LANES = 256
SG_TILE = 512
ROUTE_TILE = 512
MOE_BLOCK = 256
MOE_CAP = TOKENS * TOP_K + N_EXPERTS * MOE_BLOCK
MOE_NBLOCKS = MOE_CAP // MOE_BLOCK
DISPATCH_TILE = 512
COMBINE_TILE = 256
ISSUE_UNROLL = 8
VMEM_LIMIT_V7X = 56 * 1024 * 1024
ROW_TILES = D_MODEL // 128

NEG_BIG = -1e30
LOG2_E = math.log2(math.e)


def _silu(x):
    return x * (1.0 / (1.0 + jnp.exp(-x)))


def _gelu_tanh(x):
    return 0.5 * x * (1.0 + jnp.tanh(math.sqrt(2.0 / math.pi) * (x + 0.044715 * (x * x * x))))


def _bf16(x):
    return x.astype(jnp.bfloat16)


def _dot(a, b):
    return jnp.dot(a, b, preferred_element_type=jnp.float32)


def _dot_nt(a, b):
    return lax.dot_general(a, b, (((1,), (1,)), ((), ())), preferred_element_type=jnp.float32)


def _dot_tn(a, b):
    return lax.dot_general(a, b, (((0,), (0,)), ((), ())), preferred_element_type=jnp.float32)


def _norm_mod(x, g, sc, sh):
    y = x * lax.rsqrt(jnp.mean(x * x, axis=-1, keepdims=True) + EPS)
    return (y * g) * (1.0 + sc) + sh


def _ada_kernel(c_ref, w_ref, b_ref, o_ref):
    c_act = _silu(c_ref[...])
    o_ref[0] = _dot(c_act, w_ref[0]) + b_ref[0]


def _ada_mod(c, ada_w, ada_b):
    n_col = 6
    return pl.pallas_call(
        _ada_kernel,
        grid=(DEPTH, n_col),
        in_specs=[
            pl.BlockSpec((BATCH, D_MODEL), lambda l, j: (0, 0)),
            pl.BlockSpec((1, D_MODEL, D_MODEL), lambda l, j: (l, 0, j)),
            pl.BlockSpec((1, 1, D_MODEL), lambda l, j: (l, 0, j)),
        ],
        out_specs=pl.BlockSpec((1, BATCH, D_MODEL), lambda l, j: (l, 0, j)),
        out_shape=jax.ShapeDtypeStruct((DEPTH, BATCH, 6 * D_MODEL), jnp.float32),
        name="ada_mod",
    )(c, ada_w, ada_b.reshape(DEPTH, 1, 6 * D_MODEL))


def _norm_proj_kernel(x_ref, g_ref, sc_ref, sh_ref, w_ref, o_ref):
    h = _bf16(_norm_mod(x_ref[...], g_ref[...], sc_ref[0], sh_ref[0]))
    n_out = o_ref.shape[1]
    for j in range(n_out // COL_CHUNK):
        cols = slice(j * COL_CHUNK, (j + 1) * COL_CHUNK)
        o_ref[:, cols] = _bf16(_dot(h, w_ref[:, cols]))


def _norm_proj(x, g, sc, sh, w_bf16):
    n_out = w_bf16.shape[1]
    tiles_per_batch = SEQ // ROW_TILE
    return pl.pallas_call(
        _norm_proj_kernel,
        grid=(TOKENS // ROW_TILE,),
        in_specs=[
            pl.BlockSpec((ROW_TILE, D_MODEL), lambda i: (i, 0)),
            pl.BlockSpec((1, D_MODEL), lambda i: (0, 0)),
            pl.BlockSpec((1, 1, D_MODEL), lambda i: (i // tiles_per_batch, 0, 0)),
            pl.BlockSpec((1, 1, D_MODEL), lambda i: (i // tiles_per_batch, 0, 0)),
            pl.BlockSpec((D_MODEL, n_out), lambda i: (0, 0)),
        ],
        out_specs=pl.BlockSpec((ROW_TILE, n_out), lambda i: (i, 0)),
        out_shape=jax.ShapeDtypeStruct((TOKENS, n_out), jnp.bfloat16),
        compiler_params=pltpu.CompilerParams(vmem_limit_bytes=VMEM_LIMIT_V7X),
        name="norm_proj",
    )(x, g.reshape(1, D_MODEL), sc, sh, w_bf16)


def _out_proj_kernel(*refs, widths):
    part_refs = refs[:len(widths)]
    w_ref, x_ref, g_ref, o_ref = refs[len(widths):]
    for j in range(D_MODEL // COL_CHUNK):
        cols = slice(j * COL_CHUNK, (j + 1) * COL_CHUNK)
        acc = None
        row0 = 0
        for p_ref, width in zip(part_refs, widths):
            term = _dot(p_ref[...], w_ref[row0:row0 + width, cols])
            acc = term if acc is None else acc + term
            row0 += width
        o_ref[:, cols] = x_ref[:, cols] + g_ref[0][:, cols] * acc


def _out_proj(parts, w_bf16, x, gate):
    widths = tuple(p.shape[1] for p in parts)
    tiles_per_batch = SEQ // ROW_TILE
    in_specs = [pl.BlockSpec((ROW_TILE, width), lambda i: (i, 0)) for width in widths]
    in_specs += [
        pl.BlockSpec((sum(widths), D_MODEL), lambda i: (0, 0)),
        pl.BlockSpec((ROW_TILE, D_MODEL), lambda i: (i, 0)),
        pl.BlockSpec((1, 1, D_MODEL), lambda i: (i // tiles_per_batch, 0, 0)),
    ]
    return pl.pallas_call(
        functools.partial(_out_proj_kernel, widths=widths),
        grid=(TOKENS // ROW_TILE,),
        in_specs=in_specs,
        out_specs=pl.BlockSpec((ROW_TILE, D_MODEL), lambda i: (i, 0)),
        out_shape=jax.ShapeDtypeStruct((TOKENS, D_MODEL), jnp.float32),
        compiler_params=pltpu.CompilerParams(vmem_limit_bytes=VMEM_LIMIT_V7X),
        name="out_proj",
    )(*parts, w_bf16, x, gate)


def _retention_tables():
    heads = np.arange(RET_HEADS, dtype=np.float64)
    log_g = np.log1p(-np.exp2(-5.0 - heads))
    idx = np.arange(RET_BLOCK, dtype=np.float64)
    diff = idx[:, None] - idx[None, :]
    ci, cj = (idx // CHUNK)[:, None], (idx // CHUNK)[None, :]
    expo = np.where(ci == cj, np.abs(diff), diff)
    decay = np.where(cj <= ci, np.exp(log_g[:, None, None] * expo[None]), 0.0)
    q_dec = np.exp(log_g[:, None] * (idx[None, :] + 1.0))
    k_dec = np.exp(log_g[:, None] * (RET_BLOCK - 1.0 - idx[None, :]))
    blk_dec = np.exp(log_g * RET_BLOCK)
    q_dec = np.broadcast_to(q_dec[:, :, None], (RET_HEADS, RET_BLOCK, RET_DK))
    k_dec = np.broadcast_to(k_dec[:, :, None], (RET_HEADS, RET_BLOCK, RET_DK))
    blk_dec = np.broadcast_to(blk_dec[:, None, None], (RET_HEADS, 1, RET_DV))
    half = RET_DK // 2
    inv = ROPE_BASE ** (-np.arange(half, dtype=np.float64) / half)
    ang = np.arange(SEQ, dtype=np.float64)[:, None] * inv[None, :]
    cos = np.concatenate([np.cos(ang), np.cos(ang)], axis=1)
    sin = np.concatenate([-np.sin(ang), np.sin(ang)], axis=1)
    f32 = lambda a: jnp.asarray(np.ascontiguousarray(a), jnp.float32)
    return f32(decay), f32(q_dec), f32(k_dec), f32(blk_dec), f32(cos), f32(sin)


def _retention_kernel(q_ref, k_ref, v_ref, gate_ref, cos_ref, sin_ref, dec_ref, qd_ref, kd_ref,
                      bd_ref, o_ref, state_ref):
    @pl.when(pl.program_id(1) == 0)
    def _():
        state_ref[...] = jnp.zeros_like(state_ref)

    cos, sin = cos_ref[...], sin_ref[...]
    for h in range(RET_HEADS):
        ks = slice(h * RET_DK, (h + 1) * RET_DK)
        vs = slice(h * RET_DV, (h + 1) * RET_DV)
        q = q_ref[:, ks].astype(jnp.float32)
        k = k_ref[:, ks].astype(jnp.float32)
        qr = q * cos + pltpu.roll(q, RET_DK // 2, axis=1) * sin
        kr = (k * cos + pltpu.roll(k, RET_DK // 2, axis=1) * sin) * (RET_DK ** -0.5)
        v = v_ref[:, vs]

        scores = _dot_nt(_bf16(qr), _bf16(kr)) * dec_ref[h]
        intra = _dot(_bf16(scores), v)
        state = state_ref[h]
        inter = _dot(_bf16(qr * qd_ref[h]), _bf16(state))
        state_ref[h] = state * bd_ref[h] + _dot_tn(_bf16(kr * kd_ref[h]), v)

        o = intra + inter
        mu = jnp.mean(o, axis=-1, keepdims=True)
        cen = o - mu
        var = jnp.mean(cen * cen, axis=-1, keepdims=True)
        normed = cen * lax.rsqrt(var + EPS)
        o_ref[:, vs] = _bf16(_silu(gate_ref[:, vs].astype(jnp.float32)) * normed)


def _retention(proj):
    decay, q_dec, k_dec, blk_dec, cos, sin = _retention_tables()
    nblk = SEQ // RET_BLOCK
    row = lambda b, n: b * nblk + n
    whole = lambda b, n: (0, 0, 0)
    return pl.pallas_call(
        _retention_kernel,
        grid=(BATCH, nblk),
        in_specs=[
            pl.BlockSpec((RET_BLOCK, A_QK), lambda b, n: (row(b, n), 0)),
            pl.BlockSpec((RET_BLOCK, A_QK), lambda b, n: (row(b, n), 1)),
            pl.BlockSpec((RET_BLOCK, A_V), lambda b, n: (row(b, n), 2 * A_QK // A_V)),
            pl.BlockSpec((RET_BLOCK, A_V), lambda b, n: (row(b, n), 2 * A_QK // A_V + 1)),
            pl.BlockSpec((RET_BLOCK, RET_DK), lambda b, n: (n, 0)),
            pl.BlockSpec((RET_BLOCK, RET_DK), lambda b, n: (n, 0)),
            pl.BlockSpec((RET_HEADS, RET_BLOCK, RET_BLOCK), whole),
            pl.BlockSpec((RET_HEADS, RET_BLOCK, RET_DK), whole),
            pl.BlockSpec((RET_HEADS, RET_BLOCK, RET_DK), whole),
            pl.BlockSpec((RET_HEADS, 1, RET_DV), whole),
        ],
        out_specs=pl.BlockSpec((RET_BLOCK, A_V), lambda b, n: (row(b, n), 0)),
        out_shape=jax.ShapeDtypeStruct((TOKENS, A_V), jnp.bfloat16),
        scratch_shapes=[pltpu.VMEM((RET_HEADS, RET_DK, RET_DV), jnp.float32)],
        compiler_params=pltpu.CompilerParams(dimension_semantics=("arbitrary", "arbitrary")),
        name="retention",
    )(proj, proj, proj, proj, cos, sin, decay, q_dec, k_dec, blk_dec)


def _spatial_gate_kernel(u_ref, z_ref, lng_ref, lnb_ref, w_ref, b_ref, o_ref):
    u = _gelu_tanh(u_ref[...].astype(jnp.float32))
    z = _gelu_tanh(z_ref[...].astype(jnp.float32))
    mu = jnp.mean(z, axis=-1, keepdims=True)
    cen = z - mu
    var = jnp.mean(cen * cen, axis=-1, keepdims=True)
    zn = _bf16(cen * lax.rsqrt(var + EPS) * lng_ref[...] + lnb_ref[...])
    rows = lax.broadcasted_iota(jnp.int32, (SG_LEN, SG_LEN), 0)
    cols = lax.broadcasted_iota(jnp.int32, (SG_LEN, SG_LEN), 1)
    keep = (rows // CHUNK) >= (cols // CHUNK)
    for g in range(SG_GROUPS):
        cs = slice(g * SG_CH, (g + 1) * SG_CH)
        w = _bf16(jnp.where(keep, w_ref[g], 0.0))
        for blk in range(SG_TILE // SG_LEN):
            rs = slice(blk * SG_LEN, (blk + 1) * SG_LEN)
            mixed = _dot(w, zn[rs, cs]) + b_ref[g]
            o_ref[rs, cs] = _bf16(u[rs, cs] * mixed)


def _spatial_gate(proj, ln_g, ln_b, w_s, b_s):
    u_col = (2 * A_QK + 2 * A_V) // SG_WIDTH
    b_full = jnp.broadcast_to(b_s[:, :, None], (SG_GROUPS, SG_LEN, SG_CH))
    return pl.pallas_call(
        _spatial_gate_kernel,
        grid=(TOKENS // SG_TILE,),
        in_specs=[
            pl.BlockSpec((SG_TILE, SG_WIDTH), lambda i: (i, u_col)),
            pl.BlockSpec((SG_TILE, SG_WIDTH), lambda i: (i, u_col + 1)),
            pl.BlockSpec((1, SG_WIDTH), lambda i: (0, 0)),
            pl.BlockSpec((1, SG_WIDTH), lambda i: (0, 0)),
            pl.BlockSpec((SG_GROUPS, SG_LEN, SG_LEN), lambda i: (0, 0, 0)),
            pl.BlockSpec((SG_GROUPS, SG_LEN, SG_CH), lambda i: (0, 0, 0)),
        ],
        out_specs=pl.BlockSpec((SG_TILE, SG_WIDTH), lambda i: (i, 0)),
        out_shape=jax.ShapeDtypeStruct((TOKENS, SG_WIDTH), jnp.bfloat16),
        name="spatial_gate",
    )(proj, proj, ln_g.reshape(1, SG_WIDTH), ln_b.reshape(1, SG_WIDTH), w_s, b_full)


def _band_bias(rel_bias):
    period = ATT_QBLOCK + ATT_BAND
    m = np.arange(period)
    shift = np.where(m < ATT_BAND, m, m - period)
    rel = BAND_PAD - shift
    table = rel_bias[:, np.clip(rel, -(CHUNK - 1), MAX_REL) + (CHUNK - 1)].astype(jnp.float32)
    flat = jnp.tile(table, (1, ATT_QBLOCK))[:, :ATT_QBLOCK * (period - 1)]
    bias = flat.reshape(ATT_HEADS, ATT_QBLOCK, period - 1)[:, :, :ATT_BAND]
    i = np.arange(ATT_QBLOCK)[:, None]
    jj = np.arange(ATT_BAND)[None, :]
    in_band = (jj // CHUNK >= i // CHUNK) & (jj // CHUNK <= i // CHUNK + BAND_PAD // CHUNK)
    block = np.arange(ATT_LEAD + 1)[:, None, None]
    in_seq = (jj[None] + block * ATT_QBLOCK >= BAND_PAD) | (block == ATT_LEAD)
    keep = in_band[None] & in_seq
    return jnp.where(jnp.asarray(keep)[:, None], (bias * LOG2_E)[None], NEG_BIG)


def _head_rms(x, gain):
    head_of_lane = lax.broadcasted_iota(jnp.int32, x.shape, 1) // ATT_DH
    sq = x * x
    ms = jnp.zeros_like(x)
    for h in range(ATT_GROUP):
        in_head = head_of_lane == h
        ms = jnp.where(in_head, jnp.sum(jnp.where(in_head, sq, 0.0), axis=-1, keepdims=True), ms)
    return x * lax.rsqrt(ms * (1.0 / ATT_DH) + EPS) * gain


def _head_rms_mxu(x, gain, same_head):
    sq = x * x
    hi = _bf16(sq)
    lo = _bf16(sq - hi.astype(jnp.float32))
    groups = []
    for g in range(x.shape[1] // SEG_LANES):
        ls = slice(g * SEG_LANES, (g + 1) * SEG_LANES)
        groups.append(_dot(hi[:, ls], same_head) + _dot(lo[:, ls], same_head))
    ss = jnp.concatenate(groups, axis=-1)
    return x * lax.rsqrt(ss * (1.0 / ATT_DH) + EPS) * gain


def _band_attn_kernel(q_ref, k_ref, v_ref, bias_ref, qg_ref, kg_ref, seg_ref, o_ref,
                      kpad_ref, vpad_ref, s_ref, p_ref, den_ref):
    n = pl.program_id(2)
    pair = 2 * ATT_DH

    @pl.when(n == 0)
    def _():
        low_half = lax.broadcasted_iota(jnp.int32, (KV_PREP_ROWS, pair), 1) < ATT_DH
        zero = jnp.zeros((KV_PREP_ROWS, pair), jnp.bfloat16)
        for h in range(ATT_GROUP):
            kpad_ref[h, 0:BAND_PAD, :] = jnp.zeros((BAND_PAD, pair), jnp.bfloat16)
            vpad_ref[h, 0:BAND_PAD, :] = jnp.zeros((BAND_PAD, pair), jnp.bfloat16)
        for r in range(SEQ // KV_PREP_ROWS):
            rows = slice(r * KV_PREP_ROWS, (r + 1) * KV_PREP_ROWS)
            dst = slice(BAND_PAD + r * KV_PREP_ROWS, BAND_PAD + (r + 1) * KV_PREP_ROWS)
            kn = _bf16(_head_rms_mxu(k_ref[rows, :].astype(jnp.float32), kg_ref[...], seg_ref[...]))
            v = v_ref[rows, :]
            for h in range(ATT_GROUP):
                cols = slice((h // 2) * pair, (h // 2 + 1) * pair)
                own = low_half if h % 2 == 0 else jnp.logical_not(low_half)
                kpad_ref[h, dst, :] = jnp.where(own, kn[:, cols], zero)
                vpad_ref[h, dst, :] = jnp.where(own, v[:, cols], zero)

    q = _bf16(_head_rms(q_ref[...].astype(jnp.float32), qg_ref[...]) * (LOG2_E * ATT_DH ** -0.5))
    start = pl.multiple_of(n * ATT_QBLOCK, ATT_QBLOCK)
    for h in range(ATT_GROUP):
        q_pair = q[:, (h // 2) * pair:(h // 2 + 1) * pair]
        s_ref[h] = _dot_nt(q_pair, kpad_ref[h, pl.ds(start, ATT_BAND), :])

    for h in range(ATT_GROUP):
        s = s_ref[h] + bias_ref[h]
        m = jnp.max(s, axis=-1, keepdims=True)
        p = jnp.exp2(s - m)
        den_ref[h] = jnp.sum(p, axis=-1, keepdims=True)
        p_ref[h] = _bf16(p)
    outs = []
    for h0 in range(0, ATT_GROUP, 2):
        pv = [_dot(p_ref[h], vpad_ref[h, pl.ds(start, ATT_BAND), :]) * (1.0 / den_ref[h])
              for h in (h0, h0 + 1)]
        outs.append(pv[0] + pv[1])
    o_ref[...] = _bf16(jnp.concatenate(outs, axis=-1))


def _band_attention(qkv, q_g, k_g, rel_bias):
    bias = _band_bias(rel_bias)
    nq = SEQ // ATT_QBLOCK
    width = ATT_GROUP * ATT_DH
    n_groups = ATT_HEADS // ATT_GROUP
    qg = jnp.tile(q_g, ATT_GROUP).reshape(1, width)
    kg = jnp.tile(k_g, ATT_GROUP).reshape(1, width)
    lane_head = np.arange(SEG_LANES) // ATT_DH
    same_head = jnp.asarray(lane_head[:, None] == lane_head[None, :], jnp.bfloat16)
    return pl.pallas_call(
        _band_attn_kernel,
        grid=(BATCH, n_groups, nq),
        in_specs=[
            pl.BlockSpec((ATT_QBLOCK, width), lambda b, hg, n: (b * nq + n, hg)),
            pl.BlockSpec((SEQ, width), lambda b, hg, n: (b, n_groups + hg)),
            pl.BlockSpec((SEQ, width), lambda b, hg, n: (b, 2 * n_groups + hg)),
            pl.BlockSpec((None, ATT_GROUP, ATT_QBLOCK, ATT_BAND),
                         lambda b, hg, n: (jnp.minimum(n, ATT_LEAD), hg, 0, 0)),
            pl.BlockSpec((1, width), lambda b, hg, n: (0, 0)),
            pl.BlockSpec((1, width), lambda b, hg, n: (0, 0)),
            pl.BlockSpec((SEG_LANES, SEG_LANES), lambda b, hg, n: (0, 0)),
        ],
        out_specs=pl.BlockSpec((ATT_QBLOCK, width), lambda b, hg, n: (b * nq + n, hg)),
        out_shape=jax.ShapeDtypeStruct((TOKENS, D_MODEL), jnp.bfloat16),
        scratch_shapes=[pltpu.VMEM((ATT_GROUP, SEQ + BAND_PAD, 2 * ATT_DH), jnp.bfloat16),
                        pltpu.VMEM((ATT_GROUP, SEQ + BAND_PAD, 2 * ATT_DH), jnp.bfloat16),
                        pltpu.VMEM((ATT_GROUP, ATT_QBLOCK, ATT_BAND), jnp.float32),
                        pltpu.VMEM((ATT_GROUP, ATT_QBLOCK, ATT_BAND), jnp.bfloat16),
                        pltpu.VMEM((ATT_GROUP, ATT_QBLOCK, 1), jnp.float32)],
        compiler_params=pltpu.CompilerParams(
            dimension_semantics=("arbitrary", "arbitrary", "arbitrary")),
        name="band_attention",
    )(qkv, qkv, qkv, bias, qg, kg, same_head)


def _store_token_major(ref, value, lead=()):
    rows = value.shape[0]
    for c in range(ROW_TILES):
        ref[(*lead, pl.ds(c, rows, stride=ROW_TILES), slice(None))] = value[:, c * 128:(c + 1) * 128]


def _load_token_major(ref, rows, c, lead=()):
    return ref[(*lead, pl.ds(c, rows, stride=ROW_TILES), slice(None))]


def _split_bf16(x):
    hi = _bf16(x)
    lo = _bf16(x - hi.astype(jnp.float32))
    return hi, lo


def _router_kernel(x_ref, g_ref, sc_ref, sh_ref, rw_ref, rb_ref,
                   h_ref, idx_ref, rank_ref, gate_ref, cnt_ref, carry_ref):
    @pl.when(pl.program_id(0) == 0)
    def _():
        carry_ref[...] = jnp.zeros_like(carry_ref)

    h = _norm_mod(x_ref[...], g_ref[...], sc_ref[0], sh_ref[0])
    _store_token_major(h_ref, h)
    h_hi, h_lo = _split_bf16(h)
    w_hi, w_lo = _split_bf16(rw_ref[...])
    logits = _dot_nt(w_hi, h_hi) + _dot_nt(w_hi, h_lo) + _dot_nt(w_lo, h_hi) + rb_ref[...]

    expert = lax.broadcasted_iota(jnp.int32, logits.shape, 0).astype(jnp.float32)
    work = logits
    vals, idxs, sels = [], [], []
    for _ in range(TOP_K):
        m = jnp.max(work, axis=0, keepdims=True)
        pick = jnp.min(jnp.where(work == m, expert, float(N_EXPERTS)), axis=0, keepdims=True)
        sel = expert == pick
        work = jnp.where(sel, -jnp.inf, work)
        vals.append(m)
        idxs.append(pick)
        sels.append(sel)
    exps = [jnp.exp(v - vals[0]) for v in vals]
    denom = exps[0] + exps[1] + exps[2] + exps[3]
    gate_ref[...] = jnp.concatenate([e / denom for e in exps], axis=0)
    idx_ref[...] = jnp.concatenate(idxs, axis=0).astype(jnp.int32)

    chosen = jnp.zeros(logits.shape, jnp.float32)
    for sel in sels:
        chosen = jnp.where(sel, 1.0, chosen)
    tile = logits.shape[1]
    earlier = (lax.broadcasted_iota(jnp.int32, (tile, tile), 0)
               < lax.broadcasted_iota(jnp.int32, (tile, tile), 1))
    before = _dot(_bf16(chosen), jnp.where(earlier, 1.0, 0.0).astype(jnp.bfloat16))
    rank_full = before + carry_ref[...]
    ranks = [jnp.sum(jnp.where(sel, rank_full, 0.0), axis=0, keepdims=True) for sel in sels]
    rank_ref[...] = jnp.concatenate(ranks, axis=0).astype(jnp.int32)
    carry = carry_ref[...] + jnp.sum(chosen, axis=1, keepdims=True)
    carry_ref[...] = carry
    cnt_ref[...] = jnp.broadcast_to(carry, cnt_ref.shape)


def _router(x, g, sc, sh, router_w, router_b):
    tiles_per_batch = SEQ // ROUTE_TILE
    lane_out = lambda dt: jax.ShapeDtypeStruct((TOP_K, TOKENS), dt)
    lane_spec = pl.BlockSpec((TOP_K, ROUTE_TILE), lambda i: (0, i))
    return pl.pallas_call(
        _router_kernel,
        grid=(TOKENS // ROUTE_TILE,),
        in_specs=[
            pl.BlockSpec((ROUTE_TILE, D_MODEL), lambda i: (i, 0)),
            pl.BlockSpec((1, D_MODEL), lambda i: (0, 0)),
            pl.BlockSpec((1, 1, D_MODEL), lambda i: (i // tiles_per_batch, 0, 0)),
            pl.BlockSpec((1, 1, D_MODEL), lambda i: (i // tiles_per_batch, 0, 0)),
            pl.BlockSpec((N_EXPERTS, D_MODEL), lambda i: (0, 0)),
            pl.BlockSpec((N_EXPERTS, 1), lambda i: (0, 0)),
        ],
        out_specs=[
            pl.BlockSpec((ROUTE_TILE * ROW_TILES, 128), lambda i: (i, 0)),
            lane_spec, lane_spec, lane_spec,
            pl.BlockSpec((N_EXPERTS, 128), lambda i: (0, 0)),
        ],
        out_shape=[
            jax.ShapeDtypeStruct((TOKENS * ROW_TILES, 128), jnp.float32),
            lane_out(jnp.int32), lane_out(jnp.int32), lane_out(jnp.float32),
            jax.ShapeDtypeStruct((N_EXPERTS, 128), jnp.float32),
        ],
        scratch_shapes=[pltpu.VMEM((N_EXPERTS, 1), jnp.float32)],
        compiler_params=pltpu.CompilerParams(dimension_semantics=("arbitrary",)),
        name="moe_router",
    )(x, g.reshape(1, D_MODEL), sc, sh, router_w.T, router_b.reshape(N_EXPERTS, 1))


def _row_copy(src_ref, src_row, dst_ref, dst_row, sem):
    src = pl.multiple_of(src_row * ROW_TILES, ROW_TILES)
    dst = pl.multiple_of(dst_row * ROW_TILES, ROW_TILES)
    return pltpu.make_async_copy(src_ref.at[pl.ds(src, ROW_TILES), :],
                                 dst_ref.at[pl.ds(dst, ROW_TILES), :], sem)


def _zero_block_copy(zero_ref, xs_ref, row, sem):
    row = pl.multiple_of(row * ROW_TILES, MOE_BLOCK * ROW_TILES)
    return pltpu.make_async_copy(zero_ref, xs_ref.at[pl.ds(row, MOE_BLOCK * ROW_TILES), :], sem)


def _dispatch_kernel(zstart_ref, nused_ref, pos_ref, h_ref, xs_ref, zero_ref, zsem, sem):
    step = pl.program_id(0)

    @pl.when(step == 0)
    def _():
        zero_ref[...] = jnp.zeros_like(zero_ref)

        def each_block(fn):
            def expert_last(e, carry):
                @pl.when(zstart_ref[e] >= 0)
                def _():
                    fn(_zero_block_copy(zero_ref, xs_ref, zstart_ref[e], zsem))
                return carry

            def unused(blk, carry):
                fn(_zero_block_copy(zero_ref, xs_ref, blk * MOE_BLOCK, zsem))
                return carry

            lax.fori_loop(0, N_EXPERTS, expert_last, 0)
            lax.fori_loop(nused_ref[0], MOE_NBLOCKS, unused, 0)

        each_block(lambda copy: copy.start())
        each_block(lambda copy: copy.wait())

    def issue(i, carry):
        for u in range(ISSUE_UNROLL):
            t = i * ISSUE_UNROLL + u
            for k in range(TOP_K):
                _row_copy(h_ref, t, xs_ref, pos_ref[t * TOP_K + k], sem).start(priority=k % 2)
        return carry

    lax.fori_loop(0, DISPATCH_TILE // ISSUE_UNROLL, issue, 0)

    for _ in range(TOP_K):
        pltpu.make_async_copy(h_ref, xs_ref.at[pl.ds(0, DISPATCH_TILE * ROW_TILES), :], sem).wait()


def _dispatch(h, pos_flat, zstart, n_used):
    grid_spec = pltpu.PrefetchScalarGridSpec(
        num_scalar_prefetch=2,
        grid=(TOKENS // DISPATCH_TILE,),
        in_specs=[
            pl.BlockSpec((DISPATCH_TILE * TOP_K,), lambda i, zs, nu: (i,),
                         memory_space=pltpu.SMEM),
            pl.BlockSpec((DISPATCH_TILE * ROW_TILES, 128), lambda i, zs, nu: (i, 0)),
        ],
        out_specs=pl.BlockSpec(memory_space=pl.ANY),
        scratch_shapes=[pltpu.VMEM((MOE_BLOCK * ROW_TILES, 128), jnp.float32),
                        pltpu.SemaphoreType.DMA, pltpu.SemaphoreType.DMA],
    )
    return pl.pallas_call(
        _dispatch_kernel,
        grid_spec=grid_spec,
        out_shape=jax.ShapeDtypeStruct((MOE_CAP * ROW_TILES, 128), jnp.float32),
        compiler_params=pltpu.CompilerParams(dimension_semantics=("arbitrary",)),
        name="moe_dispatch",
    )(zstart, n_used, pos_flat, h)


def _expert_kernel(be_ref, nused_ref, next_ref, xs_ref, b1_ref, b2_ref, w1_hbm, w2_hbm, ys_ref,
                   w1f_ref, w2f_ref, w1b_ref, w2b_ref, sems, *, layer):
    i = pl.program_id(0)
    expert = be_ref[i]
    used = i < nused_ref[0]
    new_expert = jnp.logical_or(i == 0, expert != be_ref[jnp.maximum(i - 1, 0)])

    def weight_copies(e):
        return (pltpu.make_async_copy(w1_hbm.at[layer, e], w1f_ref, sems.at[0]),
                pltpu.make_async_copy(w2_hbm.at[layer, e], w2f_ref, sems.at[1]))

    @pl.when(jnp.logical_and(used, new_expert))
    def _():
        @pl.when(i == 0)
        def _():
            for copy in weight_copies(expert):
                copy.start()

        for copy in weight_copies(expert):
            copy.wait()
        for r in range(D_MODEL // 256):
            rows = slice(r * 256, (r + 1) * 256)
            w1b_ref[rows, :] = _bf16(w1f_ref[rows, :])
            w2b_ref[rows, :] = _bf16(w2f_ref[rows, :])

        @pl.when(next_ref[i] >= 0)
        def _():
            for copy in weight_copies(next_ref[i]):
                copy.start()

    @pl.when(used)
    def _():
        x = _bf16(jnp.concatenate([_load_token_major(xs_ref, MOE_BLOCK, c)
                                   for c in range(ROW_TILES)], axis=-1))
        glu = _dot(x, w1b_ref[:, :D_FF]) + b1_ref[0][:, :D_FF]
        lin = _dot(x, w1b_ref[:, D_FF:]) + b1_ref[0][:, D_FF:]
        glu = jnp.minimum(glu, SWIGLU_LIMIT)
        lin = jnp.clip(lin, -SWIGLU_LIMIT, SWIGLU_LIMIT)
        act = glu * (1.0 / (1.0 + jnp.exp(-SWIGLU_ALPHA * glu))) * (lin + 1.0)
        _store_token_major(ys_ref, _dot(_bf16(act), w2b_ref[...]) + b2_ref[0])

    @pl.when(jnp.logical_not(used))
    def _():
        ys_ref[...] = jnp.zeros_like(ys_ref)


def _experts(xs, block_e, n_used, next_e, layer, w1, b1, w2, b2):
    grid_spec = pltpu.PrefetchScalarGridSpec(
        num_scalar_prefetch=3,
        grid=(MOE_NBLOCKS,),
        in_specs=[
            pl.BlockSpec((MOE_BLOCK * ROW_TILES, 128),
                         lambda i, be, nu, nx: (jnp.minimum(i, nu[0] - 1), 0)),
            pl.BlockSpec((None, 1, 1, 2 * D_FF), lambda i, be, nu, nx: (layer, be[i], 0, 0)),
            pl.BlockSpec((None, 1, 1, D_MODEL), lambda i, be, nu, nx: (layer, be[i], 0, 0)),
            pl.BlockSpec(memory_space=pl.ANY),
            pl.BlockSpec(memory_space=pl.ANY),
        ],
        out_specs=pl.BlockSpec((MOE_BLOCK * ROW_TILES, 128), lambda i, be, nu, nx: (i, 0)),
        scratch_shapes=[pltpu.VMEM((D_MODEL, 2 * D_FF), jnp.float32),
                        pltpu.VMEM((D_FF, D_MODEL), jnp.float32),
                        pltpu.VMEM((D_MODEL, 2 * D_FF), jnp.bfloat16),
                        pltpu.VMEM((D_FF, D_MODEL), jnp.bfloat16),
                        pltpu.SemaphoreType.DMA((2,))],
    )
    return pl.pallas_call(
        functools.partial(_expert_kernel, layer=layer),
        grid_spec=grid_spec,
        out_shape=jax.ShapeDtypeStruct((MOE_CAP * ROW_TILES, 128), jnp.float32),
        compiler_params=pltpu.CompilerParams(dimension_semantics=("arbitrary",),
                                             vmem_limit_bytes=VMEM_LIMIT_V7X),
        name="moe_experts",
    )(block_e, n_used, next_e, xs, b1.reshape(DEPTH, N_EXPERTS, 1, 2 * D_FF),
      b2.reshape(DEPTH, N_EXPERTS, 1, D_MODEL), w1, w2)


def _combine_kernel(pos_ref, pos_next_ref, x_ref, gates_ref, g2_ref, ys_ref, o_ref, buf_ref, sems):
    step = pl.program_id(0)
    n_steps = pl.num_programs(0)
    slot = step % 2

    def gather(p_ref, dst_slot):
        def issue(i, carry):
            for u in range(ISSUE_UNROLL):
                t = i * ISSUE_UNROLL + u
                for k in range(TOP_K):
                    _row_copy(ys_ref, p_ref[t * TOP_K + k], buf_ref.at[dst_slot, k], t,
                              sems.at[dst_slot]).start(priority=k % 2)
            return carry

        lax.fori_loop(0, COMBINE_TILE // ISSUE_UNROLL, issue, 0)

    @pl.when(step == 0)
    def _():
        gather(pos_ref, 0)

    @pl.when(step + 1 < n_steps)
    def _():
        gather(pos_next_ref, 1 - slot)

    for k in range(TOP_K):
        pltpu.make_async_copy(ys_ref.at[pl.ds(0, COMBINE_TILE * ROW_TILES), :],
                              buf_ref.at[slot, k], sems.at[slot]).wait()

    gates = gates_ref[...]
    gate_cols = [jnp.broadcast_to(gates[:, k:k + 1], (COMBINE_TILE, 128)) for k in range(TOP_K)]
    for c in range(ROW_TILES):
        cols = slice(c * 128, (c + 1) * 128)
        y = gate_cols[0] * _load_token_major(buf_ref, COMBINE_TILE, c, (slot, 0))
        for k in range(1, TOP_K):
            y = y + gate_cols[k] * _load_token_major(buf_ref, COMBINE_TILE, c, (slot, k))
        o_ref[:, cols] = x_ref[:, cols] + g2_ref[0][:, cols] * y


def _combine(x, pos_flat, gates_tok, g2, ys):
    tiles_per_batch = SEQ // COMBINE_TILE
    n_steps = TOKENS // COMBINE_TILE
    return pl.pallas_call(
        _combine_kernel,
        grid=(n_steps,),
        in_specs=[
            pl.BlockSpec((COMBINE_TILE * TOP_K,), lambda i: (i,), memory_space=pltpu.SMEM),
            pl.BlockSpec((COMBINE_TILE * TOP_K,), lambda i: (jnp.minimum(i + 1, n_steps - 1),),
                         memory_space=pltpu.SMEM),
            pl.BlockSpec((COMBINE_TILE, D_MODEL), lambda i: (i, 0)),
            pl.BlockSpec((COMBINE_TILE, TOP_K), lambda i: (i, 0)),
            pl.BlockSpec((1, 1, D_MODEL), lambda i: (i // tiles_per_batch, 0, 0)),
            pl.BlockSpec(memory_space=pl.ANY),
        ],
        out_specs=pl.BlockSpec((COMBINE_TILE, D_MODEL), lambda i: (i, 0)),
        out_shape=jax.ShapeDtypeStruct((TOKENS, D_MODEL), jnp.float32),
        scratch_shapes=[pltpu.VMEM((2, TOP_K, COMBINE_TILE * ROW_TILES, 128), jnp.float32),
                        pltpu.SemaphoreType.DMA((2,))],
        compiler_params=pltpu.CompilerParams(dimension_semantics=("arbitrary",),
                                             vmem_limit_bytes=VMEM_LIMIT_V7X),
        name="moe_combine",
    )(pos_flat, pos_flat, x, gates_tok, g2, ys)


def _moe_layer(x, g, sc, sh, g2, router_w, router_b, layer, w1, b1, w2, b2):
    h, idx, rank, gates, cnt = _router(x, g, sc, sh, router_w, router_b)
    counts = cnt[:, 0].astype(jnp.int32)
    padded = ((counts + MOE_BLOCK - 1) // MOE_BLOCK) * MOE_BLOCK
    pends = jnp.cumsum(padded)
    pstarts = pends - padded
    experts = jnp.arange(N_EXPERTS, dtype=jnp.int32)[:, None, None]
    pos = rank + jnp.sum(jnp.where(idx[None] == experts, pstarts[:, None, None], 0), axis=0)
    pos_flat = pos.T.reshape(-1)
    block_start = jnp.arange(MOE_NBLOCKS, dtype=jnp.int32) * MOE_BLOCK
    block_e = jnp.minimum(jnp.sum(block_start[:, None] >= pends[None, :], axis=1),
                          N_EXPERTS - 1).astype(jnp.int32)
    n_used = (pends[-1:] // MOE_BLOCK).astype(jnp.int32)
    zstart = jnp.where(padded > 0, pends - MOE_BLOCK, -1).astype(jnp.int32)
    ids = jnp.arange(N_EXPERTS, dtype=jnp.int32)
    later_used = jnp.logical_and(ids[None, :] > ids[:, None], (padded > 0)[None, :])
    next_used = jnp.min(jnp.where(later_used, ids[None, :], N_EXPERTS), axis=1)
    next_used = jnp.where(next_used < N_EXPERTS, next_used, -1)
    next_e = jnp.sum(jnp.where(block_e[:, None] == ids[None, :], next_used[None, :], 0),
                     axis=1).astype(jnp.int32)
    xs = _dispatch(h, pos_flat, zstart, n_used)
    ys = _experts(xs, block_e, n_used, next_e, layer, w1, b1, w2, b2)
    return _combine(x, pos_flat, gates.T, g2, ys)


def kernel(x, c, ada_w, ada_b, norm_mix_g, norm_ffn_g, ev_w_in, ev_w_out, sg_ln_g, sg_ln_b, sg_w, sg_b, od_w_in, od_w_out, od_q_g, od_k_g, od_rel_bias, moe_router_w, moe_router_b, moe_w1, moe_b1, moe_w2, moe_b2):
    mod = _ada_mod(c, ada_w, ada_b)
    xt = x.reshape(TOKENS, D_MODEL)
    for l in range(DEPTH):
        sh1, sc1, g1, sh2, sc2, g2 = [m.reshape(BATCH, 1, D_MODEL)
                                      for m in jnp.split(mod[l], 6, axis=-1)]
        i = l // 2
        if l % 2 == 0:
            proj = _norm_proj(xt, norm_mix_g[l], sc1, sh1, _bf16(ev_w_in[i]))
            a_out = _retention(proj)
            b_out = _spatial_gate(proj, sg_ln_g[i], sg_ln_b[i], sg_w[i], sg_b[i])
            xt = _out_proj([a_out, b_out], _bf16(ev_w_out[i]), xt, g1)
        else:
            qkv = _norm_proj(xt, norm_mix_g[l], sc1, sh1, _bf16(od_w_in[i]))
            att = _band_attention(qkv, od_q_g[i], od_k_g[i], od_rel_bias[i])
            xt = _out_proj([att], _bf16(od_w_out[i]), xt, g1)
        xt = _moe_layer(xt, norm_ffn_g[l], sc2, sh2, g2, moe_router_w[l], moe_router_b[l],
                        l, moe_w1, moe_b1, moe_w2, moe_b2)
    return xt.reshape(BATCH, SEQ, D_MODEL)
```

```python
import functools
import math

import numpy as np
import jax
import jax.numpy as jnp
from jax import lax
from jax.experimental import pallas as pl
from jax.experimental.pallas import tpu as pltpu

D_MODEL = 1024
BATCH = 8
SEQ = 2048
DEPTH = 4
TOKENS = BATCH * SEQ
CHUNK = 64
EPS = 1e-6

RET_HEADS = 4
RET_DK = 128
RET_DV = 256
ROPE_BASE = 10000.0
A_QK = RET_HEADS * RET_DK
A_V = RET_HEADS * RET_DV
SG_GROUPS = 4
SG_CH = 128
SG_WIDTH = SG_GROUPS * SG_CH
SG_LEN = 128
EVEN_IN = 2 * A_QK + 2 * A_V + 2 * SG_WIDTH
EVEN_MIX = A_V + SG_WIDTH

ATT_HEADS = 16
ATT_DH = 64
BAND_PAD = 8 * CHUNK
MAX_REL = 256
REL_SIZE = (CHUNK - 1) + MAX_REL + 1

N_EXPERTS = 32
TOP_K = 4
D_FF = D_MODEL
SWIGLU_ALPHA = 1.702
SWIGLU_LIMIT = 7.0

ROW_TILE = 1024
COL_CHUNK = 512
RET_BLOCK = 256
ATT_QBLOCK = 128
ATT_BAND = ATT_QBLOCK + BAND_PAD
ATT_GROUP = 8
ATT_LEAD = BAND_PAD // ATT_QBLOCK
KV_PREP_ROWS = 256
SEG_LANES = 256
SG_TILE = 512
ROUTE_TILE = 512
MOE_BLOCK = 256
X_SLOTS = 3
MOE_CAP = TOKENS * TOP_K + N_EXPERTS * MOE_BLOCK
MOE_NBLOCKS = MOE_CAP // MOE_BLOCK
DISPATCH_TILE = 512
COMBINE_TILE = 256
ISSUE_UNROLL = 8
VMEM_LIMIT_V7X = 56 * 1024 * 1024
ROW_TILES = D_MODEL // 128

NEG_BIG = -1e30
LOG2_E = math.log2(math.e)


def _silu(x):
    return x * (1.0 / (1.0 + jnp.exp(-x)))


def _gelu_tanh(x):
    return 0.5 * x * (1.0 + jnp.tanh(math.sqrt(2.0 / math.pi) * (x + 0.044715 * (x * x * x))))


def _bf16(x):
    return x.astype(jnp.bfloat16)


def _dot(a, b):
    return jnp.dot(a, b, preferred_element_type=jnp.float32)


def _dot_nt(a, b):
    return lax.dot_general(a, b, (((1,), (1,)), ((), ())), preferred_element_type=jnp.float32)


def _dot_tn(a, b):
    return lax.dot_general(a, b, (((0,), (0,)), ((), ())), preferred_element_type=jnp.float32)


def _norm_mod(x, g, sc, sh):
    y = x * lax.rsqrt(jnp.mean(x * x, axis=-1, keepdims=True) + EPS)
    return (y * g) * (1.0 + sc) + sh


def _ada_kernel(c_ref, w_ref, b_ref, o_ref):
    c_act = _silu(c_ref[...])
    o_ref[0] = _dot(c_act, w_ref[0]) + b_ref[0]


def _ada_mod(c, ada_w, ada_b):
    n_col = 6
    return pl.pallas_call(
        _ada_kernel,
        grid=(DEPTH, n_col),
        in_specs=[
            pl.BlockSpec((BATCH, D_MODEL), lambda l, j: (0, 0)),
            pl.BlockSpec((1, D_MODEL, D_MODEL), lambda l, j: (l, 0, j)),
            pl.BlockSpec((1, 1, D_MODEL), lambda l, j: (l, 0, j)),
        ],
        out_specs=pl.BlockSpec((1, BATCH, D_MODEL), lambda l, j: (l, 0, j)),
        out_shape=jax.ShapeDtypeStruct((DEPTH, BATCH, 6 * D_MODEL), jnp.float32),
        name="ada_mod",
    )(c, ada_w, ada_b.reshape(DEPTH, 1, 6 * D_MODEL))


def _norm_proj_kernel(x_ref, g_ref, sc_ref, sh_ref, w_ref, o_ref):
    h = _bf16(_norm_mod(x_ref[...], g_ref[...], sc_ref[0], sh_ref[0]))
    n_out = o_ref.shape[1]
    for j in range(n_out // COL_CHUNK):
        cols = slice(j * COL_CHUNK, (j + 1) * COL_CHUNK)
        o_ref[:, cols] = _bf16(_dot(h, w_ref[:, cols]))


def _norm_proj(x, g, sc, sh, w_bf16):
    n_out = w_bf16.shape[1]
    tiles_per_batch = SEQ // ROW_TILE
    return pl.pallas_call(
        _norm_proj_kernel,
        grid=(TOKENS // ROW_TILE,),
        in_specs=[
            pl.BlockSpec((ROW_TILE, D_MODEL), lambda i: (i, 0)),
            pl.BlockSpec((1, D_MODEL), lambda i: (0, 0)),
            pl.BlockSpec((1, 1, D_MODEL), lambda i: (i // tiles_per_batch, 0, 0)),
            pl.BlockSpec((1, 1, D_MODEL), lambda i: (i // tiles_per_batch, 0, 0)),
            pl.BlockSpec((D_MODEL, n_out), lambda i: (0, 0)),
        ],
        out_specs=pl.BlockSpec((ROW_TILE, n_out), lambda i: (i, 0)),
        out_shape=jax.ShapeDtypeStruct((TOKENS, n_out), jnp.bfloat16),
        compiler_params=pltpu.CompilerParams(vmem_limit_bytes=VMEM_LIMIT_V7X),
        name="norm_proj",
    )(x, g.reshape(1, D_MODEL), sc, sh, w_bf16)


def _out_proj_kernel(*refs, widths):
    part_refs = refs[:len(widths)]
    w_ref, x_ref, g_ref, o_ref = refs[len(widths):]
    for j in range(D_MODEL // COL_CHUNK):
        cols = slice(j * COL_CHUNK, (j + 1) * COL_CHUNK)
        acc = None
        row0 = 0
        for p_ref, width in zip(part_refs, widths):
            term = _dot(p_ref[...], w_ref[row0:row0 + width, cols])
            acc = term if acc is None else acc + term
            row0 += width
        o_ref[:, cols] = x_ref[:, cols] + g_ref[0][:, cols] * acc


def _out_proj(parts, w_bf16, x, gate):
    widths = tuple(p.shape[1] for p in parts)
    tiles_per_batch = SEQ // ROW_TILE
    in_specs = [pl.BlockSpec((ROW_TILE, width), lambda i: (i, 0)) for width in widths]
    in_specs += [
        pl.BlockSpec((sum(widths), D_MODEL), lambda i: (0, 0)),
        pl.BlockSpec((ROW_TILE, D_MODEL), lambda i: (i, 0)),
        pl.BlockSpec((1, 1, D_MODEL), lambda i: (i // tiles_per_batch, 0, 0)),
    ]
    return pl.pallas_call(
        functools.partial(_out_proj_kernel, widths=widths),
        grid=(TOKENS // ROW_TILE,),
        in_specs=in_specs,
        out_specs=pl.BlockSpec((ROW_TILE, D_MODEL), lambda i: (i, 0)),
        out_shape=jax.ShapeDtypeStruct((TOKENS, D_MODEL), jnp.float32),
        compiler_params=pltpu.CompilerParams(vmem_limit_bytes=VMEM_LIMIT_V7X),
        name="out_proj",
    )(*parts, w_bf16, x, gate)


def _retention_tables():
    heads = np.arange(RET_HEADS, dtype=np.float64)
    log_g = np.log1p(-np.exp2(-5.0 - heads))
    idx = np.arange(RET_BLOCK, dtype=np.float64)
    diff = idx[:, None] - idx[None, :]
    ci, cj = (idx // CHUNK)[:, None], (idx // CHUNK)[None, :]
    expo = np.where(ci == cj, np.abs(diff), diff)
    decay = np.where(cj <= ci, np.exp(log_g[:, None, None] * expo[None]), 0.0)
    q_dec = np.exp(log_g[:, None] * (idx[None, :] + 1.0))
    k_dec = np.exp(log_g[:, None] * (RET_BLOCK - 1.0 - idx[None, :]))
    blk_dec = np.exp(log_g * RET_BLOCK)
    q_dec = np.broadcast_to(q_dec[:, :, None], (RET_HEADS, RET_BLOCK, RET_DK))
    k_dec = np.broadcast_to(k_dec[:, :, None], (RET_HEADS, RET_BLOCK, RET_DK))
    blk_dec = np.broadcast_to(blk_dec[:, None, None], (RET_HEADS, 1, RET_DV))
    half = RET_DK // 2
    inv = ROPE_BASE ** (-np.arange(half, dtype=np.float64) / half)
    ang = np.arange(SEQ, dtype=np.float64)[:, None] * inv[None, :]
    cos = np.concatenate([np.cos(ang), np.cos(ang)], axis=1)
    sin = np.concatenate([-np.sin(ang), np.sin(ang)], axis=1)
    f32 = lambda a: jnp.asarray(np.ascontiguousarray(a), jnp.float32)
    return f32(decay), f32(q_dec), f32(k_dec), f32(blk_dec), f32(cos), f32(sin)


def _retention_kernel(q_ref, k_ref, v_ref, gate_ref, cos_ref, sin_ref, dec_ref, qd_ref, kd_ref,
                      bd_ref, o_ref, state_ref):
    @pl.when(pl.program_id(1) == 0)
    def _():
        state_ref[...] = jnp.zeros_like(state_ref)

    cos, sin = cos_ref[...], sin_ref[...]
    for h in range(RET_HEADS):
        ks = slice(h * RET_DK, (h + 1) * RET_DK)
        vs = slice(h * RET_DV, (h + 1) * RET_DV)
        q = q_ref[:, ks].astype(jnp.float32)
        k = k_ref[:, ks].astype(jnp.float32)
        qr = q * cos + pltpu.roll(q, RET_DK // 2, axis=1) * sin
        kr = (k * cos + pltpu.roll(k, RET_DK // 2, axis=1) * sin) * (RET_DK ** -0.5)
        v = v_ref[:, vs]

        scores = _dot_nt(_bf16(qr), _bf16(kr)) * dec_ref[h]
        intra = _dot(_bf16(scores), v)
        state = state_ref[h]
        inter = _dot(_bf16(qr * qd_ref[h]), _bf16(state))
        state_ref[h] = state * bd_ref[h] + _dot_tn(_bf16(kr * kd_ref[h]), v)

        o = intra + inter
        mu = jnp.mean(o, axis=-1, keepdims=True)
        cen = o - mu
        var = jnp.mean(cen * cen, axis=-1, keepdims=True)
        normed = cen * lax.rsqrt(var + EPS)
        o_ref[:, vs] = _bf16(_silu(gate_ref[:, vs].astype(jnp.float32)) * normed)


def _retention(proj):
    decay, q_dec, k_dec, blk_dec, cos, sin = _retention_tables()
    nblk = SEQ // RET_BLOCK
    row = lambda b, n: b * nblk + n
    whole = lambda b, n: (0, 0, 0)
    return pl.pallas_call(
        _retention_kernel,
        grid=(BATCH, nblk),
        in_specs=[
            pl.BlockSpec((RET_BLOCK, A_QK), lambda b, n: (row(b, n), 0)),
            pl.BlockSpec((RET_BLOCK, A_QK), lambda b, n: (row(b, n), 1)),
            pl.BlockSpec((RET_BLOCK, A_V), lambda b, n: (row(b, n), 2 * A_QK // A_V)),
            pl.BlockSpec((RET_BLOCK, A_V), lambda b, n: (row(b, n), 2 * A_QK // A_V + 1)),
            pl.BlockSpec((RET_BLOCK, RET_DK), lambda b, n: (n, 0)),
            pl.BlockSpec((RET_BLOCK, RET_DK), lambda b, n: (n, 0)),
            pl.BlockSpec((RET_HEADS, RET_BLOCK, RET_BLOCK), whole),
            pl.BlockSpec((RET_HEADS, RET_BLOCK, RET_DK), whole),
            pl.BlockSpec((RET_HEADS, RET_BLOCK, RET_DK), whole),
            pl.BlockSpec((RET_HEADS, 1, RET_DV), whole),
        ],
        out_specs=pl.BlockSpec((RET_BLOCK, A_V), lambda b, n: (row(b, n), 0)),
        out_shape=jax.ShapeDtypeStruct((TOKENS, A_V), jnp.bfloat16),
        scratch_shapes=[pltpu.VMEM((RET_HEADS, RET_DK, RET_DV), jnp.float32)],
        compiler_params=pltpu.CompilerParams(dimension_semantics=("arbitrary", "arbitrary")),
        name="retention",
    )(proj, proj, proj, proj, cos, sin, decay, q_dec, k_dec, blk_dec)


def _spatial_gate_kernel(u_ref, z_ref, lng_ref, lnb_ref, w_ref, b_ref, o_ref):
    u = _gelu_tanh(u_ref[...].astype(jnp.float32))
    z = _gelu_tanh(z_ref[...].astype(jnp.float32))
    mu = jnp.mean(z, axis=-1, keepdims=True)
    cen = z - mu
    var = jnp.mean(cen * cen, axis=-1, keepdims=True)
    zn = _bf16(cen * lax.rsqrt(var + EPS) * lng_ref[...] + lnb_ref[...])
    rows = lax.broadcasted_iota(jnp.int32, (SG_LEN, SG_LEN), 0)
    cols = lax.broadcasted_iota(jnp.int32, (SG_LEN, SG_LEN), 1)
    keep = (rows // CHUNK) >= (cols // CHUNK)
    for g in range(SG_GROUPS):
        cs = slice(g * SG_CH, (g + 1) * SG_CH)
        w = _bf16(jnp.where(keep, w_ref[g], 0.0))
        for blk in range(SG_TILE // SG_LEN):
            rs = slice(blk * SG_LEN, (blk + 1) * SG_LEN)
            mixed = _dot(w, zn[rs, cs]) + b_ref[g]
            o_ref[rs, cs] = _bf16(u[rs, cs] * mixed)


def _spatial_gate(proj, ln_g, ln_b, w_s, b_s):
    u_col = (2 * A_QK + 2 * A_V) // SG_WIDTH
    b_full = jnp.broadcast_to(b_s[:, :, None], (SG_GROUPS, SG_LEN, SG_CH))
    return pl.pallas_call(
        _spatial_gate_kernel,
        grid=(TOKENS // SG_TILE,),
        in_specs=[
            pl.BlockSpec((SG_TILE, SG_WIDTH), lambda i: (i, u_col)),
            pl.BlockSpec((SG_TILE, SG_WIDTH), lambda i: (i, u_col + 1)),
            pl.BlockSpec((1, SG_WIDTH), lambda i: (0, 0)),
            pl.BlockSpec((1, SG_WIDTH), lambda i: (0, 0)),
            pl.BlockSpec((SG_GROUPS, SG_LEN, SG_LEN), lambda i: (0, 0, 0)),
            pl.BlockSpec((SG_GROUPS, SG_LEN, SG_CH), lambda i: (0, 0, 0)),
        ],
        out_specs=pl.BlockSpec((SG_TILE, SG_WIDTH), lambda i: (i, 0)),
        out_shape=jax.ShapeDtypeStruct((TOKENS, SG_WIDTH), jnp.bfloat16),
        name="spatial_gate",
    )(proj, proj, ln_g.reshape(1, SG_WIDTH), ln_b.reshape(1, SG_WIDTH), w_s, b_full)


def _band_bias(rel_bias):
    period = ATT_QBLOCK + ATT_BAND
    m = np.arange(period)
    shift = np.where(m < ATT_BAND, m, m - period)
    rel = BAND_PAD - shift
    table = rel_bias[:, np.clip(rel, -(CHUNK - 1), MAX_REL) + (CHUNK - 1)].astype(jnp.float32)
    flat = jnp.tile(table, (1, ATT_QBLOCK))[:, :ATT_QBLOCK * (period - 1)]
    bias = flat.reshape(ATT_HEADS, ATT_QBLOCK, period - 1)[:, :, :ATT_BAND]
    i = np.arange(ATT_QBLOCK)[:, None]
    jj = np.arange(ATT_BAND)[None, :]
    in_band = (jj // CHUNK >= i // CHUNK) & (jj // CHUNK <= i // CHUNK + BAND_PAD // CHUNK)
    block = np.arange(ATT_LEAD + 1)[:, None, None]
    in_seq = (jj[None] + block * ATT_QBLOCK >= BAND_PAD) | (block == ATT_LEAD)
    keep = in_band[None] & in_seq
    return jnp.where(jnp.asarray(keep)[:, None], (bias * LOG2_E)[None], NEG_BIG)


def _head_rms(x, gain):
    head_of_lane = lax.broadcasted_iota(jnp.int32, x.shape, 1) // ATT_DH
    sq = x * x
    ms = jnp.zeros_like(x)
    for h in range(ATT_GROUP):
        in_head = head_of_lane == h
        ms = jnp.where(in_head, jnp.sum(jnp.where(in_head, sq, 0.0), axis=-1, keepdims=True), ms)
    return x * lax.rsqrt(ms * (1.0 / ATT_DH) + EPS) * gain


def _head_rms_mxu(x, gain, same_head):
    sq = x * x
    hi = _bf16(sq)
    lo = _bf16(sq - hi.astype(jnp.float32))
    groups = []
    for g in range(x.shape[1] // SEG_LANES):
        ls = slice(g * SEG_LANES, (g + 1) * SEG_LANES)
        groups.append(_dot(hi[:, ls], same_head) + _dot(lo[:, ls], same_head))
    ss = jnp.concatenate(groups, axis=-1)
    return x * lax.rsqrt(ss * (1.0 / ATT_DH) + EPS) * gain


def _band_attn_kernel(q_ref, k_ref, v_ref, bias_ref, qg_ref, kg_ref, seg_ref, o_ref,
                      kpad_ref, vpad_ref, s_ref, p_ref, den_ref):
    n = pl.program_id(2)
    pair = 2 * ATT_DH

    @pl.when(n == 0)
    def _():
        low_half = lax.broadcasted_iota(jnp.int32, (KV_PREP_ROWS, pair), 1) < ATT_DH
        zero = jnp.zeros((KV_PREP_ROWS, pair), jnp.bfloat16)
        for h in range(ATT_GROUP):
            kpad_ref[h, 0:BAND_PAD, :] = jnp.zeros((BAND_PAD, pair), jnp.bfloat16)
            vpad_ref[h, 0:BAND_PAD, :] = jnp.zeros((BAND_PAD, pair), jnp.bfloat16)
        for r in range(SEQ // KV_PREP_ROWS):
            rows = slice(r * KV_PREP_ROWS, (r + 1) * KV_PREP_ROWS)
            dst = slice(BAND_PAD + r * KV_PREP_ROWS, BAND_PAD + (r + 1) * KV_PREP_ROWS)
            kn = _bf16(_head_rms_mxu(k_ref[rows, :].astype(jnp.float32), kg_ref[...], seg_ref[...]))
            v = v_ref[rows, :]
            for h in range(ATT_GROUP):
                cols = slice((h // 2) * pair, (h // 2 + 1) * pair)
                own = low_half if h % 2 == 0 else jnp.logical_not(low_half)
                kpad_ref[h, dst, :] = jnp.where(own, kn[:, cols], zero)
                vpad_ref[h, dst, :] = jnp.where(own, v[:, cols], zero)

    q = _bf16(_head_rms(q_ref[...].astype(jnp.float32), qg_ref[...]) * (LOG2_E * ATT_DH ** -0.5))
    start = pl.multiple_of(n * ATT_QBLOCK, ATT_QBLOCK)
    for h in range(ATT_GROUP):
        q_pair = q[:, (h // 2) * pair:(h // 2 + 1) * pair]
        s_ref[h] = _dot_nt(q_pair, kpad_ref[h, pl.ds(start, ATT_BAND), :])

    for h in range(ATT_GROUP):
        s = s_ref[h] + bias_ref[h]
        m = jnp.max(s, axis=-1, keepdims=True)
        p = jnp.exp2(s - m)
        den_ref[h] = jnp.sum(p, axis=-1, keepdims=True)
        p_ref[h] = _bf16(p)
    outs = []
    for h0 in range(0, ATT_GROUP, 2):
        pv = [_dot(p_ref[h], vpad_ref[h, pl.ds(start, ATT_BAND), :]) * (1.0 / den_ref[h])
              for h in (h0, h0 + 1)]
        outs.append(pv[0] + pv[1])
    o_ref[...] = _bf16(jnp.concatenate(outs, axis=-1))


def _band_attention(qkv, q_g, k_g, rel_bias):
    bias = _band_bias(rel_bias)
    nq = SEQ // ATT_QBLOCK
    width = ATT_GROUP * ATT_DH
    n_groups = ATT_HEADS // ATT_GROUP
    qg = jnp.tile(q_g, ATT_GROUP).reshape(1, width)
    kg = jnp.tile(k_g, ATT_GROUP).reshape(1, width)
    lane_head = np.arange(SEG_LANES) // ATT_DH
    same_head = jnp.asarray(lane_head[:, None] == lane_head[None, :], jnp.bfloat16)
    return pl.pallas_call(
        _band_attn_kernel,
        grid=(BATCH, n_groups, nq),
        in_specs=[
            pl.BlockSpec((ATT_QBLOCK, width), lambda b, hg, n: (b * nq + n, hg)),
            pl.BlockSpec((SEQ, width), lambda b, hg, n: (b, n_groups + hg)),
            pl.BlockSpec((SEQ, width), lambda b, hg, n: (b, 2 * n_groups + hg)),
            pl.BlockSpec((None, ATT_GROUP, ATT_QBLOCK, ATT_BAND),
                         lambda b, hg, n: (jnp.minimum(n, ATT_LEAD), hg, 0, 0)),
            pl.BlockSpec((1, width), lambda b, hg, n: (0, 0)),
            pl.BlockSpec((1, width), lambda b, hg, n: (0, 0)),
            pl.BlockSpec((SEG_LANES, SEG_LANES), lambda b, hg, n: (0, 0)),
        ],
        out_specs=pl.BlockSpec((ATT_QBLOCK, width), lambda b, hg, n: (b * nq + n, hg)),
        out_shape=jax.ShapeDtypeStruct((TOKENS, D_MODEL), jnp.bfloat16),
        scratch_shapes=[pltpu.VMEM((ATT_GROUP, SEQ + BAND_PAD, 2 * ATT_DH), jnp.bfloat16),
                        pltpu.VMEM((ATT_GROUP, SEQ + BAND_PAD, 2 * ATT_DH), jnp.bfloat16),
                        pltpu.VMEM((ATT_GROUP, ATT_QBLOCK, ATT_BAND), jnp.float32),
                        pltpu.VMEM((ATT_GROUP, ATT_QBLOCK, ATT_BAND), jnp.bfloat16),
                        pltpu.VMEM((ATT_GROUP, ATT_QBLOCK, 1), jnp.float32)],
        compiler_params=pltpu.CompilerParams(
            dimension_semantics=("arbitrary", "arbitrary", "arbitrary")),
        name="band_attention",
    )(qkv, qkv, qkv, bias, qg, kg, same_head)


def _store_token_major(ref, value, lead=()):
    rows = value.shape[0]
    for c in range(ROW_TILES):
        ref[(*lead, pl.ds(c, rows, stride=ROW_TILES), slice(None))] = value[:, c * 128:(c + 1) * 128]


def _load_token_major(ref, rows, c, lead=()):
    return ref[(*lead, pl.ds(c, rows, stride=ROW_TILES), slice(None))]


def _split_bf16(x):
    hi = _bf16(x)
    lo = _bf16(x - hi.astype(jnp.float32))
    return hi, lo


def _router_kernel(x_ref, g_ref, sc_ref, sh_ref, rw_ref, rb_ref,
                   h_ref, idx_ref, rank_ref, gate_ref, cnt_ref, carry_ref):
    @pl.when(pl.program_id(0) == 0)
    def _():
        carry_ref[...] = jnp.zeros_like(carry_ref)

    h = _norm_mod(x_ref[...], g_ref[...], sc_ref[0], sh_ref[0])
    _store_token_major(h_ref, h)
    h_hi, h_lo = _split_bf16(h)
    w_hi, w_lo = _split_bf16(rw_ref[...])
    logits = _dot_nt(w_hi, h_hi) + _dot_nt(w_hi, h_lo) + _dot_nt(w_lo, h_hi) + rb_ref[...]

    expert = lax.broadcasted_iota(jnp.int32, logits.shape, 0).astype(jnp.float32)
    work = logits
    vals, idxs, sels = [], [], []
    for _ in range(TOP_K):
        m = jnp.max(work, axis=0, keepdims=True)
        pick = jnp.min(jnp.where(work == m, expert, float(N_EXPERTS)), axis=0, keepdims=True)
        sel = expert == pick
        work = jnp.where(sel, -jnp.inf, work)
        vals.append(m)
        idxs.append(pick)
        sels.append(sel)
    exps = [jnp.exp(v - vals[0]) for v in vals]
    denom = exps[0] + exps[1] + exps[2] + exps[3]
    gate_ref[...] = jnp.concatenate([e / denom for e in exps], axis=0)
    idx_ref[...] = jnp.concatenate(idxs, axis=0).astype(jnp.int32)

    chosen = jnp.zeros(logits.shape, jnp.float32)
    for sel in sels:
        chosen = jnp.where(sel, 1.0, chosen)
    tile = logits.shape[1]
    earlier = (lax.broadcasted_iota(jnp.int32, (tile, tile), 0)
               < lax.broadcasted_iota(jnp.int32, (tile, tile), 1))
    before = _dot(_bf16(chosen), jnp.where(earlier, 1.0, 0.0).astype(jnp.bfloat16))
    rank_full = before + carry_ref[...]
    ranks = [jnp.sum(jnp.where(sel, rank_full, 0.0), axis=0, keepdims=True) for sel in sels]
    rank_ref[...] = jnp.concatenate(ranks, axis=0).astype(jnp.int32)
    carry = carry_ref[...] + jnp.sum(chosen, axis=1, keepdims=True)
    carry_ref[...] = carry
    cnt_ref[...] = jnp.broadcast_to(carry, cnt_ref.shape)


def _router(x, g, sc, sh, router_w, router_b):
    tiles_per_batch = SEQ // ROUTE_TILE
    lane_out = lambda dt: jax.ShapeDtypeStruct((TOP_K, TOKENS), dt)
    lane_spec = pl.BlockSpec((TOP_K, ROUTE_TILE), lambda i: (0, i))
    return pl.pallas_call(
        _router_kernel,
        grid=(TOKENS // ROUTE_TILE,),
        in_specs=[
            pl.BlockSpec((ROUTE_TILE, D_MODEL), lambda i: (i, 0)),
            pl.BlockSpec((1, D_MODEL), lambda i: (0, 0)),
            pl.BlockSpec((1, 1, D_MODEL), lambda i: (i // tiles_per_batch, 0, 0)),
            pl.BlockSpec((1, 1, D_MODEL), lambda i: (i // tiles_per_batch, 0, 0)),
            pl.BlockSpec((N_EXPERTS, D_MODEL), lambda i: (0, 0)),
            pl.BlockSpec((N_EXPERTS, 1), lambda i: (0, 0)),
        ],
        out_specs=[
            pl.BlockSpec((ROUTE_TILE * ROW_TILES, 128), lambda i: (i, 0)),
            lane_spec, lane_spec, lane_spec,
            pl.BlockSpec((N_EXPERTS, 128), lambda i: (0, 0)),
        ],
        out_shape=[
            jax.ShapeDtypeStruct((TOKENS * ROW_TILES, 128), jnp.float32),
            lane_out(jnp.int32), lane_out(jnp.int32), lane_out(jnp.float32),
            jax.ShapeDtypeStruct((N_EXPERTS, 128), jnp.float32),
        ],
        scratch_shapes=[pltpu.VMEM((N_EXPERTS, 1), jnp.float32)],
        compiler_params=pltpu.CompilerParams(dimension_semantics=("arbitrary",)),
        name="moe_router",
    )(x, g.reshape(1, D_MODEL), sc, sh, router_w.T, router_b.reshape(N_EXPERTS, 1))


def _row_copy(src_ref, src_row, dst_ref, dst_row, sem):
    src = pl.multiple_of(src_row * ROW_TILES, ROW_TILES)
    dst = pl.multiple_of(dst_row * ROW_TILES, ROW_TILES)
    return pltpu.make_async_copy(src_ref.at[pl.ds(src, ROW_TILES), :],
                                 dst_ref.at[pl.ds(dst, ROW_TILES), :], sem)


def _zero_block_copy(zero_ref, xs_ref, row, sem):
    row = pl.multiple_of(row * ROW_TILES, MOE_BLOCK * ROW_TILES)
    return pltpu.make_async_copy(zero_ref, xs_ref.at[pl.ds(row, MOE_BLOCK * ROW_TILES), :], sem)


def _dispatch_kernel(zstart_ref, nused_ref, pos_ref, h_ref, xs_ref, zero_ref, zsem, sem):
    step = pl.program_id(0)

    @pl.when(step == 0)
    def _():
        zero_ref[...] = jnp.zeros_like(zero_ref)

        def each_block(fn):
            def expert_last(e, carry):
                @pl.when(zstart_ref[e] >= 0)
                def _():
                    fn(_zero_block_copy(zero_ref, xs_ref, zstart_ref[e], zsem))
                return carry

            def unused(blk, carry):
                fn(_zero_block_copy(zero_ref, xs_ref, blk * MOE_BLOCK, zsem))
                return carry

            lax.fori_loop(0, N_EXPERTS, expert_last, 0)
            lax.fori_loop(nused_ref[0], MOE_NBLOCKS, unused, 0)

        each_block(lambda copy: copy.start())
        each_block(lambda copy: copy.wait())

    def issue(i, carry):
        for u in range(ISSUE_UNROLL):
            t = i * ISSUE_UNROLL + u
            for k in range(TOP_K):
                _row_copy(h_ref, t, xs_ref, pos_ref[t * TOP_K + k], sem).start(priority=k % 2)
        return carry

    lax.fori_loop(0, DISPATCH_TILE // ISSUE_UNROLL, issue, 0)

    for _ in range(TOP_K):
        pltpu.make_async_copy(h_ref, xs_ref.at[pl.ds(0, DISPATCH_TILE * ROW_TILES), :], sem).wait()


def _dispatch(h, pos_flat, zstart, n_used):
    grid_spec = pltpu.PrefetchScalarGridSpec(
        num_scalar_prefetch=2,
        grid=(TOKENS // DISPATCH_TILE,),
        in_specs=[
            pl.BlockSpec((DISPATCH_TILE * TOP_K,), lambda i, zs, nu: (i,),
                         memory_space=pltpu.SMEM),
            pl.BlockSpec((DISPATCH_TILE * ROW_TILES, 128), lambda i, zs, nu: (i, 0)),
        ],
        out_specs=pl.BlockSpec(memory_space=pl.ANY),
        scratch_shapes=[pltpu.VMEM((MOE_BLOCK * ROW_TILES, 128), jnp.float32),
                        pltpu.SemaphoreType.DMA, pltpu.SemaphoreType.DMA],
    )
    return pl.pallas_call(
        _dispatch_kernel,
        grid_spec=grid_spec,
        out_shape=jax.ShapeDtypeStruct((MOE_CAP * ROW_TILES, 128), jnp.float32),
        compiler_params=pltpu.CompilerParams(dimension_semantics=("arbitrary",)),
        name="moe_dispatch",
    )(zstart, n_used, pos_flat, h)


def _expert_kernel(be_ref, nused_ref, next_ref, xs_hbm, b1_ref, b2_ref, w1_hbm, w2_hbm, ys_ref,
                   w1f_ref, w2f_ref, w1b_ref, w2b_ref, sems, xbuf_ref, xsems, *, layer):
    i = pl.program_id(0)
    expert = be_ref[i]
    n_used = nused_ref[0]
    used = i < n_used
    block_rows = MOE_BLOCK * ROW_TILES

    def x_copy(blk):
        row = pl.multiple_of(blk * block_rows, block_rows)
        slot = blk % X_SLOTS
        return pltpu.make_async_copy(xs_hbm.at[pl.ds(row, block_rows), :], xbuf_ref.at[slot],
                                     xsems.at[slot])

    @pl.when(i == 0)
    def _():
        x_copy(0).start()

        @pl.when(n_used > 1)
        def _():
            x_copy(1).start()

    @pl.when(i + 2 < n_used)
    def _():
        x_copy(i + 2).start()
    new_expert = jnp.logical_or(i == 0, expert != be_ref[jnp.maximum(i - 1, 0)])

    def weight_copies(e):
        return (pltpu.make_async_copy(w1_hbm.at[layer, e], w1f_ref, sems.at[0]),
                pltpu.make_async_copy(w2_hbm.at[layer, e], w2f_ref, sems.at[1]))

    @pl.when(jnp.logical_and(used, new_expert))
    def _():
        @pl.when(i == 0)
        def _():
            for copy in weight_copies(expert):
                copy.start()

        for copy in weight_copies(expert):
            copy.wait()
        for r in range(D_MODEL // 256):
            rows = slice(r * 256, (r + 1) * 256)
            w1b_ref[rows, :] = _bf16(w1f_ref[rows, :])
            w2b_ref[rows, :] = _bf16(w2f_ref[rows, :])

        @pl.when(next_ref[i] >= 0)
        def _():
            for copy in weight_copies(next_ref[i]):
                copy.start()

    @pl.when(used)
    def _():
        x_copy(i).wait()
        x = _bf16(jnp.concatenate([_load_token_major(xbuf_ref, MOE_BLOCK, c, (i % X_SLOTS,))
                                   for c in range(ROW_TILES)], axis=-1))
        glu = _dot(x, w1b_ref[:, :D_FF]) + b1_ref[0][:, :D_FF]
        lin = _dot(x, w1b_ref[:, D_FF:]) + b1_ref[0][:, D_FF:]
        glu = jnp.minimum(glu, SWIGLU_LIMIT)
        lin = jnp.clip(lin, -SWIGLU_LIMIT, SWIGLU_LIMIT)
        act = glu * (1.0 / (1.0 + jnp.exp(-SWIGLU_ALPHA * glu))) * (lin + 1.0)
        _store_token_major(ys_ref, _dot(_bf16(act), w2b_ref[...]) + b2_ref[0])

    @pl.when(jnp.logical_not(used))
    def _():
        ys_ref[...] = jnp.zeros_like(ys_ref)


def _experts(xs, block_e, n_used, next_e, layer, w1, b1, w2, b2):
    grid_spec = pltpu.PrefetchScalarGridSpec(
        num_scalar_prefetch=3,
        grid=(MOE_NBLOCKS,),
        in_specs=[
            pl.BlockSpec(memory_space=pl.ANY),
            pl.BlockSpec((None, 1, 1, 2 * D_FF), lambda i, be, nu, nx: (layer, be[i], 0, 0)),
            pl.BlockSpec((None, 1, 1, D_MODEL), lambda i, be, nu, nx: (layer, be[i], 0, 0)),
            pl.BlockSpec(memory_space=pl.ANY),
            pl.BlockSpec(memory_space=pl.ANY),
        ],
        out_specs=pl.BlockSpec((MOE_BLOCK * ROW_TILES, 128), lambda i, be, nu, nx: (i, 0)),
        scratch_shapes=[pltpu.VMEM((D_MODEL, 2 * D_FF), jnp.float32),
                        pltpu.VMEM((D_FF, D_MODEL), jnp.float32),
                        pltpu.VMEM((D_MODEL, 2 * D_FF), jnp.bfloat16),
                        pltpu.VMEM((D_FF, D_MODEL), jnp.bfloat16),
                        pltpu.SemaphoreType.DMA((2,)),
                        pltpu.VMEM((X_SLOTS, MOE_BLOCK * ROW_TILES, 128), jnp.float32),
                        pltpu.SemaphoreType.DMA((X_SLOTS,))],
    )
    return pl.pallas_call(
        functools.partial(_expert_kernel, layer=layer),
        grid_spec=grid_spec,
        out_shape=jax.ShapeDtypeStruct((MOE_CAP * ROW_TILES, 128), jnp.float32),
        compiler_params=pltpu.CompilerParams(dimension_semantics=("arbitrary",),
                                             vmem_limit_bytes=VMEM_LIMIT_V7X),
        name="moe_experts",
    )(block_e, n_used, next_e, xs, b1.reshape(DEPTH, N_EXPERTS, 1, 2 * D_FF),
      b2.reshape(DEPTH, N_EXPERTS, 1, D_MODEL), w1, w2)


def _combine_kernel(pos_ref, pos_next_ref, x_ref, gates_ref, g2_ref, ys_ref, o_ref, buf_ref, sems):
    step = pl.program_id(0)
    n_steps = pl.num_programs(0)
    slot = step % 2

    def gather(p_ref, dst_slot):
        def issue(i, carry):
            for u in range(ISSUE_UNROLL):
                t = i * ISSUE_UNROLL + u
                for k in range(TOP_K):
                    _row_copy(ys_ref, p_ref[t * TOP_K + k], buf_ref.at[dst_slot, k], t,
                              sems.at[dst_slot]).start(priority=k % 2)
            return carry

        lax.fori_loop(0, COMBINE_TILE // ISSUE_UNROLL, issue, 0)

    @pl.when(step == 0)
    def _():
        gather(pos_ref, 0)

    @pl.when(step + 1 < n_steps)
    def _():
        gather(pos_next_ref, 1 - slot)

    for k in range(TOP_K):
        pltpu.make_async_copy(ys_ref.at[pl.ds(0, COMBINE_TILE * ROW_TILES), :],
                              buf_ref.at[slot, k], sems.at[slot]).wait()

    gates = gates_ref[...]
    gate_cols = [jnp.broadcast_to(gates[:, k:k + 1], (COMBINE_TILE, 128)) for k in range(TOP_K)]
    for c in range(ROW_TILES):
        cols = slice(c * 128, (c + 1) * 128)
        y = gate_cols[0] * _load_token_major(buf_ref, COMBINE_TILE, c, (slot, 0))
        for k in range(1, TOP_K):
            y = y + gate_cols[k] * _load_token_major(buf_ref, COMBINE_TILE, c, (slot, k))
        o_ref[:, cols] = x_ref[:, cols] + g2_ref[0][:, cols] * y


def _combine(x, pos_flat, gates_tok, g2, ys):
    tiles_per_batch = SEQ // COMBINE_TILE
    n_steps = TOKENS // COMBINE_TILE
    return pl.pallas_call(
        _combine_kernel,
        grid=(n_steps,),
        in_specs=[
            pl.BlockSpec((COMBINE_TILE * TOP_K,), lambda i: (i,), memory_space=pltpu.SMEM),
            pl.BlockSpec((COMBINE_TILE * TOP_K,), lambda i: (jnp.minimum(i + 1, n_steps - 1),),
                         memory_space=pltpu.SMEM),
            pl.BlockSpec((COMBINE_TILE, D_MODEL), lambda i: (i, 0)),
            pl.BlockSpec((COMBINE_TILE, TOP_K), lambda i: (i, 0)),
            pl.BlockSpec((1, 1, D_MODEL), lambda i: (i // tiles_per_batch, 0, 0)),
            pl.BlockSpec(memory_space=pl.ANY),
        ],
        out_specs=pl.BlockSpec((COMBINE_TILE, D_MODEL), lambda i: (i, 0)),
        out_shape=jax.ShapeDtypeStruct((TOKENS, D_MODEL), jnp.float32),
        scratch_shapes=[pltpu.VMEM((2, TOP_K, COMBINE_TILE * ROW_TILES, 128), jnp.float32),
                        pltpu.SemaphoreType.DMA((2,))],
        compiler_params=pltpu.CompilerParams(dimension_semantics=("arbitrary",),
                                             vmem_limit_bytes=VMEM_LIMIT_V7X),
        name="moe_combine",
    )(pos_flat, pos_flat, x, gates_tok, g2, ys)


def _moe_layer(x, g, sc, sh, g2, router_w, router_b, layer, w1, b1, w2, b2):
    h, idx, rank, gates, cnt = _router(x, g, sc, sh, router_w, router_b)
    counts = cnt[:, 0].astype(jnp.int32)
    padded = ((counts + MOE_BLOCK - 1) // MOE_BLOCK) * MOE_BLOCK
    pends = jnp.cumsum(padded)
    pstarts = pends - padded
    experts = jnp.arange(N_EXPERTS, dtype=jnp.int32)[:, None, None]
    pos = rank + jnp.sum(jnp.where(idx[None] == experts, pstarts[:, None, None], 0), axis=0)
    pos_flat = pos.T.reshape(-1)
    block_start = jnp.arange(MOE_NBLOCKS, dtype=jnp.int32) * MOE_BLOCK
    block_e = jnp.minimum(jnp.sum(block_start[:, None] >= pends[None, :], axis=1),
                          N_EXPERTS - 1).astype(jnp.int32)
    n_used = (pends[-1:] // MOE_BLOCK).astype(jnp.int32)
    zstart = jnp.where(padded > 0, pends - MOE_BLOCK, -1).astype(jnp.int32)
    ids = jnp.arange(N_EXPERTS, dtype=jnp.int32)
    later_used = jnp.logical_and(ids[None, :] > ids[:, None], (padded > 0)[None, :])
    next_used = jnp.min(jnp.where(later_used, ids[None, :], N_EXPERTS), axis=1)
    next_used = jnp.where(next_used < N_EXPERTS, next_used, -1)
    next_e = jnp.sum(jnp.where(block_e[:, None] == ids[None, :], next_used[None, :], 0),
                     axis=1).astype(jnp.int32)
    xs = _dispatch(h, pos_flat, zstart, n_used)
    ys = _experts(xs, block_e, n_used, next_e, layer, w1, b1, w2, b2)
    return _combine(x, pos_flat, gates.T, g2, ys)


def kernel(x, c, ada_w, ada_b, norm_mix_g, norm_ffn_g, ev_w_in, ev_w_out, sg_ln_g, sg_ln_b, sg_w, sg_b, od_w_in, od_w_out, od_q_g, od_k_g, od_rel_bias, moe_router_w, moe_router_b, moe_w1, moe_b1, moe_w2, moe_b2):
    mod = _ada_mod(c, ada_w, ada_b)
    xt = x.reshape(TOKENS, D_MODEL)
    for l in range(DEPTH):
        sh1, sc1, g1, sh2, sc2, g2 = [m.reshape(BATCH, 1, D_MODEL)
                                      for m in jnp.split(mod[l], 6, axis=-1)]
        i = l // 2
        if l % 2 == 0:
            proj = _norm_proj(xt, norm_mix_g[l], sc1, sh1, _bf16(ev_w_in[i]))
            a_out = _retention(proj)
            b_out = _spatial_gate(proj, sg_ln_g[i], sg_ln_b[i], sg_w[i], sg_b[i])
            xt = _out_proj([a_out, b_out], _bf16(ev_w_out[i]), xt, g1)
        else:
            qkv = _norm_proj(xt, norm_mix_g[l], sc1, sh1, _bf16(od_w_in[i]))
            att = _band_attention(qkv, od_q_g[i], od_k_g[i], od_rel_bias[i])
            xt = _out_proj([att], _bf16(od_w_out[i]), xt, g1)
        xt = _moe_layer(xt, norm_ffn_g[l], sc2, sh2, g2, moe_router_w[l], moe_router_b[l],
                        l, moe_w1, moe_b1, moe_w2, moe_b2)
    return xt.reshape(BATCH, SEQ, D_MODEL)
```
